```python
import jax, jax.numpy as jnp
from jax import lax
import numpy as np

D_MODEL = 1024
BATCH = 8
SEQ = 8192
DEPTH = 4

GRID_W = 64
CTX_LEN = 256
EPS = 1e-6
N_HEADS = 8
N_KV_HEADS = 2
GQA_GROUP = N_HEADS // N_KV_HEADS
HEAD_DIM = 64
ATTN_WIDTH = N_HEADS * HEAD_DIM
KV_WIDTH = N_KV_HEADS * HEAD_DIM
ATTN_SCALE = HEAD_DIM ** -0.5
ROPE_THETA = 10000.0
ROPE_PAIRS_PER_AXIS = HEAD_DIM // 4
Q_BLOCK = 128
CONV_A_WIDTH = D_MODEL // 2
CONV_A_K = 31
EVEN_IN = ATTN_WIDTH + 2 * KV_WIDTH + 2 * CONV_A_WIDTH
EVEN_SPLITS = (ATTN_WIDTH, ATTN_WIDTH + KV_WIDTH, ATTN_WIDTH + 2 * KV_WIDTH,
               ATTN_WIDTH + 2 * KV_WIDTH + CONV_A_WIDTH)
EVEN_OUT = ATTN_WIDTH + CONV_A_WIDTH
SC_WIDTH = D_MODEL
SC_K = 3
N_EXPERTS = 16
N_GROUPS = 4
EXPERTS_PER_GROUP = N_EXPERTS // N_GROUPS
TOP_K = 2
D_EXPERT = 512
MOE_BLOCK = 256
N_EVEN = (DEPTH + 1) // 2
N_ODD = DEPTH // 2
LAST_ATTN_LAYER = (DEPTH - 1) - ((DEPTH - 1) % 2)

kernel_name = "hybrid_dit_conformer_gqa_shortconv_moe"


def rms_norm(x, g):
    xf = x.astype(jnp.float32)
    y = xf * lax.rsqrt(jnp.mean(xf * xf, axis=-1, keepdims=True) + EPS)
    return (y * g.astype(jnp.float32)).astype(x.dtype)


def layer_norm(x, g, b):
    xf = x.astype(jnp.float32)
    mu = jnp.mean(xf, axis=-1, keepdims=True)
    var = jnp.mean(jnp.square(xf - mu), axis=-1, keepdims=True)
    y = (xf - mu) * lax.rsqrt(var + EPS)
    return (y * g.astype(jnp.float32) + b.astype(jnp.float32)).astype(x.dtype)


def modulate(h, shift, scale):
    return h * (1 + scale) + shift


def depthwise_conv(x, w, pad):
    return lax.conv_general_dilated(x, w[:, None, :], window_strides=(1,), padding=[(pad, pad)],
                                    dimension_numbers=("NWC", "WIO", "NWC"),
                                    feature_group_count=x.shape[-1])


def axial_rope_tables(n_tokens):
    rows = n_tokens // GRID_W
    row = jnp.repeat(jnp.arange(rows, dtype=jnp.float32), GRID_W)
    col = jnp.tile(jnp.arange(GRID_W, dtype=jnp.float32), rows)
    inv = ROPE_THETA ** (-jnp.arange(ROPE_PAIRS_PER_AXIS, dtype=jnp.float32) / ROPE_PAIRS_PER_AXIS)
    ang = jnp.concatenate([row[:, None] * inv, col[:, None] * inv], axis=-1)
    return jnp.cos(ang), jnp.sin(ang)


def apply_rope(x, cos, sin):
    bshape = (1, cos.shape[0]) + (1,) * (x.ndim - 3) + (cos.shape[1],)
    cs, sn = cos.reshape(bshape), sin.reshape(bshape)
    xf = x.astype(jnp.float32).reshape(x.shape[:-1] + (HEAD_DIM // 2, 2))
    x1, x2 = xf[..., 0], xf[..., 1]
    out = jnp.stack([x1 * cs - x2 * sn, x1 * sn + x2 * cs], axis=-1)
    return out.reshape(x.shape).astype(x.dtype)


def gqa_attend(q, k, v):
    s = jnp.einsum("bqhgd,bkhd->bhgqk", q * ATTN_SCALE, k).astype(jnp.float32)
    p = jax.nn.softmax(s, axis=-1).astype(v.dtype)
    return jnp.einsum("bhgqk,bkhd->bqhgd", p, v)


def split_even(p, q_g, k_g):
    b, l, _ = p.shape
    q, k, v, a, a_gate = jnp.split(p, EVEN_SPLITS, axis=-1)
    q = rms_norm(q.reshape(b, l, N_KV_HEADS, GQA_GROUP, HEAD_DIM), q_g)
    k = rms_norm(k.reshape(b, l, N_KV_HEADS, HEAD_DIM), k_g)
    v = v.reshape(b, l, N_KV_HEADS, HEAD_DIM)
    return q, k, v, a * jax.nn.sigmoid(a_gate)


def conformer_conv(a, dw_w, dw_b, ln_g, ln_b):
    y = depthwise_conv(a, dw_w, CONV_A_K // 2) + dw_b
    return jax.nn.silu(layer_norm(y, ln_g, ln_b))


def even_mixer(hx, hz, w_in, q_g, k_g, dw_w, dw_b, ln_g, ln_b, w_out, cos, sin, ctx_out):
    b, s, _ = hx.shape
    qx, kx, vx, ax = split_even(hx @ w_in, q_g, k_g)
    qx = apply_rope(qx, cos, sin)
    kx = apply_rope(kx, cos, sin)
    if ctx_out:
        qz, kz, vz, az = split_even(hz @ w_in, q_g, k_g)
    else:
        kz, vz = jnp.split(hz @ w_in[:, ATTN_WIDTH:ATTN_WIDTH + 2 * KV_WIDTH], 2, axis=-1)
        kz = rms_norm(kz.reshape(b, -1, N_KV_HEADS, HEAD_DIM), k_g)
        vz = vz.reshape(b, -1, N_KV_HEADS, HEAD_DIM)
    k_all = jnp.concatenate([kx, kz], axis=1)
    v_all = jnp.concatenate([vx, vz], axis=1)
    n_blk = s // Q_BLOCK
    q_blocks = qx.reshape(b, n_blk, Q_BLOCK, N_KV_HEADS, GQA_GROUP, HEAD_DIM).swapaxes(0, 1)
    attn_x = lax.map(lambda qb: gqa_attend(qb, k_all, v_all), q_blocks)
    attn_x = attn_x.swapaxes(0, 1).reshape(b, s, ATTN_WIDTH)
    ox = jnp.concatenate([attn_x, conformer_conv(ax, dw_w, dw_b, ln_g, ln_b)], axis=-1) @ w_out
    if not ctx_out:
        return ox, None
    attn_z = gqa_attend(qz, kz, vz).reshape(b, -1, ATTN_WIDTH)
    oz = jnp.concatenate([attn_z, conformer_conv(az, dw_w, dw_b, ln_g, ln_b)], axis=-1) @ w_out
    return ox, oz


def short_conv_mixer(h, w_in, conv_w, w_out):
    gb, gc, v = jnp.split(h @ w_in, 3, axis=-1)
    return (gb * depthwise_conv(gc * v, conv_w, SC_K // 2)) @ w_out


def route(h, router_w, router_bias):
    aff = jax.nn.sigmoid((h @ router_w).astype(jnp.float32))
    sel = (aff + router_bias.astype(jnp.float32)).reshape(-1, N_GROUPS, EXPERTS_PER_GROUP)
    grp_score = lax.top_k(sel, 2)[0].sum(-1)
    grp = jnp.argmax(grp_score, axis=-1)
    sel_in = jnp.take_along_axis(sel, grp[:, None, None], axis=1)[:, 0]
    _, local = lax.top_k(sel_in, TOP_K)
    e_idx = (grp[:, None] * EXPERTS_PER_GROUP + local).astype(jnp.int32)
    w = jnp.take_along_axis(aff, e_idx, axis=1)
    return e_idx, w / w.sum(-1, keepdims=True)


def moe_ffn(h, router_w, router_bias, w_gate, w_up, w_down):
    n_tok = h.shape[0]
    n_assign = n_tok * TOP_K
    e_idx, e_w = route(h, router_w, router_bias)
    flat_e = e_idx.reshape(-1)
    order = jnp.argsort(flat_e)
    counts = jnp.bincount(flat_e, length=N_EXPERTS).astype(jnp.int32)
    padded = (counts + MOE_BLOCK - 1) // MOE_BLOCK * MOE_BLOCK
    starts = jnp.cumsum(counts) - counts
    pad_ends = jnp.cumsum(padded)
    pad_starts = pad_ends - padded
    sorted_e = flat_e[order]
    dest_sorted = pad_starts[sorted_e] + jnp.arange(n_assign, dtype=jnp.int32) - starts[sorted_e]
    dest = jnp.zeros((n_assign,), jnp.int32).at[order].set(dest_sorted.astype(jnp.int32))
    n_blocks = -(-n_assign // MOE_BLOCK) + N_EXPERTS
    n_rows = n_blocks * MOE_BLOCK
    row_token = jnp.full((n_rows,), n_tok, jnp.int32).at[dest].set(
        jnp.arange(n_assign, dtype=jnp.int32) // TOP_K)
    block_e = jnp.minimum(jnp.searchsorted(pad_ends, jnp.arange(n_blocks, dtype=jnp.int32) * MOE_BLOCK,
                                           side="right"), N_EXPERTS - 1)

    def expert_block(args):
        tok, e = args
        xb = jnp.take(h, tok, axis=0, mode="fill", fill_value=0)
        return (jax.nn.silu(xb @ w_gate[e]) * (xb @ w_up[e])) @ w_down[e]

    out = lax.map(expert_block, (row_token.reshape(n_blocks, MOE_BLOCK), block_e)).reshape(n_rows, -1)
    y = out[dest].reshape(n_tok, TOP_K, -1) * e_w[..., None].astype(h.dtype)
    return y.sum(axis=1)


def setup_inputs(seed: int = 0) -> dict:
    key = jax.random.key(seed)
    ks = jax.random.split(key, 24)
    D = D_MODEL

    def nrm(k, shape, scale):
        return jax.random.normal(k, shape, jnp.float32) * scale

    return {
        "x": nrm(ks[0], (BATCH, SEQ, D), 1.0),
        "c": nrm(ks[1], (BATCH, D), 1.0),
        "ctx": nrm(ks[2], (BATCH, CTX_LEN, D), 1.0),
        "c_ctx": nrm(ks[3], (D,), 1.0),
        "w_mod": nrm(ks[4], (DEPTH, D, 6 * D), 0.5 * D ** -0.5),
        "b_mod": nrm(ks[5], (DEPTH, 6 * D), 0.02),
        "norm_mix_g": 1.0 + nrm(ks[6], (DEPTH, D), 0.02),
        "norm_ffn_g": 1.0 + nrm(ks[7], (DEPTH, D), 0.02),
        "ev_w_in": nrm(ks[8], (N_EVEN, D, EVEN_IN), D ** -0.5),
        "ev_q_norm_g": 1.0 + nrm(ks[9], (N_EVEN, HEAD_DIM), 0.02),
        "ev_k_norm_g": 1.0 + nrm(ks[10], (N_EVEN, HEAD_DIM), 0.02),
        "ev_dw_w": nrm(ks[11], (N_EVEN, CONV_A_K, CONV_A_WIDTH), CONV_A_K ** -0.5),
        "ev_dw_b": nrm(ks[12], (N_EVEN, CONV_A_WIDTH), 0.02),
        "ev_ln_g": 1.0 + nrm(ks[13], (N_EVEN, CONV_A_WIDTH), 0.02),
        "ev_ln_b": nrm(ks[14], (N_EVEN, CONV_A_WIDTH), 0.02),
        "ev_w_out": nrm(ks[15], (N_EVEN, EVEN_OUT, D), EVEN_OUT ** -0.5),
        "od_w_in": nrm(ks[16], (N_ODD, D, 3 * SC_WIDTH), D ** -0.5),
        "od_conv_w": nrm(ks[17], (N_ODD, SC_K, SC_WIDTH), SC_K ** -0.5),
        "od_w_out": nrm(ks[18], (N_ODD, SC_WIDTH, D), SC_WIDTH ** -0.5),
        "router_w": nrm(ks[19], (D, N_EXPERTS), D ** -0.5),
        "router_bias": nrm(ks[20], (N_EXPERTS,), 0.01),
        "moe_w_gate": nrm(ks[21], (DEPTH, N_EXPERTS, D, D_EXPERT), D ** -0.5),
        "moe_w_up": nrm(ks[22], (DEPTH, N_EXPERTS, D, D_EXPERT), D ** -0.5),
        "moe_w_down": nrm(ks[23], (DEPTH, N_EXPERTS, D_EXPERT, D), D_EXPERT ** -0.5),
    }


def reference(x, c, ctx, c_ctx, w_mod, b_mod, norm_mix_g, norm_ffn_g, ev_w_in, ev_q_norm_g,
              ev_k_norm_g, ev_dw_w, ev_dw_b, ev_ln_g, ev_ln_b, ev_w_out, od_w_in, od_conv_w,
              od_w_out, router_w, router_bias, moe_w_gate, moe_w_up, moe_w_down):
    b, s, d = x.shape
    cos, sin = axial_rope_tables(s)
    z = ctx
    c_act = jax.nn.silu(c)
    cz_act = jax.nn.silu(c_ctx)
    n_lat = b * s
    for l in range(DEPTH):
        i = l // 2
        is_attn = l % 2 == 0
        ctx_in = l <= LAST_ATTN_LAYER
        ctx_out = l < LAST_ATTN_LAYER
        mx = jnp.split(c_act @ w_mod[l] + b_mod[l], 6, axis=-1)
        hx = modulate(rms_norm(x, norm_mix_g[l]), mx[0][:, None], mx[1][:, None])
        hz = None
        if ctx_in:
            mz = jnp.split(cz_act @ w_mod[l] + b_mod[l], 6, axis=-1)
            hz = modulate(rms_norm(z, norm_mix_g[l]), mz[0], mz[1])
        if is_attn:
            ox, oz = even_mixer(hx, hz, ev_w_in[i], ev_q_norm_g[i], ev_k_norm_g[i], ev_dw_w[i],
                                ev_dw_b[i], ev_ln_g[i], ev_ln_b[i], ev_w_out[i], cos, sin, ctx_out)
        else:
            ox = short_conv_mixer(hx, od_w_in[i], od_conv_w[i], od_w_out[i])
            oz = short_conv_mixer(hz, od_w_in[i], od_conv_w[i], od_w_out[i]) if ctx_out else None
        x = x + mx[2][:, None] * ox
        hx = modulate(rms_norm(x, norm_ffn_g[l]), mx[3][:, None], mx[4][:, None])
        if ctx_out:
            z = z + mz[2] * oz
            hz = modulate(rms_norm(z, norm_ffn_g[l]), mz[3], mz[4])
            tokens = jnp.concatenate([hx.reshape(-1, d), hz.reshape(-1, d)], axis=0)
        else:
            tokens = hx.reshape(-1, d)
        y = moe_ffn(tokens, router_w, router_bias, moe_w_gate[l], moe_w_up[l], moe_w_down[l])
        x = x + mx[5][:, None] * y[:n_lat].reshape(b, s, d)
        if ctx_out:
            z = z + mz[5] * y[n_lat:].reshape(z.shape)
    return x
```

```python
import functools
import math

import numpy as np
import jax
import jax.numpy as jnp
from jax import lax
from jax.experimental import pallas as pl
from jax.experimental.pallas import tpu as pltpu

F32 = jnp.float32
BF16 = jnp.bfloat16
U32 = jnp.uint32

GRID_W = 64
EPS = 1e-6
N_HEADS = 8
N_KV_HEADS = 2
GQA_GROUP = N_HEADS // N_KV_HEADS
HEAD_DIM = 64
ATTN_WIDTH = N_HEADS * HEAD_DIM
KV_WIDTH = N_KV_HEADS * HEAD_DIM
QK_WIDTH = ATTN_WIDTH + KV_WIDTH
ATTN_SCALE = HEAD_DIM ** -0.5
ROPE_THETA = 10000.0
ROPE_PAIRS_PER_AXIS = HEAD_DIM // 4
CONV_A_K = 31
SC_K = 3
N_EXPERTS = 16
N_GROUPS = 4
EXPERTS_PER_GROUP = N_EXPERTS // N_GROUPS
D_EXPERT = 512
PAIR_LO = (0, 0, 0, 1, 1, 2)
PAIR_HI = (1, 2, 3, 2, 3, 3)
N_PAIRS = len(PAIR_LO)
N_CLASSES = N_GROUPS * N_PAIRS
CLASS_ROWS = 32

LANES = 128
HALO = 16
VMEM_LIMIT = 48 * 1024 * 1024
ROW_TILE = 512
Q_TILE = 256
MOE_BLOCK = 256
CONV_ROWS = 64
LOG2E = math.log2(math.e)


def _cparams(*sem):
    return pltpu.CompilerParams(dimension_semantics=sem, vmem_limit_bytes=VMEM_LIMIT)


def _sigmoid(x):
    return 1.0 / (1.0 + jnp.exp(-x))


def _rms_mod(x, g, shift, scale):
    y = x * lax.rsqrt(jnp.mean(x * x, axis=-1, keepdims=True) + EPS) * g
    return y * (1.0 + scale) + shift


def _pack_halves(h):
    w = h.shape[1] // 2
    lo = lax.bitcast_convert_type(h[:, :w].astype(BF16).astype(F32), U32)
    hi = lax.bitcast_convert_type(h[:, w:].astype(BF16).astype(F32), U32)
    return (lo >> 16) | (hi & jnp.uint32(0xFFFF0000))


def _unpack_halves(w):
    lo = lax.bitcast_convert_type(w << 16, F32)
    hi = lax.bitcast_convert_type(w & jnp.uint32(0xFFFF0000), F32)
    return lo, hi


def _mod_kernel(c_ref, w_ref, b_ref, o_ref):
    c = c_ref[...]
    a = (c * _sigmoid(c)).astype(BF16)
    o_ref[...] = jnp.dot(a, w_ref[...].astype(BF16), preferred_element_type=F32) + b_ref[...]


def _modulation(c_all, w_mod, b_mod):
    depth, d, n = w_mod.shape
    r = c_all.shape[0]
    nt = n // d
    return pl.pallas_call(
        _mod_kernel,
        out_shape=jax.ShapeDtypeStruct((depth, r, n), F32),
        grid=(depth, nt),
        in_specs=[pl.BlockSpec((r, d), lambda l, j: (0, 0)),
                  pl.BlockSpec((None, d, d), lambda l, j: (l, 0, j)),
                  pl.BlockSpec((None, 1, d), lambda l, j: (l, 0, j))],
        out_specs=pl.BlockSpec((None, r, d), lambda l, j: (l, 0, j)),
        compiler_params=_cparams("parallel", "parallel"),
        name="modulation",
    )(c_all, w_mod, b_mod.reshape(depth, 1, n))


def _inproj_even_kernel(x_ref, g_ref, sh_ref, sc_ref, w_ref, gqk_ref, cos_ref, sin_ref, gm_ref,
                        q_ref, k_ref, v_ref, a_ref, *, rope):
    tm = x_ref.shape[0]
    h = _rms_mod(x_ref[...], g_ref[...], sh_ref[...], sc_ref[...]).astype(BF16)
    p = jnp.dot(h, w_ref[...], preferred_element_type=F32)
    lane = lax.broadcasted_iota(jnp.int32, (tm, LANES), 1)
    first_head = lane < HEAD_DIM
    first_half = (lane % HEAD_DIM) < (HEAD_DIM // 2)
    for m in range(QK_WIDTH // LANES):
        y = p[:, LANES * m:LANES * (m + 1)]
        sq = y * y
        sq_hi = sq.astype(BF16)
        sq_lo = (sq - sq_hi.astype(F32)).astype(BF16)
        ms = jnp.dot(jnp.concatenate([sq_hi, sq_lo], axis=1), gm_ref[...], preferred_element_type=F32)
        y = y * lax.rsqrt(ms + EPS) * gqk_ref[:, LANES * m:LANES * (m + 1)]
        if rope:
            partner = jnp.where(first_half, pltpu.roll(y, LANES - HEAD_DIM // 2, 1),
                                pltpu.roll(y, HEAD_DIM // 2, 1))
            y = y * cos_ref[...] + partner * sin_ref[...]
        if m < ATTN_WIDTH // LANES:
            q_ref[m] = jnp.where(first_head, y, 0.0).astype(BF16)
            q_ref[m + ATTN_WIDTH // LANES] = jnp.where(first_head, 0.0, y).astype(BF16)
        else:
            k_ref[...] = y.astype(BF16)
    v = p[:, QK_WIDTH:QK_WIDTH + KV_WIDTH]
    v_ref[0] = jnp.where(first_head, v, 1.0).astype(BF16)
    v_ref[1] = jnp.where(first_head, 1.0, v).astype(BF16)
    c0 = QK_WIDTH + KV_WIDTH
    cw = (p.shape[1] - c0) // 2
    a_ref[...] = (p[:, c0:c0 + cw] * _sigmoid(p[:, c0 + cw:])).astype(BF16)


def _inproj_even(x, g, shift, scale, rows_per_mod, w, gqk, cos_t, sin_t, gmat, tm, rope):
    t, d = x.shape
    n = w.shape[1]
    cw = (n - QK_WIDTH - KV_WIDTH) // 2
    n_pos = cos_t.shape[0] // tm
    mod_map = lambda i: ((i * tm) // rows_per_mod, 0, 0)
    return pl.pallas_call(
        functools.partial(_inproj_even_kernel, rope=rope),
        out_shape=(jax.ShapeDtypeStruct((N_HEADS, t, LANES), BF16),
                   jax.ShapeDtypeStruct((t, KV_WIDTH), BF16),
                   jax.ShapeDtypeStruct((N_KV_HEADS, t, LANES), BF16),
                   jax.ShapeDtypeStruct((t, cw), BF16)),
        grid=(t // tm,),
        in_specs=[pl.BlockSpec((tm, d), lambda i: (i, 0)),
                  pl.BlockSpec((1, d), lambda i: (0, 0)),
                  pl.BlockSpec((None, 1, d), mod_map),
                  pl.BlockSpec((None, 1, d), mod_map),
                  pl.BlockSpec((d, n), lambda i: (0, 0)),
                  pl.BlockSpec((1, QK_WIDTH), lambda i: (0, 0)),
                  pl.BlockSpec((tm, LANES), lambda i: (i % n_pos, 0)),
                  pl.BlockSpec((tm, LANES), lambda i: (i % n_pos, 0)),
                  pl.BlockSpec((2 * LANES, LANES), lambda i: (0, 0))],
        out_specs=(pl.BlockSpec((N_HEADS, tm, LANES), lambda i: (0, i, 0)),
                   pl.BlockSpec((tm, KV_WIDTH), lambda i: (i, 0)),
                   pl.BlockSpec((N_KV_HEADS, tm, LANES), lambda i: (0, i, 0)),
                   pl.BlockSpec((tm, cw), lambda i: (i, 0))),
        compiler_params=_cparams("parallel"),
        name="inproj_even",
    )(x, g, shift, scale, w, gqk, cos_t, sin_t, gmat)


def _inproj_odd_kernel(x_ref, g_ref, sh_ref, sc_ref, w_ref, gb_ref, u_ref):
    h = _rms_mod(x_ref[...], g_ref[...], sh_ref[...], sc_ref[...]).astype(BF16)
    p = jnp.dot(h, w_ref[...], preferred_element_type=F32)
    w = p.shape[1] // 3
    gb_ref[...] = p[:, :w].astype(BF16)
    u_ref[...] = (p[:, w:2 * w] * p[:, 2 * w:]).astype(BF16)


def _inproj_odd(x, g, shift, scale, rows_per_mod, w, tm):
    t, d = x.shape
    n = w.shape[1]
    mod_map = lambda i: ((i * tm) // rows_per_mod, 0, 0)
    return pl.pallas_call(
        _inproj_odd_kernel,
        out_shape=(jax.ShapeDtypeStruct((t, n // 3), BF16), jax.ShapeDtypeStruct((t, n // 3), BF16)),
        grid=(t // tm,),
        in_specs=[pl.BlockSpec((tm, d), lambda i: (i, 0)),
                  pl.BlockSpec((1, d), lambda i: (0, 0)),
                  pl.BlockSpec((None, 1, d), mod_map),
                  pl.BlockSpec((None, 1, d), mod_map),
                  pl.BlockSpec((d, n), lambda i: (0, 0))],
        out_specs=(pl.BlockSpec((tm, n // 3), lambda i: (i, 0)),
                   pl.BlockSpec((tm, n // 3), lambda i: (i, 0))),
        compiler_params=_cparams("parallel"),
        name="inproj_odd",
    )(x, g, shift, scale, w)


def _attn_kernel(q_ref, k_ref, v_ref, o_ref, acc_ref):
    tq = q_ref.shape[1]
    n_pair = N_HEADS // N_KV_HEADS
    lane = lax.broadcasted_iota(jnp.int32, (tq, LANES), 1)

    def pair(m, carry):
        outs = []
        for g in range(N_KV_HEADS):
            q = q_ref[m + n_pair * g]
            s = lax.dot_general(q, k_ref[...], (((1,), (1,)), ((), ())), preferred_element_type=F32)
            p = jnp.exp2(s - jnp.max(s, axis=-1, keepdims=True)).astype(BF16)
            o = jnp.dot(p, v_ref[g], preferred_element_type=F32)
            l = o[:, HEAD_DIM:HEAD_DIM + 1] if g == 0 else o[:, 0:1]
            outs.append(o / l)
        acc_ref[m] = jnp.where(lane < HEAD_DIM, outs[0], outs[1])
        return carry

    lax.fori_loop(0, n_pair, pair, 0)
    o_ref[...] = jnp.concatenate([acc_ref[m] for m in range(n_pair)], axis=1).astype(BF16)


def _attention(q, k_all, v_all, s_q, tq):
    t = q.shape[1]
    b, skv, _ = k_all.shape
    nq = s_q // tq
    return pl.pallas_call(
        _attn_kernel,
        out_shape=jax.ShapeDtypeStruct((t, ATTN_WIDTH), BF16),
        grid=(b, nq),
        in_specs=[pl.BlockSpec((N_HEADS, tq, LANES), lambda bi, i: (0, bi * nq + i, 0)),
                  pl.BlockSpec((None, skv, LANES), lambda bi, i: (bi, 0, 0)),
                  pl.BlockSpec((N_KV_HEADS, None, skv, LANES), lambda bi, i: (0, bi, 0, 0))],
        out_specs=pl.BlockSpec((tq, ATTN_WIDTH), lambda bi, i: (bi * nq + i, 0)),
        scratch_shapes=[pltpu.VMEM((N_HEADS // N_KV_HEADS, tq, LANES), F32)],
        compiler_params=_cparams("parallel", "parallel"),
        name="attention",
    )(q, k_all, v_all)


def _fill_window(win_ref, cur_ref, prev_ref, next_ref, is_first, is_last):
    tm = cur_ref.shape[0]
    win_ref[0:HALO, :] = jnp.where(is_first, 0.0, prev_ref[...].astype(F32))
    win_ref[HALO:HALO + tm, :] = cur_ref[...].astype(F32)
    win_ref[HALO + tm:2 * HALO + tm, :] = jnp.where(is_last, 0.0, next_ref[...].astype(F32))


def _dwconv_chunk(win_ref, w_ref, r0, rows, taps):
    off = HALO - taps // 2
    acc = win_ref[r0 + off:r0 + off + rows, :] * w_ref[0:1, :]
    for kk in range(1, taps):
        acc = acc + win_ref[r0 + off + kk:r0 + off + kk + rows, :] * w_ref[kk:kk + 1, :]
    return acc


def _residual_tail(x_ref, o, gate_ref, g2_ref, sh2_ref, sc2_ref, xo_ref, hp_ref):
    x_new = x_ref[...] + gate_ref[...] * o
    xo_ref[...] = x_new
    hp_ref[...] = _pack_halves(_rms_mod(x_new, g2_ref[...], sh2_ref[...], sc2_ref[...]))


def _outproj_even_kernel(x_ref, at_ref, a_ref, ap_ref, an_ref, dw_ref, db_ref, lg_ref, lb_ref, w_ref,
                         gate_ref, g2_ref, sh2_ref, sc2_ref, xo_ref, hp_ref, win_ref, cv_ref,
                         *, tiles_per_seq):
    tm = x_ref.shape[0]
    i = pl.program_id(0) % tiles_per_seq
    _fill_window(win_ref, a_ref, ap_ref, an_ref, i == 0, i == tiles_per_seq - 1)
    for r0 in range(0, tm, CONV_ROWS):
        y = _dwconv_chunk(win_ref, dw_ref, r0, CONV_ROWS, CONV_A_K) + db_ref[...]
        mu = jnp.mean(y, axis=-1, keepdims=True)
        yc = y - mu
        var = jnp.mean(yc * yc, axis=-1, keepdims=True)
        yn = yc * lax.rsqrt(var + EPS) * lg_ref[...] + lb_ref[...]
        cv_ref[r0:r0 + CONV_ROWS, :] = (yn * _sigmoid(yn)).astype(BF16)
    aw = at_ref.shape[1]
    o = (jnp.dot(at_ref[...], w_ref[0:aw, :], preferred_element_type=F32)
         + jnp.dot(cv_ref[...], w_ref[aw:, :], preferred_element_type=F32))
    _residual_tail(x_ref, o, gate_ref, g2_ref, sh2_ref, sc2_ref, xo_ref, hp_ref)


def _outproj_odd_kernel(x_ref, gb_ref, u_ref, up_ref, un_ref, cw_ref, w_ref,
                        gate_ref, g2_ref, sh2_ref, sc2_ref, xo_ref, hp_ref, win_ref, cv_ref,
                        *, tiles_per_seq):
    tm = x_ref.shape[0]
    i = pl.program_id(0) % tiles_per_seq
    _fill_window(win_ref, u_ref, up_ref, un_ref, i == 0, i == tiles_per_seq - 1)
    for r0 in range(0, tm, CONV_ROWS):
        y = _dwconv_chunk(win_ref, cw_ref, r0, CONV_ROWS, SC_K)
        cv_ref[r0:r0 + CONV_ROWS, :] = (gb_ref[r0:r0 + CONV_ROWS, :].astype(F32) * y).astype(BF16)
    o = jnp.dot(cv_ref[...], w_ref[...], preferred_element_type=F32)
    _residual_tail(x_ref, o, gate_ref, g2_ref, sh2_ref, sc2_ref, xo_ref, hp_ref)


def _halo_specs(tm, width, n_rows):
    hb = tm // HALO
    last = n_rows // HALO - 1
    return [pl.BlockSpec((tm, width), lambda i: (i, 0)),
            pl.BlockSpec((HALO, width), lambda i: (jnp.maximum(i * hb - 1, 0), 0)),
            pl.BlockSpec((HALO, width), lambda i: (jnp.minimum((i + 1) * hb, last), 0))]


def _outproj(kernel_fn, name, x, lead, conv_in, params, w, gate, g2, shift2, scale2, rows_per_mod,
             seq_len, tm):
    t, d = x.shape
    cw = conv_in.shape[1]
    mod_map = lambda i: ((i * tm) // rows_per_mod, 0, 0)
    row_spec = lambda a: pl.BlockSpec((tm, a.shape[1]), lambda i: (i, 0))
    full_spec = lambda a: pl.BlockSpec(a.shape, lambda i: (0,) * a.ndim)
    in_specs = ([row_spec(x)] + [row_spec(a) for a in lead] + _halo_specs(tm, cw, t)
                + [full_spec(a) for a in params] + [full_spec(w)]
                + [pl.BlockSpec((None, 1, d), mod_map), full_spec(g2),
                   pl.BlockSpec((None, 1, d), mod_map), pl.BlockSpec((None, 1, d), mod_map)])
    return pl.pallas_call(
        functools.partial(kernel_fn, tiles_per_seq=seq_len // tm),
        out_shape=(jax.ShapeDtypeStruct((t, d), F32), jax.ShapeDtypeStruct((t, d // 2), U32)),
        grid=(t // tm,),
        in_specs=in_specs,
        out_specs=(pl.BlockSpec((tm, d), lambda i: (i, 0)), pl.BlockSpec((tm, d // 2), lambda i: (i, 0))),
        scratch_shapes=[pltpu.VMEM((tm + 2 * HALO, cw), F32), pltpu.VMEM((tm, cw), BF16)],
        compiler_params=_cparams("parallel"),
        name=name,
    )(x, *lead, conv_in, conv_in, conv_in, *params, w, gate, g2, shift2, scale2)


def _router_kernel(hp_ref, rw_ref, bias_ref, tri_ref, route_ref, cnt_ref, base_ref):
    tm = hp_ref.shape[0]

    @pl.when(pl.program_id(0) == 0)
    def _():
        base_ref[...] = jnp.zeros_like(base_ref)

    lo, hi = _unpack_halves(hp_ref[...])
    h = jnp.concatenate([lo, hi], axis=1).astype(BF16)
    logits = jnp.dot(h, rw_ref[...], preferred_element_type=F32)
    lt = logits.T[0:N_EXPERTS, :]
    aff = _sigmoid(lt)
    sel = aff + bias_ref[...]
    rows = lambda a, e: a[e:e + 1, :]

    best = None
    for g in range(N_GROUPS):
        a, b, c, d = (rows(sel, EXPERTS_PER_GROUP * g + j) for j in range(EXPERTS_PER_GROUP))
        score = jnp.maximum(jnp.maximum(a + b, c + d), jnp.maximum(a, b) + jnp.maximum(c, d))
        if best is None:
            best, grp = score, jnp.zeros((1, tm), jnp.int32)
        else:
            better = score > best
            grp = jnp.where(better, g, grp)
            best = jnp.where(better, score, best)

    def in_group(a, j):
        out = rows(a, j)
        for g in range(1, N_GROUPS):
            out = jnp.where(grp == g, rows(a, EXPERTS_PER_GROUP * g + j), out)
        return out

    s = [in_group(sel, j) for j in range(EXPERTS_PER_GROUP)]
    f = [in_group(aff, j) for j in range(EXPERTS_PER_GROUP)]

    def first_argmax(vals):
        m = functools.reduce(jnp.maximum, vals)
        idx = jnp.full((1, tm), len(vals) - 1, jnp.int32)
        for j in range(len(vals) - 2, -1, -1):
            idx = jnp.where(vals[j] == m, j, idx)
        return idx

    i1 = first_argmax(s)
    i2 = first_argmax([jnp.where(i1 == j, -jnp.inf, s[j]) for j in range(EXPERTS_PER_GROUP)])
    e_lo = jnp.minimum(i1, i2)
    e_hi = jnp.maximum(i1, i2)
    pick = lambda vals, idx: functools.reduce(
        lambda acc, j: jnp.where(idx == j, vals[j], acc), range(1, len(vals)), vals[0])
    f_lo, f_hi = pick(f, e_lo), pick(f, e_hi)
    denom = f_lo + f_hi
    pair = jnp.where(e_lo == 0, e_hi - 1, jnp.where(e_lo == 1, e_hi + 1, N_PAIRS - 1))
    cls = grp * N_PAIRS + pair

    onehot = (lax.broadcasted_iota(jnp.int32, (CLASS_ROWS, tm), 0) == cls)
    onehot_f = onehot.astype(F32)
    before = jnp.dot(onehot_f.astype(BF16), tri_ref[...], preferred_element_type=F32) + base_ref[:, 0:1]
    rank = jnp.sum(onehot_f * before, axis=0, keepdims=True)
    base_ref[...] = base_ref[...] + jnp.sum(onehot_f, axis=1, keepdims=True)

    route_ref[...] = jnp.concatenate(
        [cls.astype(F32), rank, f_lo / denom, f_hi / denom, jnp.zeros((4, tm), F32)], axis=0)
    cnt_ref[...] = base_ref[...]


def _router(hp, rw, bias, tri, tm):
    t, hw = hp.shape
    return pl.pallas_call(
        _router_kernel,
        out_shape=(jax.ShapeDtypeStruct((8, t), F32), jax.ShapeDtypeStruct((CLASS_ROWS, LANES), F32)),
        grid=(t // tm,),
        in_specs=[pl.BlockSpec((tm, hw), lambda i: (i, 0)),
                  pl.BlockSpec(rw.shape, lambda i: (0, 0)),
                  pl.BlockSpec(bias.shape, lambda i: (0, 0)),
                  pl.BlockSpec((tm, tm), lambda i: (0, 0))],
        out_specs=(pl.BlockSpec((8, tm), lambda i: (0, i)),
                   pl.BlockSpec((CLASS_ROWS, LANES), lambda i: (0, 0))),
        scratch_shapes=[pltpu.VMEM((CLASS_ROWS, LANES), F32)],
        compiler_params=_cparams("arbitrary"),
        name="router",
    )(hp, rw, bias, tri)


def _dispatch_kernel(pos_ref, hp_ref, zero_ref, hs_ref, sem):
    del zero_ref
    tm = hp_ref.shape[0]
    base = pl.program_id(0) * tm

    def row_copy(r):
        return pltpu.make_async_copy(hp_ref.at[pl.ds(r, 1)], hs_ref.at[pl.ds(pos_ref[base + r], 1)], sem)

    def issue(r, carry):
        row_copy(r).start()
        return carry

    lax.fori_loop(0, tm, issue, 0, unroll=8)
    pltpu.make_async_copy(hp_ref, hs_ref.at[pl.ds(0, tm)], sem).wait()


def _dispatch(pos, hp, n_rows, tm):
    t, hw = hp.shape
    zeros = jnp.zeros((n_rows, hw), U32)
    return pl.pallas_call(
        _dispatch_kernel,
        out_shape=jax.ShapeDtypeStruct((n_rows, hw), U32),
        grid_spec=pltpu.PrefetchScalarGridSpec(
            num_scalar_prefetch=1,
            grid=(t // tm,),
            in_specs=[pl.BlockSpec((tm, hw), lambda i, pos: (i, 0)),
                      pl.BlockSpec(memory_space=pl.ANY)],
            out_specs=pl.BlockSpec(memory_space=pl.ANY),
            scratch_shapes=[pltpu.SemaphoreType.DMA(())]),
        input_output_aliases={2: 0},
        compiler_params=_cparams("arbitrary"),
        name="dispatch",
    )(pos, hp, zeros)


def _moe_kernel(elo_ref, ehi_ref, nused_ref, hs_ref, wgu_lo, wd_lo, wgu_hi, wd_hi, ys_ref):
    del elo_ref, ehi_ref

    @pl.when(pl.program_id(0) >= nused_ref[0])
    def _():
        ys_ref[...] = jnp.zeros_like(ys_ref)

    @pl.when(pl.program_id(0) < nused_ref[0])
    def _():
        lo, hi = _unpack_halves(hs_ref[...])
        h = jnp.concatenate([lo, hi], axis=1).astype(BF16)

        def expert(wgu_ref, wd_ref):
            gu = jnp.dot(h, wgu_ref[...], preferred_element_type=F32)
            g, u = gu[:, :D_EXPERT], gu[:, D_EXPERT:]
            act = (g * _sigmoid(g) * u).astype(BF16)
            return jnp.dot(act, wd_ref[...], preferred_element_type=F32)

        ya = lax.bitcast_convert_type(expert(wgu_lo, wd_lo).astype(BF16).astype(F32), U32)
        yb = lax.bitcast_convert_type(expert(wgu_hi, wd_hi).astype(BF16).astype(F32), U32)
        ys_ref[...] = (ya >> 16) | (yb & jnp.uint32(0xFFFF0000))


def _moe(e_lo, e_hi, n_used, hs, wgu, wd):
    n_rows, hw = hs.shape
    d = wd.shape[2]
    n_blocks = n_rows // MOE_BLOCK
    row_map = lambda j, lo, hi, nu: (jnp.minimum(j, nu[0] - 1), 0)
    return pl.pallas_call(
        _moe_kernel,
        out_shape=jax.ShapeDtypeStruct((n_rows, d), U32),
        grid_spec=pltpu.PrefetchScalarGridSpec(
            num_scalar_prefetch=3,
            grid=(n_blocks,),
            in_specs=[pl.BlockSpec((MOE_BLOCK, hw), row_map),
                      pl.BlockSpec((None,) + wgu.shape[1:], lambda j, lo, hi, nu: (lo[j], 0, 0)),
                      pl.BlockSpec((None,) + wd.shape[1:], lambda j, lo, hi, nu: (lo[j], 0, 0)),
                      pl.BlockSpec((None,) + wgu.shape[1:], lambda j, lo, hi, nu: (hi[j], 0, 0)),
                      pl.BlockSpec((None,) + wd.shape[1:], lambda j, lo, hi, nu: (hi[j], 0, 0))],
            out_specs=pl.BlockSpec((MOE_BLOCK, d), lambda j, lo, hi, nu: (j, 0))),
        compiler_params=_cparams("arbitrary"),
        name="experts",
    )(e_lo, e_hi, n_used, hs, wgu, wd, wgu, wd)


def _combine_kernel(pos_ref, x_ref, rc_ref, gate_ref, ys_ref, xo_ref, ybuf, sems):
    tm = x_ref.shape[0]
    i = pl.program_id(0)
    n = pl.num_programs(0)

    def gather(tile, slot):
        def issue(r, carry):
            pltpu.make_async_copy(ys_ref.at[pl.ds(pos_ref[tile * tm + r], 1)],
                                  ybuf.at[slot, pl.ds(r, 1)], sems.at[slot]).start()
            return carry
        lax.fori_loop(0, tm, issue, 0, unroll=8)

    @pl.when(i == 0)
    def _():
        gather(0, 0)

    @pl.when(i + 1 < n)
    def _():
        gather(i + 1, (i + 1) % 2)

    slot = i % 2
    pltpu.make_async_copy(ys_ref.at[pl.ds(0, tm)], ybuf.at[slot], sems.at[slot]).wait()
    ya, yb = _unpack_halves(ybuf[slot])
    rc = rc_ref[...]
    y = rc[:, 2:3] * ya + rc[:, 3:4] * yb
    xo_ref[...] = x_ref[...] + gate_ref[...] * y


def _combine(pos, x, rc, gate, ys, rows_per_mod, tm):
    t, d = x.shape
    mod_map = lambda i, pos: ((i * tm) // rows_per_mod, 0, 0)
    return pl.pallas_call(
        _combine_kernel,
        out_shape=jax.ShapeDtypeStruct((t, d), F32),
        grid_spec=pltpu.PrefetchScalarGridSpec(
            num_scalar_prefetch=1,
            grid=(t // tm,),
            in_specs=[pl.BlockSpec((tm, d), lambda i, pos: (i, 0)),
                      pl.BlockSpec((tm, rc.shape[1]), lambda i, pos: (i, 0)),
                      pl.BlockSpec((None, 1, d), mod_map),
                      pl.BlockSpec(memory_space=pl.ANY)],
            out_specs=pl.BlockSpec((tm, d), lambda i, pos: (i, 0)),
            scratch_shapes=[pltpu.VMEM((2, tm, d), U32), pltpu.SemaphoreType.DMA((2,))]),
        compiler_params=_cparams("arbitrary"),
        name="combine",
    )(pos, x, rc, gate, ys)


def _deinterleave(base):
    return [base + 2 * p for p in range(HEAD_DIM // 2)] + [base + 2 * p + 1 for p in range(HEAD_DIM // 2)]


def _even_in_perm(n_cols):
    cols = []
    half = N_HEADS // N_KV_HEADS
    for m in range(half):
        cols += _deinterleave(HEAD_DIM * m) + _deinterleave(HEAD_DIM * (m + half))
    for kh in range(N_KV_HEADS):
        cols += _deinterleave(ATTN_WIDTH + HEAD_DIM * kh)
    cols += list(range(QK_WIDTH, n_cols))
    return np.asarray(cols, np.int32)


def _even_out_perm(n_rows):
    rows = []
    half = N_HEADS // N_KV_HEADS
    for m in range(half):
        rows += list(range(HEAD_DIM * m, HEAD_DIM * (m + 1)))
        rows += list(range(HEAD_DIM * (m + half), HEAD_DIM * (m + half + 1)))
    rows += list(range(ATTN_WIDTH, n_rows))
    return np.asarray(rows, np.int32)


def _rope_tables(n_tokens):
    rows = n_tokens // GRID_W
    row = jnp.repeat(jnp.arange(rows, dtype=F32), GRID_W)
    col = jnp.tile(jnp.arange(GRID_W, dtype=F32), rows)
    inv = ROPE_THETA ** (-jnp.arange(ROPE_PAIRS_PER_AXIS, dtype=F32) / ROPE_PAIRS_PER_AXIS)
    ang = jnp.concatenate([row[:, None] * inv, col[:, None] * inv], axis=-1)
    cos, sin = jnp.cos(ang), jnp.sin(ang)
    reps = LANES // HEAD_DIM
    return (jnp.tile(jnp.concatenate([cos, cos], axis=-1), (1, reps)),
            jnp.tile(jnp.concatenate([-sin, sin], axis=-1), (1, reps)))


def _head_mean_matrix():
    blk = np.kron(np.eye(LANES // HEAD_DIM), np.full((HEAD_DIM, HEAD_DIM), 1.0 / HEAD_DIM))
    return jnp.asarray(np.concatenate([blk, blk], axis=0), BF16)


def _moe_layer(tokens_hp, splits, router_wp, router_b, tri, wgu, wd, tm):
    t = tokens_hp.shape[0]
    route, counts = _router(tokens_hp, router_wp, router_b, tri, tm)
    cls = route[0].astype(jnp.int32)
    rank = route[1].astype(jnp.int32)
    cnt = counts[:N_CLASSES, 0].astype(jnp.int32)
    padded = (cnt + MOE_BLOCK - 1) // MOE_BLOCK * MOE_BLOCK
    pad_end = jnp.cumsum(padded)
    pos = (pad_end - padded)[cls] + rank
    n_blocks = -(-t // MOE_BLOCK) + N_CLASSES
    n_used = pad_end[-1] // MOE_BLOCK
    blk = jnp.minimum(jnp.arange(n_blocks, dtype=jnp.int32), n_used - 1)
    bcls = jnp.minimum(jnp.searchsorted(pad_end, blk * MOE_BLOCK, side="right"), N_CLASSES - 1).astype(jnp.int32)
    e_lo = EXPERTS_PER_GROUP * (bcls // N_PAIRS) + jnp.asarray(PAIR_LO, jnp.int32)[bcls % N_PAIRS]
    e_hi = EXPERTS_PER_GROUP * (bcls // N_PAIRS) + jnp.asarray(PAIR_HI, jnp.int32)[bcls % N_PAIRS]
    hs = _dispatch(pos, tokens_hp, n_blocks * MOE_BLOCK, tm)
    ys = _moe(e_lo, e_hi, n_used.reshape(1).astype(jnp.int32), hs, wgu, wd)
    rc = route.T
    outs, start = [], 0
    for x, gate, rows_per_mod in splits:
        n = x.shape[0]
        ctm = min(tm, 256, n)
        outs.append(_combine(pos[start:start + n], x, rc[start:start + n], gate, ys, rows_per_mod, ctm))
        start += n
    return outs


def kernel(x, c, ctx, c_ctx, w_mod, b_mod, norm_mix_g, norm_ffn_g, ev_w_in, ev_q_norm_g, ev_k_norm_g,
           ev_dw_w, ev_dw_b, ev_ln_g, ev_ln_b, ev_w_out, od_w_in, od_conv_w, od_w_out, router_w,
           router_bias, moe_w_gate, moe_w_up, moe_w_down):
    b, s, d = x.shape
    n_ctx = ctx.shape[1]
    depth = w_mod.shape[0]
    last_attn = (depth - 1) - ((depth - 1) % 2)
    tm = min(ROW_TILE, s)
    tmz = min(ROW_TILE, n_ctx)
    tq = min(Q_TILE, s)
    tqz = min(Q_TILE, n_ctx)
    assert s % tm == 0 and n_ctx % tmz == 0 and s % tq == 0 and n_ctx % tqz == 0 and s % GRID_W == 0

    r = -(-(b + 1) // 8) * 8
    c_all = jnp.zeros((r, d), F32).at[:b].set(c).at[b].set(c_ctx)
    mods = _modulation(c_all, w_mod, b_mod)

    def mod_x(l, j):
        return mods[l, :b, j * d:(j + 1) * d].reshape(b, 1, d)

    def mod_z(l, j):
        return mods[l, b:b + 1, j * d:(j + 1) * d].reshape(1, 1, d)

    cos_t, sin_t = _rope_tables(s)
    gmat = _head_mean_matrix()
    in_perm = _even_in_perm(ev_w_in.shape[2])
    out_perm = _even_out_perm(ev_w_out.shape[1])
    head_perm = np.asarray(_deinterleave(0), np.int32)
    router_wp = jnp.zeros((d, LANES), F32).at[:, :N_EXPERTS].set(router_w).astype(BF16)
    router_b = router_bias.astype(F32).reshape(N_EXPERTS, 1)
    tri_n = min(ROW_TILE, b * s)
    tri = jnp.asarray(np.triu(np.ones((tri_n, tri_n), np.float32), 1), BF16)

    xf = x.reshape(b * s, d)
    zf = ctx.reshape(b * n_ctx, d)
    for l in range(depth):
        i = l // 2
        ctx_in = l <= last_attn
        ctx_out = l < last_attn
        gmix = norm_mix_g[l].reshape(1, d)
        gffn = norm_ffn_g[l].reshape(1, d)
        if l % 2 == 0:
            w_in = ev_w_in[i][:, in_perm].astype(BF16)
            w_out = ev_w_out[i][out_perm, :].astype(BF16)
            gq = ev_q_norm_g[i][head_perm] * (ATTN_SCALE * LOG2E)
            gk = ev_k_norm_g[i][head_perm]
            gqk = jnp.concatenate([jnp.tile(gq, N_HEADS), jnp.tile(gk, N_KV_HEADS)]).reshape(1, QK_WIDTH)
            conv_params = [ev_dw_w[i], ev_dw_b[i].reshape(1, -1), ev_ln_g[i].reshape(1, -1),
                           ev_ln_b[i].reshape(1, -1)]
            q, k, v, a = _inproj_even(xf, gmix, mod_x(l, 0), mod_x(l, 1), s, w_in, gqk, cos_t, sin_t, gmat,
                                      tm, True)
            k_all = k.reshape(b, s, KV_WIDTH)
            v_all = v.reshape(N_KV_HEADS, b, s, LANES)
            if ctx_in:
                qz, kz, vz, az = _inproj_even(zf, gmix, mod_z(l, 0), mod_z(l, 1), b * n_ctx, w_in, gqk,
                                              cos_t, sin_t, gmat, tmz, False)
                kz = kz.reshape(b, n_ctx, KV_WIDTH)
                vz = vz.reshape(N_KV_HEADS, b, n_ctx, LANES)
                k_all = jnp.concatenate([k_all, kz], axis=1)
                v_all = jnp.concatenate([v_all, vz], axis=2)
            attn = _attention(q, k_all, v_all, s, tq)
            xf, hp = _outproj(_outproj_even_kernel, "outproj_even", xf, [attn], a, conv_params, w_out,
                              mod_x(l, 2), gffn, mod_x(l, 3), mod_x(l, 4), s, s, tm)
            if ctx_out:
                attn_z = _attention(qz, kz, vz, n_ctx, tqz)
                zf, hpz = _outproj(_outproj_even_kernel, "outproj_even_ctx", zf, [attn_z], az, conv_params,
                                   w_out, mod_z(l, 2), gffn, mod_z(l, 3), mod_z(l, 4), b * n_ctx, n_ctx, tmz)
        else:
            w_in = od_w_in[i].astype(BF16)
            w_out = od_w_out[i].astype(BF16)
            conv_params = [od_conv_w[i]]
            gb, u = _inproj_odd(xf, gmix, mod_x(l, 0), mod_x(l, 1), s, w_in, tm)
            xf, hp = _outproj(_outproj_odd_kernel, "outproj_odd", xf, [gb], u, conv_params, w_out,
                              mod_x(l, 2), gffn, mod_x(l, 3), mod_x(l, 4), s, s, tm)
            if ctx_out:
                gbz, uz = _inproj_odd(zf, gmix, mod_z(l, 0), mod_z(l, 1), b * n_ctx, w_in, tmz)
                zf, hpz = _outproj(_outproj_odd_kernel, "outproj_odd_ctx", zf, [gbz], uz, conv_params, w_out,
                                   mod_z(l, 2), gffn, mod_z(l, 3), mod_z(l, 4), b * n_ctx, n_ctx, tmz)

        wgu = jnp.concatenate([moe_w_gate[l], moe_w_up[l]], axis=-1).astype(BF16)
        wd = moe_w_down[l].astype(BF16)
        splits = [(xf, mod_x(l, 5), s)]
        tokens = hp
        if ctx_out:
            splits.append((zf, mod_z(l, 5), b * n_ctx))
            tokens = jnp.concatenate([hp, hpz], axis=0)
        rtm = math.gcd(tri_n, math.gcd(b * s, b * n_ctx)) if ctx_out else tri_n
        outs = _moe_layer(tokens, splits, router_wp, router_b, tri[:rtm, :rtm], wgu, wd, rtm)
        xf = outs[0]
        if ctx_out:
            zf = outs[1]
    return xf.reshape(b, s, d)
```

```python
import functools
import math

import numpy as np
import jax
import jax.numpy as jnp
from jax import lax
from jax.experimental import pallas as pl
from jax.experimental.pallas import tpu as pltpu

F32 = jnp.float32
BF16 = jnp.bfloat16
U32 = jnp.uint32

GRID_W = 64
EPS = 1e-6
N_HEADS = 8
N_KV_HEADS = 2
GQA_GROUP = N_HEADS // N_KV_HEADS
HEAD_DIM = 64
ATTN_WIDTH = N_HEADS * HEAD_DIM
KV_WIDTH = N_KV_HEADS * HEAD_DIM
QK_WIDTH = ATTN_WIDTH + KV_WIDTH
ATTN_SCALE = HEAD_DIM ** -0.5
ROPE_THETA = 10000.0
ROPE_PAIRS_PER_AXIS = HEAD_DIM // 4
CONV_A_K = 31
SC_K = 3
N_EXPERTS = 16
N_GROUPS = 4
EXPERTS_PER_GROUP = N_EXPERTS // N_GROUPS
D_EXPERT = 512
PAIR_LO = (0, 0, 0, 1, 1, 2)
PAIR_HI = (1, 2, 3, 2, 3, 3)
N_PAIRS = len(PAIR_LO)
N_CLASSES = N_GROUPS * N_PAIRS
CLASS_ROWS = 32

LANES = 128
MXU_COLS = 256
HALO = 16
VMEM_LIMIT = 48 * 1024 * 1024
ROW_TILE = 512
Q_TILE = 256
MOE_BLOCK = 256
CONV_ROWS = 64
LOG2E = math.log2(math.e)


def _cparams(*sem):
    return pltpu.CompilerParams(dimension_semantics=sem, vmem_limit_bytes=VMEM_LIMIT)


def _sigmoid(x):
    return 1.0 / (1.0 + jnp.exp(-x))


def _rms_mod(x, g, shift, scale):
    y = x * lax.rsqrt(jnp.mean(x * x, axis=-1, keepdims=True) + EPS) * g
    return y * (1.0 + scale) + shift


def _pack_halves(h):
    w = h.shape[1] // 2
    lo = lax.bitcast_convert_type(h[:, :w].astype(BF16).astype(F32), U32)
    hi = lax.bitcast_convert_type(h[:, w:].astype(BF16).astype(F32), U32)
    return (lo >> 16) | (hi & jnp.uint32(0xFFFF0000))


def _unpack_halves(w):
    lo = lax.bitcast_convert_type(w << 16, F32)
    hi = lax.bitcast_convert_type(w & jnp.uint32(0xFFFF0000), F32)
    return lo, hi


def _issue_row_gathers(pos_ref, ys_ref, ybuf, sems, tile, slot, rows):
    tm = ybuf.shape[1]
    for r in rows:
        pltpu.make_async_copy(ys_ref.at[pl.ds(pos_ref[tile * tm + r], 1)], ybuf.at[slot, pl.ds(r, 1)],
                              sems.at[slot]).start()


def _wait_row_gathers(ys_ref, ybuf, sems, slot):
    pltpu.make_async_copy(ys_ref.at[pl.ds(0, ybuf.shape[1])], ybuf.at[slot], sems.at[slot]).wait()


def _combined_residual(pos_ref, x_ref, rc_ref, gate_ref, ys_ref, ybuf, sems):
    tm = ybuf.shape[1]
    i = pl.program_id(0)
    n = pl.num_programs(0)

    @pl.when(i == 0)
    def _():
        _issue_row_gathers(pos_ref, ys_ref, ybuf, sems, 0, 0, range(tm))

    _wait_row_gathers(ys_ref, ybuf, sems, i % 2)
    ya, yb = _unpack_halves(ybuf[i % 2])
    rc = rc_ref[...]
    x_new = x_ref[...] + gate_ref[...] * (rc[:, 2:3] * ya + rc[:, 3:4] * yb)

    def issue(k, n_slices):
        _issue_row_gathers(pos_ref, ys_ref, ybuf, sems, jnp.minimum(i + 1, n - 1), (i + 1) % 2,
                           range(k * tm // n_slices, (k + 1) * tm // n_slices))
        if k == n_slices - 1:
            @pl.when(i == n - 1)
            def _():
                _wait_row_gathers(ys_ref, ybuf, sems, (i + 1) % 2)

    return x_new, issue


def _combine_in_specs(tm, d, rc_width, rows_per_mod):
    return [pl.BlockSpec((tm, d), lambda i, pos: (i, 0)),
            pl.BlockSpec((tm, rc_width), lambda i, pos: (i, 0)),
            pl.BlockSpec((None, 1, d), lambda i, pos: ((i * tm) // rows_per_mod, 0, 0)),
            pl.BlockSpec(memory_space=pl.ANY)]


def _combine_scratch(tm, d):
    return [pltpu.VMEM((2, tm, d), U32), pltpu.SemaphoreType.DMA((2,))]


def _mod_kernel(c_ref, w_ref, b_ref, o_ref):
    c = c_ref[...]
    a = (c * _sigmoid(c)).astype(BF16)
    o_ref[...] = jnp.dot(a, w_ref[...].astype(BF16), preferred_element_type=F32) + b_ref[...]


def _modulation(c_all, w_mod, b_mod):
    depth, d, n = w_mod.shape
    r = c_all.shape[0]
    nt = n // d
    return pl.pallas_call(
        _mod_kernel,
        out_shape=jax.ShapeDtypeStruct((depth, r, n), F32),
        grid=(depth, nt),
        in_specs=[pl.BlockSpec((r, d), lambda l, j: (0, 0)),
                  pl.BlockSpec((None, d, d), lambda l, j: (l, 0, j)),
                  pl.BlockSpec((None, 1, d), lambda l, j: (l, 0, j))],
        out_specs=pl.BlockSpec((None, r, d), lambda l, j: (l, 0, j)),
        compiler_params=_cparams("parallel", "parallel"),
        name="modulation",
    )(c_all, w_mod, b_mod.reshape(depth, 1, n))


def _inproj_kernel(*refs, body, n_in, combine):
    if combine:
        pos_ref, x_ref, rc_ref, gate_ref, ys_ref = refs[:5]
        ybuf, sems = refs[-2:]
        rest = refs[5:-2]
        x, issue = _combined_residual(pos_ref, x_ref, rc_ref, gate_ref, ys_ref, ybuf, sems)
        rest[3 + n_in][...] = x
        ins, outs = rest[:3 + n_in], rest[4 + n_in:]
    else:
        x = refs[0][...]
        issue = lambda k, n_slices: None
        ins, outs = refs[1:4 + n_in], refs[4 + n_in:]
    h = _rms_mod(x, ins[0][...], ins[1][...], ins[2][...]).astype(BF16)
    body(h, issue, *ins[3:], *outs)


def _inproj(body, name, x, pending, g, shift, scale, rows_per_mod, extra, extra_specs, out_shape, out_specs,
            tm):
    t, d = x.shape
    mod_map = lambda i, *_: ((i * tm) // rows_per_mod, 0, 0)
    in_specs = [pl.BlockSpec((1, d), lambda i, *_: (0, 0)), pl.BlockSpec((None, 1, d), mod_map),
                pl.BlockSpec((None, 1, d), mod_map)] + list(extra_specs)
    kern = functools.partial(_inproj_kernel, body=body, n_in=len(extra), combine=pending is not None)
    if pending is None:
        return pl.pallas_call(
            kern, out_shape=tuple(out_shape), grid=(t // tm,),
            in_specs=[pl.BlockSpec((tm, d), lambda i: (i, 0))] + in_specs, out_specs=tuple(out_specs),
            compiler_params=_cparams("parallel"), name=name,
        )(x, g, shift, scale, *extra)
    pos, rc, gate, ys = pending
    return pl.pallas_call(
        kern,
        out_shape=(jax.ShapeDtypeStruct((t, d), F32),) + tuple(out_shape),
        grid_spec=pltpu.PrefetchScalarGridSpec(
            num_scalar_prefetch=1, grid=(t // tm,),
            in_specs=_combine_in_specs(tm, d, rc.shape[1], rows_per_mod) + in_specs,
            out_specs=(pl.BlockSpec((tm, d), lambda i, pos: (i, 0)),) + tuple(out_specs),
            scratch_shapes=_combine_scratch(tm, d)),
        compiler_params=_cparams("arbitrary"), name=name + "_combine",
    )(pos, x, rc, gate, ys, g, shift, scale, *extra)


def _inproj_even_body(h, issue, w_ref, gqk_ref, cos_ref, sin_ref, gm_ref, q_ref, k_ref, v_ref, a_ref, *, rope):
    tm = h.shape[0]
    lane = lax.broadcasted_iota(jnp.int32, (tm, LANES), 1)
    first_head = lane < HEAD_DIM
    first_half = (lane % HEAD_DIM) < (HEAD_DIM // 2)
    c0 = QK_WIDTH + KV_WIDTH
    cw = (w_ref.shape[1] - c0) // 2
    n_glu = cw // MXU_COLS
    issue(0, 1)
    p_all = jnp.dot(h, w_ref[...], preferred_element_type=F32)
    proj = lambda lo, width: p_all[:, lo:lo + width]
    for m in range(QK_WIDTH // LANES):
        if m % (MXU_COLS // LANES) == 0:
            p = proj(LANES * m, MXU_COLS)
        y = p[:, LANES * (m % (MXU_COLS // LANES)):LANES * (m % (MXU_COLS // LANES) + 1)]
        sq = y * y
        sq_hi = sq.astype(BF16)
        sq_lo = (sq - sq_hi.astype(F32)).astype(BF16)
        ms = jnp.dot(jnp.concatenate([sq_hi, sq_lo], axis=1), gm_ref[...], preferred_element_type=F32)
        y = y * lax.rsqrt(ms + EPS) * gqk_ref[:, LANES * m:LANES * (m + 1)]
        if rope:
            partner = jnp.where(first_half, pltpu.roll(y, LANES - HEAD_DIM // 2, 1),
                                pltpu.roll(y, HEAD_DIM // 2, 1))
            y = y * cos_ref[...] + partner * sin_ref[...]
        if m < ATTN_WIDTH // LANES:
            q_ref[m] = jnp.where(first_head, y, 0.0).astype(BF16)
            q_ref[m + ATTN_WIDTH // LANES] = jnp.where(first_head, 0.0, y).astype(BF16)
        else:
            k_ref[...] = y.astype(BF16)
    v = p[:, MXU_COLS - KV_WIDTH:]
    v_ref[0] = jnp.where(first_head, v, 1.0).astype(BF16)
    v_ref[1] = jnp.where(first_head, 1.0, v).astype(BF16)
    for c in range(n_glu):
        a = proj(c0 + MXU_COLS * c, MXU_COLS)
        gate = proj(c0 + cw + MXU_COLS * c, MXU_COLS)
        a_ref[:, MXU_COLS * c:MXU_COLS * (c + 1)] = (a * _sigmoid(gate)).astype(BF16)


def _inproj_even(x, pending, g, shift, scale, rows_per_mod, w, gqk, cos_t, sin_t, gmat, tm, rope):
    t, d = x.shape
    n = w.shape[1]
    cw = (n - QK_WIDTH - KV_WIDTH) // 2
    n_pos = cos_t.shape[0] // tm
    return _inproj(
        functools.partial(_inproj_even_body, rope=rope), "inproj_even", x, pending, g, shift, scale,
        rows_per_mod, [w, gqk, cos_t, sin_t, gmat],
        [pl.BlockSpec((d, n), lambda i, *_: (0, 0)),
         pl.BlockSpec((1, QK_WIDTH), lambda i, *_: (0, 0)),
         pl.BlockSpec((tm, LANES), lambda i, *_: (i % n_pos, 0)),
         pl.BlockSpec((tm, LANES), lambda i, *_: (i % n_pos, 0)),
         pl.BlockSpec((2 * LANES, LANES), lambda i, *_: (0, 0))],
        [jax.ShapeDtypeStruct((N_HEADS, t, LANES), BF16), jax.ShapeDtypeStruct((t, KV_WIDTH), BF16),
         jax.ShapeDtypeStruct((N_KV_HEADS, t, LANES), BF16), jax.ShapeDtypeStruct((t, cw), BF16)],
        [pl.BlockSpec((N_HEADS, tm, LANES), lambda i, *_: (0, i, 0)),
         pl.BlockSpec((tm, KV_WIDTH), lambda i, *_: (i, 0)),
         pl.BlockSpec((N_KV_HEADS, tm, LANES), lambda i, *_: (0, i, 0)),
         pl.BlockSpec((tm, cw), lambda i, *_: (i, 0))],
        tm)


def _inproj_odd_body(h, issue, w_ref, gb_ref, u_ref):
    w = w_ref.shape[1] // 3
    n_phase = w // MXU_COLS
    proj = lambda lo: jnp.dot(h, w_ref[:, lo:lo + MXU_COLS], preferred_element_type=F32)
    for c in range(n_phase):
        issue(c, n_phase)
        cols = slice(MXU_COLS * c, MXU_COLS * (c + 1))
        gb_ref[:, cols] = proj(MXU_COLS * c).astype(BF16)
        u_ref[:, cols] = (proj(w + MXU_COLS * c) * proj(2 * w + MXU_COLS * c)).astype(BF16)


def _inproj_odd(x, pending, g, shift, scale, rows_per_mod, w, tm):
    t, d = x.shape
    n = w.shape[1]
    return _inproj(
        _inproj_odd_body, "inproj_odd", x, pending, g, shift, scale, rows_per_mod, [w],
        [pl.BlockSpec((d, n), lambda i, *_: (0, 0))],
        [jax.ShapeDtypeStruct((t, n // 3), BF16), jax.ShapeDtypeStruct((t, n // 3), BF16)],
        [pl.BlockSpec((tm, n // 3), lambda i, *_: (i, 0)), pl.BlockSpec((tm, n // 3), lambda i, *_: (i, 0))],
        tm)


def _attn_kernel(q_ref, k_ref, v_ref, o_ref, acc_ref):
    tq = q_ref.shape[1]
    n_pair = N_HEADS // N_KV_HEADS
    lane = lax.broadcasted_iota(jnp.int32, (tq, LANES), 1)

    def pair(m, carry):
        outs = []
        for g in range(N_KV_HEADS):
            q = q_ref[m + n_pair * g]
            s = lax.dot_general(q, k_ref[...], (((1,), (1,)), ((), ())), preferred_element_type=F32)
            p = jnp.exp2(s - jnp.max(s, axis=-1, keepdims=True)).astype(BF16)
            o = jnp.dot(p, v_ref[g], preferred_element_type=F32)
            l = o[:, HEAD_DIM:HEAD_DIM + 1] if g == 0 else o[:, 0:1]
            outs.append(o / l)
        acc_ref[m] = jnp.where(lane < HEAD_DIM, outs[0], outs[1])
        return carry

    lax.fori_loop(0, n_pair, pair, 0)
    o_ref[...] = jnp.concatenate([acc_ref[m] for m in range(n_pair)], axis=1).astype(BF16)


def _attention(q, k_all, v_all, s_q, tq):
    t = q.shape[1]
    b, skv, _ = k_all.shape
    nq = s_q // tq
    return pl.pallas_call(
        _attn_kernel,
        out_shape=jax.ShapeDtypeStruct((t, ATTN_WIDTH), BF16),
        grid=(b, nq),
        in_specs=[pl.BlockSpec((N_HEADS, tq, LANES), lambda bi, i: (0, bi * nq + i, 0)),
                  pl.BlockSpec((None, skv, LANES), lambda bi, i: (bi, 0, 0)),
                  pl.BlockSpec((N_KV_HEADS, None, skv, LANES), lambda bi, i: (0, bi, 0, 0))],
        out_specs=pl.BlockSpec((tq, ATTN_WIDTH), lambda bi, i: (bi * nq + i, 0)),
        scratch_shapes=[pltpu.VMEM((N_HEADS // N_KV_HEADS, tq, LANES), F32)],
        compiler_params=_cparams("parallel", "parallel"),
        name="attention",
    )(q, k_all, v_all)


def _fill_window(win_ref, cur_ref, prev_ref, next_ref, is_first, is_last):
    tm = cur_ref.shape[0]
    win_ref[0:HALO, :] = jnp.where(is_first, 0.0, prev_ref[...].astype(F32))
    win_ref[HALO:HALO + tm, :] = cur_ref[...].astype(F32)
    win_ref[HALO + tm:2 * HALO + tm, :] = jnp.where(is_last, 0.0, next_ref[...].astype(F32))


def _dwconv_chunk(win_ref, w_ref, r0, rows, taps):
    off = HALO - taps // 2
    acc = win_ref[r0 + off:r0 + off + rows, :] * w_ref[0:1, :]
    for kk in range(1, taps):
        acc = acc + win_ref[r0 + off + kk:r0 + off + kk + rows, :] * w_ref[kk:kk + 1, :]
    return acc


def _residual_tail(x_ref, o, gate_ref, g2_ref, sh2_ref, sc2_ref, xo_ref, hp_ref):
    x_new = x_ref[...] + gate_ref[...] * o
    xo_ref[...] = x_new
    hp_ref[...] = _pack_halves(_rms_mod(x_new, g2_ref[...], sh2_ref[...], sc2_ref[...]))


def _outproj_even_kernel(x_ref, at_ref, a_ref, ap_ref, an_ref, dw_ref, db_ref, lg_ref, lb_ref, w_ref,
                         gate_ref, g2_ref, sh2_ref, sc2_ref, xo_ref, hp_ref, win_ref, cv_ref,
                         *, tiles_per_seq):
    tm = x_ref.shape[0]
    i = pl.program_id(0) % tiles_per_seq
    _fill_window(win_ref, a_ref, ap_ref, an_ref, i == 0, i == tiles_per_seq - 1)
    for r0 in range(0, tm, CONV_ROWS):
        y = _dwconv_chunk(win_ref, dw_ref, r0, CONV_ROWS, CONV_A_K) + db_ref[...]
        mu = jnp.mean(y, axis=-1, keepdims=True)
        yc = y - mu
        var = jnp.mean(yc * yc, axis=-1, keepdims=True)
        yn = yc * lax.rsqrt(var + EPS) * lg_ref[...] + lb_ref[...]
        cv_ref[r0:r0 + CONV_ROWS, :] = (yn * _sigmoid(yn)).astype(BF16)
    aw = at_ref.shape[1]
    o = (jnp.dot(at_ref[...], w_ref[0:aw, :], preferred_element_type=F32)
         + jnp.dot(cv_ref[...], w_ref[aw:, :], preferred_element_type=F32))
    _residual_tail(x_ref, o, gate_ref, g2_ref, sh2_ref, sc2_ref, xo_ref, hp_ref)


def _outproj_odd_kernel(x_ref, gb_ref, u_ref, up_ref, un_ref, cw_ref, w_ref,
                        gate_ref, g2_ref, sh2_ref, sc2_ref, xo_ref, hp_ref, win_ref, cv_ref,
                        *, tiles_per_seq):
    tm = x_ref.shape[0]
    i = pl.program_id(0) % tiles_per_seq
    _fill_window(win_ref, u_ref, up_ref, un_ref, i == 0, i == tiles_per_seq - 1)
    for r0 in range(0, tm, CONV_ROWS):
        y = _dwconv_chunk(win_ref, cw_ref, r0, CONV_ROWS, SC_K)
        cv_ref[r0:r0 + CONV_ROWS, :] = (gb_ref[r0:r0 + CONV_ROWS, :].astype(F32) * y).astype(BF16)
    o = jnp.dot(cv_ref[...], w_ref[...], preferred_element_type=F32)
    _residual_tail(x_ref, o, gate_ref, g2_ref, sh2_ref, sc2_ref, xo_ref, hp_ref)


def _halo_specs(tm, width, n_rows):
    hb = tm // HALO
    last = n_rows // HALO - 1
    return [pl.BlockSpec((tm, width), lambda i: (i, 0)),
            pl.BlockSpec((HALO, width), lambda i: (jnp.maximum(i * hb - 1, 0), 0)),
            pl.BlockSpec((HALO, width), lambda i: (jnp.minimum((i + 1) * hb, last), 0))]


def _outproj(kernel_fn, name, x, lead, conv_in, params, w, gate, g2, shift2, scale2, rows_per_mod,
             seq_len, tm):
    t, d = x.shape
    cw = conv_in.shape[1]
    mod_map = lambda i: ((i * tm) // rows_per_mod, 0, 0)
    row_spec = lambda a: pl.BlockSpec((tm, a.shape[1]), lambda i: (i, 0))
    full_spec = lambda a: pl.BlockSpec(a.shape, lambda i: (0,) * a.ndim)
    in_specs = ([row_spec(x)] + [row_spec(a) for a in lead] + _halo_specs(tm, cw, t)
                + [full_spec(a) for a in params] + [full_spec(w)]
                + [pl.BlockSpec((None, 1, d), mod_map), full_spec(g2),
                   pl.BlockSpec((None, 1, d), mod_map), pl.BlockSpec((None, 1, d), mod_map)])
    return pl.pallas_call(
        functools.partial(kernel_fn, tiles_per_seq=seq_len // tm),
        out_shape=(jax.ShapeDtypeStruct((t, d), F32), jax.ShapeDtypeStruct((t, d // 2), U32)),
        grid=(t // tm,),
        in_specs=in_specs,
        out_specs=(pl.BlockSpec((tm, d), lambda i: (i, 0)), pl.BlockSpec((tm, d // 2), lambda i: (i, 0))),
        scratch_shapes=[pltpu.VMEM((tm + 2 * HALO, cw), F32), pltpu.VMEM((tm, cw), BF16)],
        compiler_params=_cparams("parallel"),
        name=name,
    )(x, *lead, conv_in, conv_in, conv_in, *params, w, gate, g2, shift2, scale2)


def _router_kernel(hp_ref, rw_ref, bias_ref, tri_ref, route_ref, cnt_ref, base_ref):
    tm = hp_ref.shape[0]

    @pl.when(pl.program_id(0) == 0)
    def _():
        base_ref[...] = jnp.zeros_like(base_ref)

    lo, hi = _unpack_halves(hp_ref[...])
    h = jnp.concatenate([lo, hi], axis=1).astype(BF16)
    logits = jnp.dot(h, rw_ref[...], preferred_element_type=F32)
    lt = logits.T[0:N_EXPERTS, :]
    aff = _sigmoid(lt)
    sel = aff + bias_ref[...]
    rows = lambda a, e: a[e:e + 1, :]

    best = None
    for g in range(N_GROUPS):
        a, b, c, d = (rows(sel, EXPERTS_PER_GROUP * g + j) for j in range(EXPERTS_PER_GROUP))
        score = jnp.maximum(jnp.maximum(a + b, c + d), jnp.maximum(a, b) + jnp.maximum(c, d))
        if best is None:
            best, grp = score, jnp.zeros((1, tm), jnp.int32)
        else:
            better = score > best
            grp = jnp.where(better, g, grp)
            best = jnp.where(better, score, best)

    def in_group(a, j):
        out = rows(a, j)
        for g in range(1, N_GROUPS):
            out = jnp.where(grp == g, rows(a, EXPERTS_PER_GROUP * g + j), out)
        return out

    s = [in_group(sel, j) for j in range(EXPERTS_PER_GROUP)]
    f = [in_group(aff, j) for j in range(EXPERTS_PER_GROUP)]

    def first_argmax(vals):
        m = functools.reduce(jnp.maximum, vals)
        idx = jnp.full((1, tm), len(vals) - 1, jnp.int32)
        for j in range(len(vals) - 2, -1, -1):
            idx = jnp.where(vals[j] == m, j, idx)
        return idx

    i1 = first_argmax(s)
    i2 = first_argmax([jnp.where(i1 == j, -jnp.inf, s[j]) for j in range(EXPERTS_PER_GROUP)])
    e_lo = jnp.minimum(i1, i2)
    e_hi = jnp.maximum(i1, i2)
    pick = lambda vals, idx: functools.reduce(
        lambda acc, j: jnp.where(idx == j, vals[j], acc), range(1, len(vals)), vals[0])
    f_lo, f_hi = pick(f, e_lo), pick(f, e_hi)
    denom = f_lo + f_hi
    pair = jnp.where(e_lo == 0, e_hi - 1, jnp.where(e_lo == 1, e_hi + 1, N_PAIRS - 1))
    cls = grp * N_PAIRS + pair

    onehot = (lax.broadcasted_iota(jnp.int32, (CLASS_ROWS, tm), 0) == cls)
    onehot_f = onehot.astype(F32)
    before = jnp.dot(onehot_f.astype(BF16), tri_ref[...], preferred_element_type=F32) + base_ref[:, 0:1]
    rank = jnp.sum(onehot_f * before, axis=0, keepdims=True)
    base_ref[...] = base_ref[...] + jnp.sum(onehot_f, axis=1, keepdims=True)

    route_ref[...] = jnp.concatenate(
        [cls.astype(F32), rank, f_lo / denom, f_hi / denom, jnp.zeros((4, tm), F32)], axis=0)
    cnt_ref[...] = base_ref[...]


def _router(hp, rw, bias, tri, tm):
    t, hw = hp.shape
    return pl.pallas_call(
        _router_kernel,
        out_shape=(jax.ShapeDtypeStruct((8, t), F32), jax.ShapeDtypeStruct((CLASS_ROWS, LANES), F32)),
        grid=(t // tm,),
        in_specs=[pl.BlockSpec((tm, hw), lambda i: (i, 0)),
                  pl.BlockSpec(rw.shape, lambda i: (0, 0)),
                  pl.BlockSpec(bias.shape, lambda i: (0, 0)),
                  pl.BlockSpec((tm, tm), lambda i: (0, 0))],
        out_specs=(pl.BlockSpec((8, tm), lambda i: (0, i)),
                   pl.BlockSpec((CLASS_ROWS, LANES), lambda i: (0, 0))),
        scratch_shapes=[pltpu.VMEM((CLASS_ROWS, LANES), F32)],
        compiler_params=_cparams("arbitrary"),
        name="router",
    )(hp, rw, bias, tri)


def _dispatch_kernel(pos_ref, hp_ref, zero_ref, hs_ref, sem):
    del zero_ref
    tm = hp_ref.shape[0]
    base = pl.program_id(0) * tm

    for r in range(tm):
        pltpu.make_async_copy(hp_ref.at[pl.ds(r, 1)], hs_ref.at[pl.ds(pos_ref[base + r], 1)], sem).start()
    pltpu.make_async_copy(hp_ref, hs_ref.at[pl.ds(0, tm)], sem).wait()


def _dispatch(pos, hp, n_rows, tm):
    t, hw = hp.shape
    zeros = jnp.zeros((n_rows, hw), U32)
    return pl.pallas_call(
        _dispatch_kernel,
        out_shape=jax.ShapeDtypeStruct((n_rows, hw), U32),
        grid_spec=pltpu.PrefetchScalarGridSpec(
            num_scalar_prefetch=1,
            grid=(t // tm,),
            in_specs=[pl.BlockSpec((tm, hw), lambda i, pos: (i, 0)),
                      pl.BlockSpec(memory_space=pl.ANY)],
            out_specs=pl.BlockSpec(memory_space=pl.ANY),
            scratch_shapes=[pltpu.SemaphoreType.DMA(())]),
        input_output_aliases={2: 0},
        compiler_params=_cparams("arbitrary"),
        name="dispatch",
    )(pos, hp, zeros)


def _moe_kernel(elo_ref, ehi_ref, nused_ref, hs_ref, wgu_lo, wd_lo, wgu_hi, wd_hi, ys_ref):
    del elo_ref, ehi_ref

    @pl.when(pl.program_id(0) >= nused_ref[0])
    def _():
        ys_ref[...] = jnp.zeros_like(ys_ref)

    @pl.when(pl.program_id(0) < nused_ref[0])
    def _():
        lo, hi = _unpack_halves(hs_ref[...])
        h = jnp.concatenate([lo, hi], axis=1).astype(BF16)

        def expert(wgu_ref, wd_ref):
            gu = jnp.dot(h, wgu_ref[...], preferred_element_type=F32)
            g, u = gu[:, :D_EXPERT], gu[:, D_EXPERT:]
            act = (g * _sigmoid(g) * u).astype(BF16)
            return jnp.dot(act, wd_ref[...], preferred_element_type=F32)

        ya = lax.bitcast_convert_type(expert(wgu_lo, wd_lo).astype(BF16).astype(F32), U32)
        yb = lax.bitcast_convert_type(expert(wgu_hi, wd_hi).astype(BF16).astype(F32), U32)
        ys_ref[...] = (ya >> 16) | (yb & jnp.uint32(0xFFFF0000))


def _moe(e_lo, e_hi, n_used, hs, wgu, wd):
    n_rows, hw = hs.shape
    d = wd.shape[2]
    n_blocks = n_rows // MOE_BLOCK
    row_map = lambda j, lo, hi, nu: (jnp.minimum(j, nu[0] - 1), 0)
    return pl.pallas_call(
        _moe_kernel,
        out_shape=jax.ShapeDtypeStruct((n_rows, d), U32),
        grid_spec=pltpu.PrefetchScalarGridSpec(
            num_scalar_prefetch=3,
            grid=(n_blocks,),
            in_specs=[pl.BlockSpec((MOE_BLOCK, hw), row_map),
                      pl.BlockSpec((None,) + wgu.shape[1:], lambda j, lo, hi, nu: (lo[j], 0, 0)),
                      pl.BlockSpec((None,) + wd.shape[1:], lambda j, lo, hi, nu: (lo[j], 0, 0)),
                      pl.BlockSpec((None,) + wgu.shape[1:], lambda j, lo, hi, nu: (hi[j], 0, 0)),
                      pl.BlockSpec((None,) + wd.shape[1:], lambda j, lo, hi, nu: (hi[j], 0, 0))],
            out_specs=pl.BlockSpec((MOE_BLOCK, d), lambda j, lo, hi, nu: (j, 0))),
        compiler_params=_cparams("arbitrary"),
        name="experts",
    )(e_lo, e_hi, n_used, hs, wgu, wd, wgu, wd)


def _combine_kernel(pos_ref, x_ref, rc_ref, gate_ref, ys_ref, xo_ref, ybuf, sems):
    x_new, issue = _combined_residual(pos_ref, x_ref, rc_ref, gate_ref, ys_ref, ybuf, sems)
    xo_ref[...] = x_new
    issue(0, 1)


def _combine(pos, x, rc, gate, ys, rows_per_mod, tm):
    t, d = x.shape
    return pl.pallas_call(
        _combine_kernel,
        out_shape=jax.ShapeDtypeStruct((t, d), F32),
        grid_spec=pltpu.PrefetchScalarGridSpec(
            num_scalar_prefetch=1,
            grid=(t // tm,),
            in_specs=_combine_in_specs(tm, d, rc.shape[1], rows_per_mod),
            out_specs=pl.BlockSpec((tm, d), lambda i, pos: (i, 0)),
            scratch_shapes=_combine_scratch(tm, d)),
        compiler_params=_cparams("arbitrary"),
        name="combine",
    )(pos, x, rc, gate, ys)


def _deinterleave(base):
    return [base + 2 * p for p in range(HEAD_DIM // 2)] + [base + 2 * p + 1 for p in range(HEAD_DIM // 2)]


def _even_in_perm(n_cols):
    cols = []
    half = N_HEADS // N_KV_HEADS
    for m in range(half):
        cols += _deinterleave(HEAD_DIM * m) + _deinterleave(HEAD_DIM * (m + half))
    for kh in range(N_KV_HEADS):
        cols += _deinterleave(ATTN_WIDTH + HEAD_DIM * kh)
    cols += list(range(QK_WIDTH, n_cols))
    return np.asarray(cols, np.int32)


def _even_out_perm(n_rows):
    rows = []
    half = N_HEADS // N_KV_HEADS
    for m in range(half):
        rows += list(range(HEAD_DIM * m, HEAD_DIM * (m + 1)))
        rows += list(range(HEAD_DIM * (m + half), HEAD_DIM * (m + half + 1)))
    rows += list(range(ATTN_WIDTH, n_rows))
    return np.asarray(rows, np.int32)


def _rope_tables(n_tokens):
    rows = n_tokens // GRID_W
    row = jnp.repeat(jnp.arange(rows, dtype=F32), GRID_W)
    col = jnp.tile(jnp.arange(GRID_W, dtype=F32), rows)
    inv = ROPE_THETA ** (-jnp.arange(ROPE_PAIRS_PER_AXIS, dtype=F32) / ROPE_PAIRS_PER_AXIS)
    ang = jnp.concatenate([row[:, None] * inv, col[:, None] * inv], axis=-1)
    cos, sin = jnp.cos(ang), jnp.sin(ang)
    reps = LANES // HEAD_DIM
    return (jnp.tile(jnp.concatenate([cos, cos], axis=-1), (1, reps)),
            jnp.tile(jnp.concatenate([-sin, sin], axis=-1), (1, reps)))


def _head_mean_matrix():
    blk = np.kron(np.eye(LANES // HEAD_DIM), np.full((HEAD_DIM, HEAD_DIM), 1.0 / HEAD_DIM))
    return jnp.asarray(np.concatenate([blk, blk], axis=0), BF16)


def _moe_layer(tokens_hp, router_wp, router_b, tri, wgu, wd, tm):
    t = tokens_hp.shape[0]
    route, counts = _router(tokens_hp, router_wp, router_b, tri, tm)
    cls = route[0].astype(jnp.int32)
    rank = route[1].astype(jnp.int32)
    cnt = counts[:N_CLASSES, 0].astype(jnp.int32)
    padded = (cnt + MOE_BLOCK - 1) // MOE_BLOCK * MOE_BLOCK
    pad_end = jnp.cumsum(padded)
    pos = (pad_end - padded)[cls] + rank
    n_blocks = -(-t // MOE_BLOCK) + N_CLASSES
    n_used = pad_end[-1] // MOE_BLOCK
    blk = jnp.minimum(jnp.arange(n_blocks, dtype=jnp.int32), n_used - 1)
    bcls = jnp.sum((pad_end[None, :] <= (blk * MOE_BLOCK)[:, None]).astype(jnp.int32), axis=1)
    bcls = jnp.minimum(bcls, N_CLASSES - 1)
    e_lo = EXPERTS_PER_GROUP * (bcls // N_PAIRS) + jnp.asarray(PAIR_LO, jnp.int32)[bcls % N_PAIRS]
    e_hi = EXPERTS_PER_GROUP * (bcls // N_PAIRS) + jnp.asarray(PAIR_HI, jnp.int32)[bcls % N_PAIRS]
    hs = _dispatch(pos, tokens_hp, n_blocks * MOE_BLOCK, tm)
    ys = _moe(e_lo, e_hi, n_used.reshape(1).astype(jnp.int32), hs, wgu, wd)
    return pos, route.T, ys


def kernel(x, c, ctx, c_ctx, w_mod, b_mod, norm_mix_g, norm_ffn_g, ev_w_in, ev_q_norm_g, ev_k_norm_g,
           ev_dw_w, ev_dw_b, ev_ln_g, ev_ln_b, ev_w_out, od_w_in, od_conv_w, od_w_out, router_w,
           router_bias, moe_w_gate, moe_w_up, moe_w_down):
    b, s, d = x.shape
    n_ctx = ctx.shape[1]
    depth = w_mod.shape[0]
    last_attn = (depth - 1) - ((depth - 1) % 2)
    tm = min(ROW_TILE, s)
    tmz = min(ROW_TILE, n_ctx)
    tq = min(Q_TILE, s)
    tqz = min(Q_TILE, n_ctx)
    assert s % tm == 0 and n_ctx % tmz == 0 and s % tq == 0 and n_ctx % tqz == 0 and s % GRID_W == 0

    r = -(-(b + 1) // 8) * 8
    c_all = jnp.zeros((r, d), F32).at[:b].set(c).at[b].set(c_ctx)
    mods = _modulation(c_all, w_mod, b_mod)

    def mod_x(l, j):
        return mods[l, :b, j * d:(j + 1) * d].reshape(b, 1, d)

    def mod_z(l, j):
        return mods[l, b:b + 1, j * d:(j + 1) * d].reshape(1, 1, d)

    cos_t, sin_t = _rope_tables(s)
    gmat = _head_mean_matrix()
    in_perm = _even_in_perm(ev_w_in.shape[2])
    out_perm = _even_out_perm(ev_w_out.shape[1])
    head_perm = np.asarray(_deinterleave(0), np.int32)
    router_wp = jnp.zeros((d, LANES), F32).at[:, :N_EXPERTS].set(router_w).astype(BF16)
    router_b = router_bias.astype(F32).reshape(N_EXPERTS, 1)
    tri_n = min(ROW_TILE, b * s)
    tri = jnp.asarray(np.triu(np.ones((tri_n, tri_n), np.float32), 1), BF16)

    xf = x.reshape(b * s, d)
    zf = ctx.reshape(b * n_ctx, d)
    pending = None
    ctm = min(Q_TILE, s, n_ctx)
    for l in range(depth):
        i = l // 2
        ctx_in = l <= last_attn
        ctx_out = l < last_attn
        gmix = norm_mix_g[l].reshape(1, d)
        gffn = norm_ffn_g[l].reshape(1, d)
        if l % 2 == 0:
            w_in = ev_w_in[i][:, in_perm].astype(BF16)
            w_out = ev_w_out[i][out_perm, :].astype(BF16)
            gq = ev_q_norm_g[i][head_perm] * (ATTN_SCALE * LOG2E)
            gk = ev_k_norm_g[i][head_perm]
            gqk = jnp.concatenate([jnp.tile(gq, N_HEADS), jnp.tile(gk, N_KV_HEADS)]).reshape(1, QK_WIDTH)
            conv_params = [ev_dw_w[i], ev_dw_b[i].reshape(1, -1), ev_ln_g[i].reshape(1, -1),
                           ev_ln_b[i].reshape(1, -1)]
            res = _inproj_even(xf, pending, gmix, mod_x(l, 0), mod_x(l, 1), s, w_in, gqk, cos_t, sin_t, gmat,
                               tm, True)
            if pending is not None:
                xf, res = res[0], res[1:]
            q, k, v, a = res
            k_all = k.reshape(b, s, KV_WIDTH)
            v_all = v.reshape(N_KV_HEADS, b, s, LANES)
            if ctx_in:
                qz, kz, vz, az = _inproj_even(zf, None, gmix, mod_z(l, 0), mod_z(l, 1), b * n_ctx, w_in, gqk,
                                              cos_t, sin_t, gmat, tmz, False)
                kz = kz.reshape(b, n_ctx, KV_WIDTH)
                vz = vz.reshape(N_KV_HEADS, b, n_ctx, LANES)
                k_all = jnp.concatenate([k_all, kz], axis=1)
                v_all = jnp.concatenate([v_all, vz], axis=2)
            attn = _attention(q, k_all, v_all, s, tq)
            xf, hp = _outproj(_outproj_even_kernel, "outproj_even", xf, [attn], a, conv_params, w_out,
                              mod_x(l, 2), gffn, mod_x(l, 3), mod_x(l, 4), s, s, tm)
            if ctx_out:
                attn_z = _attention(qz, kz, vz, n_ctx, tqz)
                zf, hpz = _outproj(_outproj_even_kernel, "outproj_even_ctx", zf, [attn_z], az, conv_params,
                                   w_out, mod_z(l, 2), gffn, mod_z(l, 3), mod_z(l, 4), b * n_ctx, n_ctx, tmz)
        else:
            w_in = od_w_in[i].astype(BF16)
            w_out = od_w_out[i].astype(BF16)
            conv_params = [od_conv_w[i]]
            res = _inproj_odd(xf, pending, gmix, mod_x(l, 0), mod_x(l, 1), s, w_in, tm)
            if pending is not None:
                xf, res = res[0], res[1:]
            gb, u = res
            xf, hp = _outproj(_outproj_odd_kernel, "outproj_odd", xf, [gb], u, conv_params, w_out,
                              mod_x(l, 2), gffn, mod_x(l, 3), mod_x(l, 4), s, s, tm)
            if ctx_out:
                gbz, uz = _inproj_odd(zf, None, gmix, mod_z(l, 0), mod_z(l, 1), b * n_ctx, w_in, tmz)
                zf, hpz = _outproj(_outproj_odd_kernel, "outproj_odd_ctx", zf, [gbz], uz, conv_params, w_out,
                                   mod_z(l, 2), gffn, mod_z(l, 3), mod_z(l, 4), b * n_ctx, n_ctx, tmz)

        wgu = jnp.concatenate([moe_w_gate[l], moe_w_up[l]], axis=-1).astype(BF16)
        wd = moe_w_down[l].astype(BF16)
        tokens = jnp.concatenate([hp, hpz], axis=0) if ctx_out else hp
        rtm = math.gcd(tri_n, math.gcd(b * s, b * n_ctx)) if ctx_out else tri_n
        pos, rc, ys = _moe_layer(tokens, router_wp, router_b, tri[:rtm, :rtm], wgu, wd, rtm)
        n_lat = b * s
        pending = (pos[:n_lat], rc[:n_lat], mod_x(l, 5), ys)
        if ctx_out:
            zf = _combine(pos[n_lat:], zf, rc[n_lat:], mod_z(l, 5), ys, b * n_ctx, ctm)
    pos, rc, gate, ys = pending
    return _combine(pos, xf, rc, gate, ys, s, ctm).reshape(b, s, d)
```

```python
import functools
import math

import numpy as np
import jax
import jax.numpy as jnp
from jax import lax
from jax.experimental import pallas as pl
from jax.experimental.pallas import tpu as pltpu

F32 = jnp.float32
BF16 = jnp.bfloat16
U32 = jnp.uint32

GRID_W = 64
EPS = 1e-6
N_HEADS = 8
N_KV_HEADS = 2
GQA_GROUP = N_HEADS // N_KV_HEADS
HEAD_DIM = 64
ATTN_WIDTH = N_HEADS * HEAD_DIM
KV_WIDTH = N_KV_HEADS * HEAD_DIM
QK_WIDTH = ATTN_WIDTH + KV_WIDTH
ATTN_SCALE = HEAD_DIM ** -0.5
ROPE_THETA = 10000.0
ROPE_PAIRS_PER_AXIS = HEAD_DIM // 4
CONV_A_K = 31
SC_K = 3
N_EXPERTS = 16
N_GROUPS = 4
EXPERTS_PER_GROUP = N_EXPERTS // N_GROUPS
D_EXPERT = 512
PAIR_LO = (0, 0, 0, 1, 1, 2)
PAIR_HI = (1, 2, 3, 2, 3, 3)
N_PAIRS = len(PAIR_LO)
N_CLASSES = N_GROUPS * N_PAIRS
CLASS_ROWS = 32

LANES = 128
SUBLANES = 8
MXU_COLS = 256
HALO = 16
VMEM_LIMIT = 48 * 1024 * 1024
ROW_TILE = 512
Q_TILE = 256
MOE_BLOCK = 512
CONV_ROWS = 64
LOG2E = math.log2(math.e)


def _cparams(*sem):
    return pltpu.CompilerParams(dimension_semantics=sem, vmem_limit_bytes=VMEM_LIMIT)


def _sigmoid(x):
    return 1.0 / (1.0 + jnp.exp(-x))


def _rms_mod(x, g, shift, scale):
    y = x * lax.rsqrt(jnp.mean(x * x, axis=-1, keepdims=True) + EPS) * g
    return y * (1.0 + scale) + shift


def _pack_halves(h):
    w = h.shape[1] // 2
    lo = lax.bitcast_convert_type(h[:, :w].astype(BF16).astype(F32), U32)
    hi = lax.bitcast_convert_type(h[:, w:].astype(BF16).astype(F32), U32)
    return (lo >> 16) | (hi & jnp.uint32(0xFFFF0000))


def _unpack_halves(w):
    lo = lax.bitcast_convert_type(w << 16, F32)
    hi = lax.bitcast_convert_type(w & jnp.uint32(0xFFFF0000), F32)
    return lo, hi


def _issue_row_gathers(pos_ref, ys_ref, ybuf, sems, tile, slot, rows):
    tm = ybuf.shape[1]
    for r in rows:
        pltpu.make_async_copy(ys_ref.at[pl.ds(pos_ref[tile * tm + r], 1)], ybuf.at[slot, pl.ds(r, 1)],
                              sems.at[slot]).start()


def _wait_row_gathers(ys_ref, ybuf, sems, slot):
    pltpu.make_async_copy(ys_ref.at[pl.ds(0, ybuf.shape[1])], ybuf.at[slot], sems.at[slot]).wait()


def _combined_residual(pos_ref, x_ref, rc_ref, gate_ref, ys_ref, ybuf, sems):
    tm = ybuf.shape[1]
    i = pl.program_id(0)
    n = pl.num_programs(0)

    @pl.when(i == 0)
    def _():
        _issue_row_gathers(pos_ref, ys_ref, ybuf, sems, 0, 0, range(tm))

    _wait_row_gathers(ys_ref, ybuf, sems, i % 2)
    ya, yb = _unpack_halves(ybuf[i % 2])
    rc = rc_ref[...]
    x_new = x_ref[...] + gate_ref[...] * (rc[:, 2:3] * ya + rc[:, 3:4] * yb)

    def issue(k, n_slices):
        _issue_row_gathers(pos_ref, ys_ref, ybuf, sems, jnp.minimum(i + 1, n - 1), (i + 1) % 2,
                           range(k * tm // n_slices, (k + 1) * tm // n_slices))
        if k == n_slices - 1:
            @pl.when(i == n - 1)
            def _():
                _wait_row_gathers(ys_ref, ybuf, sems, (i + 1) % 2)

    return x_new, issue


def _combine_in_specs(tm, d, rc_width, rows_per_mod):
    return [pl.BlockSpec((tm, d), lambda i, pos: (i, 0)),
            pl.BlockSpec((tm, rc_width), lambda i, pos: (i, 0)),
            pl.BlockSpec((None, 1, d), lambda i, pos: ((i * tm) // rows_per_mod, 0, 0)),
            pl.BlockSpec(memory_space=pl.ANY)]


def _combine_scratch(tm, d):
    return [pltpu.VMEM((2, tm, d), U32), pltpu.SemaphoreType.DMA((2,))]


def _mod_kernel(c_ref, w_ref, b_ref, o_ref):
    c = c_ref[...]
    a = (c * _sigmoid(c)).astype(BF16)
    o_ref[...] = jnp.dot(a, w_ref[...].astype(BF16), preferred_element_type=F32) + b_ref[...]


def _modulation(c_all, w_mod, b_mod):
    depth, d, n = w_mod.shape
    r = c_all.shape[0]
    nt = n // d
    return pl.pallas_call(
        _mod_kernel,
        out_shape=jax.ShapeDtypeStruct((depth, r, n), F32),
        grid=(depth, nt),
        in_specs=[pl.BlockSpec((r, d), lambda l, j: (0, 0)),
                  pl.BlockSpec((None, d, d), lambda l, j: (l, 0, j)),
                  pl.BlockSpec((None, 1, d), lambda l, j: (l, 0, j))],
        out_specs=pl.BlockSpec((None, r, d), lambda l, j: (l, 0, j)),
        compiler_params=_cparams("parallel", "parallel"),
        name="modulation",
    )(c_all, w_mod, b_mod.reshape(depth, 1, n))


def _inproj_kernel(*refs, body, n_in, combine):
    if combine:
        pos_ref, x_ref, rc_ref, gate_ref, ys_ref = refs[:5]
        ybuf, sems = refs[-2:]
        rest = refs[5:-2]
        x, issue = _combined_residual(pos_ref, x_ref, rc_ref, gate_ref, ys_ref, ybuf, sems)
        rest[3 + n_in][...] = x
        ins, outs = rest[:3 + n_in], rest[4 + n_in:]
    else:
        x = refs[0][...]
        issue = lambda k, n_slices: None
        ins, outs = refs[1:4 + n_in], refs[4 + n_in:]
    h = _rms_mod(x, ins[0][...], ins[1][...], ins[2][...]).astype(BF16)
    body(h, issue, *ins[3:], *outs)


def _inproj(body, name, x, pending, g, shift, scale, rows_per_mod, extra, extra_specs, out_shape, out_specs,
            tm):
    t, d = x.shape
    mod_map = lambda i, *_: ((i * tm) // rows_per_mod, 0, 0)
    in_specs = [pl.BlockSpec((1, d), lambda i, *_: (0, 0)), pl.BlockSpec((None, 1, d), mod_map),
                pl.BlockSpec((None, 1, d), mod_map)] + list(extra_specs)
    kern = functools.partial(_inproj_kernel, body=body, n_in=len(extra), combine=pending is not None)
    if pending is None:
        return pl.pallas_call(
            kern, out_shape=tuple(out_shape), grid=(t // tm,),
            in_specs=[pl.BlockSpec((tm, d), lambda i: (i, 0))] + in_specs, out_specs=tuple(out_specs),
            compiler_params=_cparams("parallel"), name=name,
        )(x, g, shift, scale, *extra)
    pos, rc, gate, ys = pending
    return pl.pallas_call(
        kern,
        out_shape=(jax.ShapeDtypeStruct((t, d), F32),) + tuple(out_shape),
        grid_spec=pltpu.PrefetchScalarGridSpec(
            num_scalar_prefetch=1, grid=(t // tm,),
            in_specs=_combine_in_specs(tm, d, rc.shape[1], rows_per_mod) + in_specs,
            out_specs=(pl.BlockSpec((tm, d), lambda i, pos: (i, 0)),) + tuple(out_specs),
            scratch_shapes=_combine_scratch(tm, d)),
        compiler_params=_cparams("arbitrary"), name=name + "_combine",
    )(pos, x, rc, gate, ys, g, shift, scale, *extra)


def _inproj_even_body(h, issue, w_ref, gqk_ref, cos_ref, sin_ref, gm_ref, q_ref, k_ref, v_ref, a_ref, *, rope):
    tm = h.shape[0]
    lane = lax.broadcasted_iota(jnp.int32, (tm, LANES), 1)
    first_head = lane < HEAD_DIM
    first_half = (lane % HEAD_DIM) < (HEAD_DIM // 2)
    c0 = QK_WIDTH + KV_WIDTH
    cw = (w_ref.shape[1] - c0) // 2
    n_glu = cw // MXU_COLS
    issue(0, 1)
    p_all = jnp.dot(h, w_ref[...], preferred_element_type=F32)
    proj = lambda lo, width: p_all[:, lo:lo + width]
    for m in range(QK_WIDTH // LANES):
        if m % (MXU_COLS // LANES) == 0:
            p = proj(LANES * m, MXU_COLS)
        y = p[:, LANES * (m % (MXU_COLS // LANES)):LANES * (m % (MXU_COLS // LANES) + 1)]
        sq = y * y
        sq_hi = sq.astype(BF16)
        sq_lo = (sq - sq_hi.astype(F32)).astype(BF16)
        ms = jnp.dot(jnp.concatenate([sq_hi, sq_lo], axis=1), gm_ref[...], preferred_element_type=F32)
        y = y * lax.rsqrt(ms + EPS) * gqk_ref[:, LANES * m:LANES * (m + 1)]
        if rope:
            partner = jnp.where(first_half, pltpu.roll(y, LANES - HEAD_DIM // 2, 1),
                                pltpu.roll(y, HEAD_DIM // 2, 1))
            y = y * cos_ref[...] + partner * sin_ref[...]
        if m < ATTN_WIDTH // LANES:
            q_ref[m] = jnp.where(first_head, y, 0.0).astype(BF16)
            q_ref[m + ATTN_WIDTH // LANES] = jnp.where(first_head, 0.0, y).astype(BF16)
        else:
            k_ref[...] = y.astype(BF16)
    v = p[:, MXU_COLS - KV_WIDTH:]
    v_ref[0] = jnp.where(first_head, v, 1.0).astype(BF16)
    v_ref[1] = jnp.where(first_head, 1.0, v).astype(BF16)
    for c in range(n_glu):
        a = proj(c0 + MXU_COLS * c, MXU_COLS)
        gate = proj(c0 + cw + MXU_COLS * c, MXU_COLS)
        a_ref[:, MXU_COLS * c:MXU_COLS * (c + 1)] = (a * _sigmoid(gate)).astype(BF16)


def _inproj_even(x, pending, g, shift, scale, rows_per_mod, w, gqk, cos_t, sin_t, gmat, tm, rope):
    t, d = x.shape
    n = w.shape[1]
    cw = (n - QK_WIDTH - KV_WIDTH) // 2
    n_pos = cos_t.shape[0] // tm
    return _inproj(
        functools.partial(_inproj_even_body, rope=rope), "inproj_even", x, pending, g, shift, scale,
        rows_per_mod, [w, gqk, cos_t, sin_t, gmat],
        [pl.BlockSpec((d, n), lambda i, *_: (0, 0)),
         pl.BlockSpec((1, QK_WIDTH), lambda i, *_: (0, 0)),
         pl.BlockSpec((tm, LANES), lambda i, *_: (i % n_pos, 0)),
         pl.BlockSpec((tm, LANES), lambda i, *_: (i % n_pos, 0)),
         pl.BlockSpec((2 * LANES, LANES), lambda i, *_: (0, 0))],
        [jax.ShapeDtypeStruct((N_HEADS, t, LANES), BF16), jax.ShapeDtypeStruct((t, KV_WIDTH), BF16),
         jax.ShapeDtypeStruct((N_KV_HEADS, t, LANES), BF16), jax.ShapeDtypeStruct((t, cw), BF16)],
        [pl.BlockSpec((N_HEADS, tm, LANES), lambda i, *_: (0, i, 0)),
         pl.BlockSpec((tm, KV_WIDTH), lambda i, *_: (i, 0)),
         pl.BlockSpec((N_KV_HEADS, tm, LANES), lambda i, *_: (0, i, 0)),
         pl.BlockSpec((tm, cw), lambda i, *_: (i, 0))],
        tm)


def _inproj_odd_body(h, issue, w_ref, gb_ref, u_ref):
    w = w_ref.shape[1] // 3
    n_phase = w // MXU_COLS
    proj = lambda lo: jnp.dot(h, w_ref[:, lo:lo + MXU_COLS], preferred_element_type=F32)
    for c in range(n_phase):
        issue(c, n_phase)
        cols = slice(MXU_COLS * c, MXU_COLS * (c + 1))
        gb_ref[:, cols] = proj(MXU_COLS * c).astype(BF16)
        u_ref[:, cols] = (proj(w + MXU_COLS * c) * proj(2 * w + MXU_COLS * c)).astype(BF16)


def _inproj_odd(x, pending, g, shift, scale, rows_per_mod, w, tm):
    t, d = x.shape
    n = w.shape[1]
    return _inproj(
        _inproj_odd_body, "inproj_odd", x, pending, g, shift, scale, rows_per_mod, [w],
        [pl.BlockSpec((d, n), lambda i, *_: (0, 0))],
        [jax.ShapeDtypeStruct((t, n // 3), BF16), jax.ShapeDtypeStruct((t, n // 3), BF16)],
        [pl.BlockSpec((tm, n // 3), lambda i, *_: (i, 0)), pl.BlockSpec((tm, n // 3), lambda i, *_: (i, 0))],
        tm)


def _attn_kernel(q_ref, k_ref, v_ref, o_ref, acc_ref):
    tq = q_ref.shape[1]
    n_pair = N_HEADS // N_KV_HEADS
    lane = lax.broadcasted_iota(jnp.int32, (tq, LANES), 1)

    def pair(m, carry):
        outs = []
        for g in range(N_KV_HEADS):
            q = q_ref[m + n_pair * g]
            s = lax.dot_general(q, k_ref[...], (((1,), (1,)), ((), ())), preferred_element_type=F32)
            p = jnp.exp2(s - jnp.max(s, axis=-1, keepdims=True)).astype(BF16)
            o = jnp.dot(p, v_ref[g], preferred_element_type=F32)
            l = o[:, HEAD_DIM:HEAD_DIM + 1] if g == 0 else o[:, 0:1]
            outs.append(o / l)
        acc_ref[m] = jnp.where(lane < HEAD_DIM, outs[0], outs[1])
        return carry

    lax.fori_loop(0, n_pair, pair, 0)
    o_ref[...] = jnp.concatenate([acc_ref[m] for m in range(n_pair)], axis=1).astype(BF16)


def _attention(q, k_all, v_all, s_q, tq):
    t = q.shape[1]
    b, skv, _ = k_all.shape
    nq = s_q // tq
    return pl.pallas_call(
        _attn_kernel,
        out_shape=jax.ShapeDtypeStruct((t, ATTN_WIDTH), BF16),
        grid=(b, nq),
        in_specs=[pl.BlockSpec((N_HEADS, tq, LANES), lambda bi, i: (0, bi * nq + i, 0)),
                  pl.BlockSpec((None, skv, LANES), lambda bi, i: (bi, 0, 0)),
                  pl.BlockSpec((N_KV_HEADS, None, skv, LANES), lambda bi, i: (0, bi, 0, 0))],
        out_specs=pl.BlockSpec((tq, ATTN_WIDTH), lambda bi, i: (bi * nq + i, 0)),
        scratch_shapes=[pltpu.VMEM((N_HEADS // N_KV_HEADS, tq, LANES), F32)],
        compiler_params=_cparams("parallel", "parallel"),
        name="attention",
    )(q, k_all, v_all)


def _fill_window(win_ref, cur_ref, prev_ref, next_ref, is_first, is_last):
    tm = cur_ref.shape[0]
    win_ref[0:HALO, :] = jnp.where(is_first, 0.0, prev_ref[...].astype(F32))
    win_ref[HALO:HALO + tm, :] = cur_ref[...].astype(F32)
    win_ref[HALO + tm:2 * HALO + tm, :] = jnp.where(is_last, 0.0, next_ref[...].astype(F32))


def _fill_shifted(win_ref, sh_ref):
    n = sh_ref.shape[1]
    for s in range(1, SUBLANES):
        sh_ref[s - 1] = win_ref[s:s + n, :]


def _dwconv_chunk(win_ref, sh_ref, w_ref, r0, rows, taps):
    off = HALO - taps // 2
    acc = None
    for kk in range(taps):
        q, s = divmod(off + kk, SUBLANES)
        if sh_ref is None:
            src = win_ref[r0 + off + kk:r0 + off + kk + rows, :]
        elif s == 0:
            src = win_ref[r0 + SUBLANES * q:r0 + SUBLANES * q + rows, :]
        else:
            src = sh_ref[s - 1, r0 + SUBLANES * q:r0 + SUBLANES * q + rows, :]
        term = src.reshape(rows // SUBLANES, SUBLANES, -1) * w_ref[SUBLANES * kk:SUBLANES * (kk + 1), :]
        acc = term if acc is None else acc + term
    return acc.reshape(rows, -1)


def _residual_tail(x_ref, o, gate_ref, g2_ref, sh2_ref, sc2_ref, xo_ref, hp_ref):
    x_new = x_ref[...] + gate_ref[...] * o
    xo_ref[...] = x_new
    hp_ref[...] = _pack_halves(_rms_mod(x_new, g2_ref[...], sh2_ref[...], sc2_ref[...]))


def _outproj_even_kernel(x_ref, at_ref, a_ref, ap_ref, an_ref, dw_ref, db_ref, lg_ref, lb_ref, w_ref,
                         gate_ref, g2_ref, sh2_ref, sc2_ref, xo_ref, hp_ref, win_ref, cv_ref, shw_ref,
                         *, tiles_per_seq):
    tm = x_ref.shape[0]
    i = pl.program_id(0) % tiles_per_seq
    _fill_window(win_ref, a_ref, ap_ref, an_ref, i == 0, i == tiles_per_seq - 1)
    _fill_shifted(win_ref, shw_ref)
    for r0 in range(0, tm, CONV_ROWS):
        y = _dwconv_chunk(win_ref, shw_ref, dw_ref, r0, CONV_ROWS, CONV_A_K) + db_ref[...]
        mu = jnp.mean(y, axis=-1, keepdims=True)
        yc = y - mu
        var = jnp.mean(yc * yc, axis=-1, keepdims=True)
        yn = yc * lax.rsqrt(var + EPS) * lg_ref[...] + lb_ref[...]
        cv_ref[r0:r0 + CONV_ROWS, :] = (yn * _sigmoid(yn)).astype(BF16)
    aw = at_ref.shape[1]
    o = (jnp.dot(at_ref[...], w_ref[0:aw, :], preferred_element_type=F32)
         + jnp.dot(cv_ref[...], w_ref[aw:, :], preferred_element_type=F32))
    _residual_tail(x_ref, o, gate_ref, g2_ref, sh2_ref, sc2_ref, xo_ref, hp_ref)


def _outproj_odd_kernel(x_ref, gb_ref, u_ref, up_ref, un_ref, cw_ref, w_ref,
                        gate_ref, g2_ref, sh2_ref, sc2_ref, xo_ref, hp_ref, win_ref, cv_ref,
                        *, tiles_per_seq):
    tm = x_ref.shape[0]
    i = pl.program_id(0) % tiles_per_seq
    _fill_window(win_ref, u_ref, up_ref, un_ref, i == 0, i == tiles_per_seq - 1)
    for r0 in range(0, tm, CONV_ROWS):
        y = _dwconv_chunk(win_ref, None, cw_ref, r0, CONV_ROWS, SC_K)
        cv_ref[r0:r0 + CONV_ROWS, :] = (gb_ref[r0:r0 + CONV_ROWS, :].astype(F32) * y).astype(BF16)
    o = jnp.dot(cv_ref[...], w_ref[...], preferred_element_type=F32)
    _residual_tail(x_ref, o, gate_ref, g2_ref, sh2_ref, sc2_ref, xo_ref, hp_ref)


def _halo_specs(tm, width, n_rows):
    hb = tm // HALO
    last = n_rows // HALO - 1
    return [pl.BlockSpec((tm, width), lambda i: (i, 0)),
            pl.BlockSpec((HALO, width), lambda i: (jnp.maximum(i * hb - 1, 0), 0)),
            pl.BlockSpec((HALO, width), lambda i: (jnp.minimum((i + 1) * hb, last), 0))]


def _outproj(kernel_fn, name, x, lead, conv_in, params, w, gate, g2, shift2, scale2, rows_per_mod,
             seq_len, tm, shifted_window):
    t, d = x.shape
    cw = conv_in.shape[1]
    scratch = [pltpu.VMEM((tm + 2 * HALO, cw), F32), pltpu.VMEM((tm, cw), BF16)]
    if shifted_window:
        scratch.append(pltpu.VMEM((SUBLANES - 1, tm + 2 * HALO - SUBLANES, cw), F32))
    mod_map = lambda i: ((i * tm) // rows_per_mod, 0, 0)
    row_spec = lambda a: pl.BlockSpec((tm, a.shape[1]), lambda i: (i, 0))
    full_spec = lambda a: pl.BlockSpec(a.shape, lambda i: (0,) * a.ndim)
    in_specs = ([row_spec(x)] + [row_spec(a) for a in lead] + _halo_specs(tm, cw, t)
                + [full_spec(a) for a in params] + [full_spec(w)]
                + [pl.BlockSpec((None, 1, d), mod_map), full_spec(g2),
                   pl.BlockSpec((None, 1, d), mod_map), pl.BlockSpec((None, 1, d), mod_map)])
    return pl.pallas_call(
        functools.partial(kernel_fn, tiles_per_seq=seq_len // tm),
        out_shape=(jax.ShapeDtypeStruct((t, d), F32), jax.ShapeDtypeStruct((t, d // 2), U32)),
        grid=(t // tm,),
        in_specs=in_specs,
        out_specs=(pl.BlockSpec((tm, d), lambda i: (i, 0)), pl.BlockSpec((tm, d // 2), lambda i: (i, 0))),
        scratch_shapes=scratch,
        compiler_params=_cparams("parallel"),
        name=name,
    )(x, *lead, conv_in, conv_in, conv_in, *params, w, gate, g2, shift2, scale2)


def _router_kernel(hp_ref, rw_ref, bias_ref, tri_ref, route_ref, cnt_ref, base_ref):
    tm = hp_ref.shape[0]

    @pl.when(pl.program_id(0) == 0)
    def _():
        base_ref[...] = jnp.zeros_like(base_ref)

    lo, hi = _unpack_halves(hp_ref[...])
    h = jnp.concatenate([lo, hi], axis=1).astype(BF16)
    logits = jnp.dot(h, rw_ref[...], preferred_element_type=F32)
    lt = logits.T[0:N_EXPERTS, :]
    aff = _sigmoid(lt)
    sel = aff + bias_ref[...]
    rows = lambda a, e: a[e:e + 1, :]

    best = None
    for g in range(N_GROUPS):
        a, b, c, d = (rows(sel, EXPERTS_PER_GROUP * g + j) for j in range(EXPERTS_PER_GROUP))
        score = jnp.maximum(jnp.maximum(a + b, c + d), jnp.maximum(a, b) + jnp.maximum(c, d))
        if best is None:
            best, grp = score, jnp.zeros((1, tm), jnp.int32)
        else:
            better = score > best
            grp = jnp.where(better, g, grp)
            best = jnp.where(better, score, best)

    def in_group(a, j):
        out = rows(a, j)
        for g in range(1, N_GROUPS):
            out = jnp.where(grp == g, rows(a, EXPERTS_PER_GROUP * g + j), out)
        return out

    s = [in_group(sel, j) for j in range(EXPERTS_PER_GROUP)]
    f = [in_group(aff, j) for j in range(EXPERTS_PER_GROUP)]

    def first_argmax(vals):
        m = functools.reduce(jnp.maximum, vals)
        idx = jnp.full((1, tm), len(vals) - 1, jnp.int32)
        for j in range(len(vals) - 2, -1, -1):
            idx = jnp.where(vals[j] == m, j, idx)
        return idx

    i1 = first_argmax(s)
    i2 = first_argmax([jnp.where(i1 == j, -jnp.inf, s[j]) for j in range(EXPERTS_PER_GROUP)])
    e_lo = jnp.minimum(i1, i2)
    e_hi = jnp.maximum(i1, i2)
    pick = lambda vals, idx: functools.reduce(
        lambda acc, j: jnp.where(idx == j, vals[j], acc), range(1, len(vals)), vals[0])
    f_lo, f_hi = pick(f, e_lo), pick(f, e_hi)
    denom = f_lo + f_hi
    pair = jnp.where(e_lo == 0, e_hi - 1, jnp.where(e_lo == 1, e_hi + 1, N_PAIRS - 1))
    cls = grp * N_PAIRS + pair

    onehot = (lax.broadcasted_iota(jnp.int32, (CLASS_ROWS, tm), 0) == cls)
    onehot_f = onehot.astype(F32)
    before = jnp.dot(onehot_f.astype(BF16), tri_ref[...], preferred_element_type=F32) + base_ref[:, 0:1]
    rank = jnp.sum(onehot_f * before, axis=0, keepdims=True)
    base_ref[...] = base_ref[...] + jnp.sum(onehot_f, axis=1, keepdims=True)

    route_ref[...] = jnp.concatenate(
        [cls.astype(F32), rank, f_lo / denom, f_hi / denom, jnp.zeros((4, tm), F32)], axis=0)
    cnt_ref[...] = base_ref[...]


def _router(hp, rw, bias, tri, tm):
    t, hw = hp.shape
    return pl.pallas_call(
        _router_kernel,
        out_shape=(jax.ShapeDtypeStruct((8, t), F32), jax.ShapeDtypeStruct((CLASS_ROWS, LANES), F32)),
        grid=(t // tm,),
        in_specs=[pl.BlockSpec((tm, hw), lambda i: (i, 0)),
                  pl.BlockSpec(rw.shape, lambda i: (0, 0)),
                  pl.BlockSpec(bias.shape, lambda i: (0, 0)),
                  pl.BlockSpec((tm, tm), lambda i: (0, 0))],
        out_specs=(pl.BlockSpec((8, tm), lambda i: (0, i)),
                   pl.BlockSpec((CLASS_ROWS, LANES), lambda i: (0, 0))),
        scratch_shapes=[pltpu.VMEM((CLASS_ROWS, LANES), F32)],
        compiler_params=_cparams("arbitrary"),
        name="router",
    )(hp, rw, bias, tri)


def _dispatch_kernel(pos_ref, hp_ref, zero_ref, hs_ref, sem):
    del zero_ref
    tm = hp_ref.shape[0]
    base = pl.program_id(0) * tm

    for r in range(tm):
        pltpu.make_async_copy(hp_ref.at[pl.ds(r, 1)], hs_ref.at[pl.ds(pos_ref[base + r], 1)], sem).start()
    pltpu.make_async_copy(hp_ref, hs_ref.at[pl.ds(0, tm)], sem).wait()


def _dispatch(pos, hp, n_rows, tm):
    t, hw = hp.shape
    zeros = jnp.zeros((n_rows, hw), U32)
    return pl.pallas_call(
        _dispatch_kernel,
        out_shape=jax.ShapeDtypeStruct((n_rows, hw), U32),
        grid_spec=pltpu.PrefetchScalarGridSpec(
            num_scalar_prefetch=1,
            grid=(t // tm,),
            in_specs=[pl.BlockSpec((tm, hw), lambda i, pos: (i, 0)),
                      pl.BlockSpec(memory_space=pl.ANY)],
            out_specs=pl.BlockSpec(memory_space=pl.ANY),
            scratch_shapes=[pltpu.SemaphoreType.DMA(())]),
        input_output_aliases={2: 0},
        compiler_params=_cparams("arbitrary"),
        name="dispatch",
    )(pos, hp, zeros)


def _moe_kernel(elo_ref, ehi_ref, nused_ref, hs_ref, wgu_lo, wd_lo, wgu_hi, wd_hi, ys_ref):
    del elo_ref, ehi_ref

    @pl.when(pl.program_id(0) >= nused_ref[0])
    def _():
        ys_ref[...] = jnp.zeros_like(ys_ref)

    @pl.when(pl.program_id(0) < nused_ref[0])
    def _():
        lo, hi = _unpack_halves(hs_ref[...])
        h = jnp.concatenate([lo, hi], axis=1).astype(BF16)

        def expert(wgu_ref, wd_ref):
            gu = jnp.dot(h, wgu_ref[...], preferred_element_type=F32)
            g, u = gu[:, :D_EXPERT], gu[:, D_EXPERT:]
            act = (g * _sigmoid(g) * u).astype(BF16)
            return jnp.dot(act, wd_ref[...], preferred_element_type=F32)

        ya = lax.bitcast_convert_type(expert(wgu_lo, wd_lo).astype(BF16).astype(F32), U32)
        yb = lax.bitcast_convert_type(expert(wgu_hi, wd_hi).astype(BF16).astype(F32), U32)
        ys_ref[...] = (ya >> 16) | (yb & jnp.uint32(0xFFFF0000))


def _moe(e_lo, e_hi, n_used, hs, wgu, wd):
    n_rows, hw = hs.shape
    d = wd.shape[2]
    n_blocks = n_rows // MOE_BLOCK
    row_map = lambda j, lo, hi, nu: (jnp.minimum(j, nu[0] - 1), 0)
    return pl.pallas_call(
        _moe_kernel,
        out_shape=jax.ShapeDtypeStruct((n_rows, d), U32),
        grid_spec=pltpu.PrefetchScalarGridSpec(
            num_scalar_prefetch=3,
            grid=(n_blocks,),
            in_specs=[pl.BlockSpec((MOE_BLOCK, hw), row_map),
                      pl.BlockSpec((None,) + wgu.shape[1:], lambda j, lo, hi, nu: (lo[j], 0, 0)),
                      pl.BlockSpec((None,) + wd.shape[1:], lambda j, lo, hi, nu: (lo[j], 0, 0)),
                      pl.BlockSpec((None,) + wgu.shape[1:], lambda j, lo, hi, nu: (hi[j], 0, 0)),
                      pl.BlockSpec((None,) + wd.shape[1:], lambda j, lo, hi, nu: (hi[j], 0, 0))],
            out_specs=pl.BlockSpec((MOE_BLOCK, d), lambda j, lo, hi, nu: (j, 0))),
        compiler_params=_cparams("arbitrary"),
        name="experts",
    )(e_lo, e_hi, n_used, hs, wgu, wd, wgu, wd)


def _combine_kernel(pos_ref, x_ref, rc_ref, gate_ref, ys_ref, xo_ref, ybuf, sems):
    x_new, issue = _combined_residual(pos_ref, x_ref, rc_ref, gate_ref, ys_ref, ybuf, sems)
    xo_ref[...] = x_new
    issue(0, 1)


def _combine(pos, x, rc, gate, ys, rows_per_mod, tm):
    t, d = x.shape
    return pl.pallas_call(
        _combine_kernel,
        out_shape=jax.ShapeDtypeStruct((t, d), F32),
        grid_spec=pltpu.PrefetchScalarGridSpec(
            num_scalar_prefetch=1,
            grid=(t // tm,),
            in_specs=_combine_in_specs(tm, d, rc.shape[1], rows_per_mod),
            out_specs=pl.BlockSpec((tm, d), lambda i, pos: (i, 0)),
            scratch_shapes=_combine_scratch(tm, d)),
        compiler_params=_cparams("arbitrary"),
        name="combine",
    )(pos, x, rc, gate, ys)


def _deinterleave(base):
    return [base + 2 * p for p in range(HEAD_DIM // 2)] + [base + 2 * p + 1 for p in range(HEAD_DIM // 2)]


def _even_in_perm(n_cols):
    cols = []
    half = N_HEADS // N_KV_HEADS
    for m in range(half):
        cols += _deinterleave(HEAD_DIM * m) + _deinterleave(HEAD_DIM * (m + half))
    for kh in range(N_KV_HEADS):
        cols += _deinterleave(ATTN_WIDTH + HEAD_DIM * kh)
    cols += list(range(QK_WIDTH, n_cols))
    return np.asarray(cols, np.int32)


def _even_out_perm(n_rows):
    rows = []
    half = N_HEADS // N_KV_HEADS
    for m in range(half):
        rows += list(range(HEAD_DIM * m, HEAD_DIM * (m + 1)))
        rows += list(range(HEAD_DIM * (m + half), HEAD_DIM * (m + half + 1)))
    rows += list(range(ATTN_WIDTH, n_rows))
    return np.asarray(rows, np.int32)


def _rope_tables(n_tokens):
    rows = n_tokens // GRID_W
    row = jnp.repeat(jnp.arange(rows, dtype=F32), GRID_W)
    col = jnp.tile(jnp.arange(GRID_W, dtype=F32), rows)
    inv = ROPE_THETA ** (-jnp.arange(ROPE_PAIRS_PER_AXIS, dtype=F32) / ROPE_PAIRS_PER_AXIS)
    ang = jnp.concatenate([row[:, None] * inv, col[:, None] * inv], axis=-1)
    cos, sin = jnp.cos(ang), jnp.sin(ang)
    reps = LANES // HEAD_DIM
    return (jnp.tile(jnp.concatenate([cos, cos], axis=-1), (1, reps)),
            jnp.tile(jnp.concatenate([-sin, sin], axis=-1), (1, reps)))


def _head_mean_matrix():
    blk = np.kron(np.eye(LANES // HEAD_DIM), np.full((HEAD_DIM, HEAD_DIM), 1.0 / HEAD_DIM))
    return jnp.asarray(np.concatenate([blk, blk], axis=0), BF16)


def _moe_layer(tokens_hp, router_wp, router_b, tri, wgu, wd, tm):
    t = tokens_hp.shape[0]
    route, counts = _router(tokens_hp, router_wp, router_b, tri, tm)
    cls = route[0].astype(jnp.int32)
    rank = route[1].astype(jnp.int32)
    cnt = counts[:N_CLASSES, 0].astype(jnp.int32)
    padded = (cnt + MOE_BLOCK - 1) // MOE_BLOCK * MOE_BLOCK
    pad_end = jnp.cumsum(padded)
    pos = (pad_end - padded)[cls] + rank
    n_blocks = -(-t // MOE_BLOCK) + N_CLASSES
    n_used = pad_end[-1] // MOE_BLOCK
    blk = jnp.minimum(jnp.arange(n_blocks, dtype=jnp.int32), n_used - 1)
    bcls = jnp.sum((pad_end[None, :] <= (blk * MOE_BLOCK)[:, None]).astype(jnp.int32), axis=1)
    bcls = jnp.minimum(bcls, N_CLASSES - 1)
    e_lo = EXPERTS_PER_GROUP * (bcls // N_PAIRS) + jnp.asarray(PAIR_LO, jnp.int32)[bcls % N_PAIRS]
    e_hi = EXPERTS_PER_GROUP * (bcls // N_PAIRS) + jnp.asarray(PAIR_HI, jnp.int32)[bcls % N_PAIRS]
    hs = _dispatch(pos, tokens_hp, n_blocks * MOE_BLOCK, tm)
    ys = _moe(e_lo, e_hi, n_used.reshape(1).astype(jnp.int32), hs, wgu, wd)
    return pos, route.T, ys


def kernel(x, c, ctx, c_ctx, w_mod, b_mod, norm_mix_g, norm_ffn_g, ev_w_in, ev_q_norm_g, ev_k_norm_g,
           ev_dw_w, ev_dw_b, ev_ln_g, ev_ln_b, ev_w_out, od_w_in, od_conv_w, od_w_out, router_w,
           router_bias, moe_w_gate, moe_w_up, moe_w_down):
    b, s, d = x.shape
    n_ctx = ctx.shape[1]
    depth = w_mod.shape[0]
    last_attn = (depth - 1) - ((depth - 1) % 2)
    tm = min(ROW_TILE, s)
    tmz = min(ROW_TILE, n_ctx)
    tq = min(Q_TILE, s)
    tqz = min(Q_TILE, n_ctx)
    assert s % tm == 0 and n_ctx % tmz == 0 and s % tq == 0 and n_ctx % tqz == 0 and s % GRID_W == 0

    r = -(-(b + 1) // 8) * 8
    c_all = jnp.zeros((r, d), F32).at[:b].set(c).at[b].set(c_ctx)
    mods = _modulation(c_all, w_mod, b_mod)

    def mod_x(l, j):
        return mods[l, :b, j * d:(j + 1) * d].reshape(b, 1, d)

    def mod_z(l, j):
        return mods[l, b:b + 1, j * d:(j + 1) * d].reshape(1, 1, d)

    cos_t, sin_t = _rope_tables(s)
    gmat = _head_mean_matrix()
    in_perm = _even_in_perm(ev_w_in.shape[2])
    out_perm = _even_out_perm(ev_w_out.shape[1])
    head_perm = np.asarray(_deinterleave(0), np.int32)
    router_wp = jnp.zeros((d, LANES), F32).at[:, :N_EXPERTS].set(router_w).astype(BF16)
    router_b = router_bias.astype(F32).reshape(N_EXPERTS, 1)
    tri_n = min(ROW_TILE, b * s)
    tri = jnp.asarray(np.triu(np.ones((tri_n, tri_n), np.float32), 1), BF16)

    xf = x.reshape(b * s, d)
    zf = ctx.reshape(b * n_ctx, d)
    pending = None
    ctm = min(Q_TILE, s, n_ctx)
    for l in range(depth):
        i = l // 2
        ctx_in = l <= last_attn
        ctx_out = l < last_attn
        gmix = norm_mix_g[l].reshape(1, d)
        gffn = norm_ffn_g[l].reshape(1, d)
        if l % 2 == 0:
            w_in = ev_w_in[i][:, in_perm].astype(BF16)
            w_out = ev_w_out[i][out_perm, :].astype(BF16)
            gq = ev_q_norm_g[i][head_perm] * (ATTN_SCALE * LOG2E)
            gk = ev_k_norm_g[i][head_perm]
            gqk = jnp.concatenate([jnp.tile(gq, N_HEADS), jnp.tile(gk, N_KV_HEADS)]).reshape(1, QK_WIDTH)
            conv_params = [jnp.repeat(ev_dw_w[i], SUBLANES, axis=0), ev_dw_b[i].reshape(1, -1),
                           ev_ln_g[i].reshape(1, -1),
                           ev_ln_b[i].reshape(1, -1)]
            res = _inproj_even(xf, pending, gmix, mod_x(l, 0), mod_x(l, 1), s, w_in, gqk, cos_t, sin_t, gmat,
                               tm, True)
            if pending is not None:
                xf, res = res[0], res[1:]
            q, k, v, a = res
            k_all = k.reshape(b, s, KV_WIDTH)
            v_all = v.reshape(N_KV_HEADS, b, s, LANES)
            if ctx_in:
                qz, kz, vz, az = _inproj_even(zf, None, gmix, mod_z(l, 0), mod_z(l, 1), b * n_ctx, w_in, gqk,
                                              cos_t, sin_t, gmat, tmz, False)
                kz = kz.reshape(b, n_ctx, KV_WIDTH)
                vz = vz.reshape(N_KV_HEADS, b, n_ctx, LANES)
                k_all = jnp.concatenate([k_all, kz], axis=1)
                v_all = jnp.concatenate([v_all, vz], axis=2)
            attn = _attention(q, k_all, v_all, s, tq)
            xf, hp = _outproj(_outproj_even_kernel, "outproj_even", xf, [attn], a, conv_params, w_out,
                              mod_x(l, 2), gffn, mod_x(l, 3), mod_x(l, 4), s, s, tm, True)
            if ctx_out:
                attn_z = _attention(qz, kz, vz, n_ctx, tqz)
                zf, hpz = _outproj(_outproj_even_kernel, "outproj_even_ctx", zf, [attn_z], az, conv_params,
                                   w_out, mod_z(l, 2), gffn, mod_z(l, 3), mod_z(l, 4), b * n_ctx, n_ctx, tmz,
                                   True)
        else:
            w_in = od_w_in[i].astype(BF16)
            w_out = od_w_out[i].astype(BF16)
            conv_params = [jnp.repeat(od_conv_w[i], SUBLANES, axis=0)]
            res = _inproj_odd(xf, pending, gmix, mod_x(l, 0), mod_x(l, 1), s, w_in, tm)
            if pending is not None:
                xf, res = res[0], res[1:]
            gb, u = res
            xf, hp = _outproj(_outproj_odd_kernel, "outproj_odd", xf, [gb], u, conv_params, w_out,
                              mod_x(l, 2), gffn, mod_x(l, 3), mod_x(l, 4), s, s, tm, False)
            if ctx_out:
                gbz, uz = _inproj_odd(zf, None, gmix, mod_z(l, 0), mod_z(l, 1), b * n_ctx, w_in, tmz)
                zf, hpz = _outproj(_outproj_odd_kernel, "outproj_odd_ctx", zf, [gbz], uz, conv_params, w_out,
                                   mod_z(l, 2), gffn, mod_z(l, 3), mod_z(l, 4), b * n_ctx, n_ctx, tmz, False)

        wgu = jnp.concatenate([moe_w_gate[l], moe_w_up[l]], axis=-1).astype(BF16)
        wd = moe_w_down[l].astype(BF16)
        tokens = jnp.concatenate([hp, hpz], axis=0) if ctx_out else hp
        rtm = math.gcd(tri_n, math.gcd(b * s, b * n_ctx)) if ctx_out else tri_n
        pos, rc, ys = _moe_layer(tokens, router_wp, router_b, tri[:rtm, :rtm], wgu, wd, rtm)
        n_lat = b * s
        pending = (pos[:n_lat], rc[:n_lat], mod_x(l, 5), ys)
        if ctx_out:
            zf = _combine(pos[n_lat:], zf, rc[n_lat:], mod_z(l, 5), ys, b * n_ctx, ctm)
    pos, rc, gate, ys = pending
    return _combine(pos, xf, rc, gate, ys, s, ctm).reshape(b, s, d)
```

```python
import functools
import math

import numpy as np
import jax
import jax.numpy as jnp
from jax import lax
from jax.experimental import pallas as pl
from jax.experimental.pallas import tpu as pltpu

F32 = jnp.float32
BF16 = jnp.bfloat16
U32 = jnp.uint32

GRID_W = 64
EPS = 1e-6
N_HEADS = 8
N_KV_HEADS = 2
GQA_GROUP = N_HEADS // N_KV_HEADS
HEAD_DIM = 64
ATTN_WIDTH = N_HEADS * HEAD_DIM
KV_WIDTH = N_KV_HEADS * HEAD_DIM
QK_WIDTH = ATTN_WIDTH + KV_WIDTH
ATTN_SCALE = HEAD_DIM ** -0.5
ROPE_THETA = 10000.0
ROPE_PAIRS_PER_AXIS = HEAD_DIM // 4
CONV_A_K = 31
SC_K = 3
N_EXPERTS = 16
N_GROUPS = 4
EXPERTS_PER_GROUP = N_EXPERTS // N_GROUPS
D_EXPERT = 512
PAIR_LO = (0, 0, 0, 1, 1, 2)
PAIR_HI = (1, 2, 3, 2, 3, 3)
N_PAIRS = len(PAIR_LO)
N_CLASSES = N_GROUPS * N_PAIRS
CLASS_ROWS = 32

LANES = 128
SUBLANES = 8
MXU_COLS = 256
HALO = 16
VMEM_LIMIT = 48 * 1024 * 1024
ROW_TILE = 512
Q_TILE = 256
ROUTER_TILE = 1024
DMA_QUEUES = 2
MOE_BLOCK = 512
CONV_ROWS = 64
LOG2E = math.log2(math.e)


def _cparams(*sem):
    return pltpu.CompilerParams(dimension_semantics=sem, vmem_limit_bytes=VMEM_LIMIT)


def _sigmoid(x):
    return 1.0 / (1.0 + jnp.exp(-x))


def _rms_mod(x, g, shift, scale):
    y = x * lax.rsqrt(jnp.mean(x * x, axis=-1, keepdims=True) + EPS) * g
    return y * (1.0 + scale) + shift


def _pack_halves(h):
    w = h.shape[1] // 2
    lo = lax.bitcast_convert_type(h[:, :w].astype(BF16).astype(F32), U32)
    hi = lax.bitcast_convert_type(h[:, w:].astype(BF16).astype(F32), U32)
    return (lo >> 16) | (hi & jnp.uint32(0xFFFF0000))


def _unpack_halves(w):
    lo = lax.bitcast_convert_type(w << 16, F32)
    hi = lax.bitcast_convert_type(w & jnp.uint32(0xFFFF0000), F32)
    return lo, hi


def _combine_halves(pos_ref, x_ref, rc_ref, gate_ref, ys_ref, bufs, sems, consume):
    tm = bufs[0].shape[0]
    i = pl.program_id(0)
    n = pl.num_programs(0)

    def issue(tile, half):
        for r in range(tm):
            pltpu.async_copy(ys_ref.at[pl.ds(pos_ref[tile * tm + r], 1)], bufs[half].at[pl.ds(r, 1)],
                             sems.at[half], priority=r % DMA_QUEUES)

    def wait(half):
        pltpu.make_async_copy(ys_ref.at[pl.ds(0, tm)], bufs[half], sems.at[half]).wait()

    @pl.when(i == 0)
    def _():
        issue(0, 0)

    for half in range(2):
        wait(half)
        issue(2 * i + 1 if half == 0 else jnp.minimum(2 * i + 2, 2 * n - 2), 1 - half)
        rows = slice(half * tm, (half + 1) * tm)
        ya, yb = _unpack_halves(bufs[half][...])
        rc = rc_ref[rows, :]
        consume(half, x_ref[rows, :] + gate_ref[...] * (rc[:, 2:3] * ya + rc[:, 3:4] * yb))

    @pl.when(i == n - 1)
    def _():
        wait(0)


def _combine_in_specs(tm, d, rc_width, rows_per_mod):
    return [pl.BlockSpec((2 * tm, d), lambda i, pos: (i, 0)),
            pl.BlockSpec((2 * tm, rc_width), lambda i, pos: (i, 0)),
            pl.BlockSpec((None, 1, d), lambda i, pos: ((i * 2 * tm) // rows_per_mod, 0, 0)),
            pl.BlockSpec(memory_space=pl.ANY)]


def _combine_scratch(tm, d):
    return [pltpu.VMEM((tm, d), U32), pltpu.VMEM((tm, d), U32), pltpu.SemaphoreType.DMA((2,))]


def _mod_kernel(c_ref, w_ref, b_ref, o_ref):
    c = c_ref[...]
    a = (c * _sigmoid(c)).astype(BF16)
    o_ref[...] = jnp.dot(a, w_ref[...].astype(BF16), preferred_element_type=F32) + b_ref[...]


def _modulation(c_all, w_mod, b_mod):
    depth, d, n = w_mod.shape
    r = c_all.shape[0]
    nt = n // d
    return pl.pallas_call(
        _mod_kernel,
        out_shape=jax.ShapeDtypeStruct((depth, r, n), F32),
        grid=(depth, nt),
        in_specs=[pl.BlockSpec((r, d), lambda l, j: (0, 0)),
                  pl.BlockSpec((None, d, d), lambda l, j: (l, 0, j)),
                  pl.BlockSpec((None, 1, d), lambda l, j: (l, 0, j))],
        out_specs=pl.BlockSpec((None, r, d), lambda l, j: (l, 0, j)),
        compiler_params=_cparams("parallel", "parallel"),
        name="modulation",
    )(c_all, w_mod, b_mod.reshape(depth, 1, n))


def _row_view(ref, axis, half, tm):
    if axis is None:
        return ref
    return ref.at[(slice(None),) * axis + (pl.ds(half * tm, tm),)]


def _inproj_kernel(*refs, body, in_axes, out_axes, combine, tm):
    n_in = len(in_axes)
    if combine:
        pos_ref, x_ref, rc_ref, gate_ref, ys_ref = refs[:5]
        buf_a, buf_b, sems = refs[-3:]
        rest = refs[5:-3]
        ins, xo_ref, outs = rest[:3 + n_in], rest[3 + n_in], rest[4 + n_in:]

        def consume(half, x):
            xo_ref[half * tm:(half + 1) * tm, :] = x
            h = _rms_mod(x, ins[0][...], ins[1][...], ins[2][...]).astype(BF16)
            body(h, *[_row_view(r, a, half, tm) for r, a in zip(ins[3:], in_axes)],
                 *[_row_view(r, a, half, tm) for r, a in zip(outs, out_axes)])

        _combine_halves(pos_ref, x_ref, rc_ref, gate_ref, ys_ref, (buf_a, buf_b), sems, consume)
    else:
        ins, outs = refs[1:4 + n_in], refs[4 + n_in:]
        h = _rms_mod(refs[0][...], ins[0][...], ins[1][...], ins[2][...]).astype(BF16)
        body(h, *ins[3:], *outs)


def _inproj(body, name, x, pending, g, shift, scale, rows_per_mod, extra, extra_specs, in_axes, out_shape,
            out_specs, out_axes, tm):
    t, d = x.shape
    bt = tm if pending is None else 2 * tm
    mod_map = lambda i, *_: ((i * bt) // rows_per_mod, 0, 0)
    in_specs = [pl.BlockSpec((1, d), lambda i, *_: (0, 0)), pl.BlockSpec((None, 1, d), mod_map),
                pl.BlockSpec((None, 1, d), mod_map)] + list(extra_specs)
    kern = functools.partial(_inproj_kernel, body=body, in_axes=tuple(in_axes), out_axes=tuple(out_axes),
                             combine=pending is not None, tm=tm)
    if pending is None:
        return pl.pallas_call(
            kern, out_shape=tuple(out_shape), grid=(t // bt,),
            in_specs=[pl.BlockSpec((bt, d), lambda i: (i, 0))] + in_specs, out_specs=tuple(out_specs),
            compiler_params=_cparams("parallel"), name=name,
        )(x, g, shift, scale, *extra)
    pos, rc, gate, ys = pending
    return pl.pallas_call(
        kern,
        out_shape=(jax.ShapeDtypeStruct((t, d), F32),) + tuple(out_shape),
        grid_spec=pltpu.PrefetchScalarGridSpec(
            num_scalar_prefetch=1, grid=(t // bt,),
            in_specs=_combine_in_specs(tm, d, rc.shape[1], rows_per_mod) + in_specs,
            out_specs=(pl.BlockSpec((bt, d), lambda i, pos: (i, 0)),) + tuple(out_specs),
            scratch_shapes=_combine_scratch(tm, d)),
        compiler_params=_cparams("arbitrary"), name=name + "_combine",
    )(pos, x, rc, gate, ys, g, shift, scale, *extra)


def _inproj_even_body(h, w_ref, gqk_ref, cos_ref, sin_ref, gm_ref, q_ref, k_ref, v_ref, a_ref, *, rope):
    tm = h.shape[0]
    lane = lax.broadcasted_iota(jnp.int32, (tm, LANES), 1)
    first_head = lane < HEAD_DIM
    first_half = (lane % HEAD_DIM) < (HEAD_DIM // 2)
    c0 = QK_WIDTH + KV_WIDTH
    cw = (w_ref.shape[1] - c0) // 2
    p = jnp.dot(h, w_ref[...], preferred_element_type=F32)
    for m in range(QK_WIDTH // LANES):
        y = p[:, LANES * m:LANES * (m + 1)]
        sq = y * y
        sq_hi = sq.astype(BF16)
        sq_lo = (sq - sq_hi.astype(F32)).astype(BF16)
        ms = jnp.dot(jnp.concatenate([sq_hi, sq_lo], axis=1), gm_ref[...], preferred_element_type=F32)
        y = y * lax.rsqrt(ms + EPS) * gqk_ref[:, LANES * m:LANES * (m + 1)]
        if rope:
            partner = jnp.where(first_half, pltpu.roll(y, LANES - HEAD_DIM // 2, 1),
                                pltpu.roll(y, HEAD_DIM // 2, 1))
            y = y * cos_ref[...] + partner * sin_ref[...]
        if m < ATTN_WIDTH // LANES:
            q_ref[m] = jnp.where(first_head, y, 0.0).astype(BF16)
            q_ref[m + ATTN_WIDTH // LANES] = jnp.where(first_head, 0.0, y).astype(BF16)
        else:
            k_ref[...] = y.astype(BF16)
    v = p[:, QK_WIDTH:c0]
    v_ref[0] = jnp.where(first_head, v, 1.0).astype(BF16)
    v_ref[1] = jnp.where(first_head, 1.0, v).astype(BF16)
    a_ref[...] = (p[:, c0:c0 + cw] * _sigmoid(p[:, c0 + cw:])).astype(BF16)


def _inproj_even(x, pending, g, shift, scale, rows_per_mod, w, gqk, cos_t, sin_t, gmat, tm, rope):
    t, d = x.shape
    n = w.shape[1]
    cw = (n - QK_WIDTH - KV_WIDTH) // 2
    bt = tm if pending is None else 2 * tm
    n_pos = cos_t.shape[0] // bt
    return _inproj(
        functools.partial(_inproj_even_body, rope=rope), "inproj_even", x, pending, g, shift, scale,
        rows_per_mod, [w, gqk, cos_t, sin_t, gmat],
        [pl.BlockSpec((d, n), lambda i, *_: (0, 0)),
         pl.BlockSpec((1, QK_WIDTH), lambda i, *_: (0, 0)),
         pl.BlockSpec((bt, LANES), lambda i, *_: (i % n_pos, 0)),
         pl.BlockSpec((bt, LANES), lambda i, *_: (i % n_pos, 0)),
         pl.BlockSpec((2 * LANES, LANES), lambda i, *_: (0, 0))],
        [None, None, 0, 0, None],
        [jax.ShapeDtypeStruct((N_HEADS, t, LANES), BF16), jax.ShapeDtypeStruct((t, KV_WIDTH), BF16),
         jax.ShapeDtypeStruct((N_KV_HEADS, t, LANES), BF16), jax.ShapeDtypeStruct((t, cw), BF16)],
        [pl.BlockSpec((N_HEADS, bt, LANES), lambda i, *_: (0, i, 0)),
         pl.BlockSpec((bt, KV_WIDTH), lambda i, *_: (i, 0)),
         pl.BlockSpec((N_KV_HEADS, bt, LANES), lambda i, *_: (0, i, 0)),
         pl.BlockSpec((bt, cw), lambda i, *_: (i, 0))],
        [1, 0, 1, 0],
        tm)


def _inproj_odd_body(h, w_ref, gb_ref, u_ref):
    w = w_ref.shape[1] // 3
    proj = lambda lo: jnp.dot(h, w_ref[:, lo:lo + MXU_COLS], preferred_element_type=F32)
    for c in range(w // MXU_COLS):
        cols = slice(MXU_COLS * c, MXU_COLS * (c + 1))
        gb_ref[:, cols] = proj(MXU_COLS * c).astype(BF16)
        u_ref[:, cols] = (proj(w + MXU_COLS * c) * proj(2 * w + MXU_COLS * c)).astype(BF16)


def _inproj_odd(x, pending, g, shift, scale, rows_per_mod, w, tm):
    t, d = x.shape
    n = w.shape[1]
    bt = tm if pending is None else 2 * tm
    return _inproj(
        _inproj_odd_body, "inproj_odd", x, pending, g, shift, scale, rows_per_mod, [w],
        [pl.BlockSpec((d, n), lambda i, *_: (0, 0))], [None],
        [jax.ShapeDtypeStruct((t, n // 3), BF16), jax.ShapeDtypeStruct((t, n // 3), BF16)],
        [pl.BlockSpec((bt, n // 3), lambda i, *_: (i, 0)), pl.BlockSpec((bt, n // 3), lambda i, *_: (i, 0))],
        [0, 0],
        tm)


def _attn_kernel(q_ref, k_ref, v_ref, o_ref, acc_ref):
    tq = q_ref.shape[1]
    first = lax.broadcasted_iota(jnp.int32, (tq, LANES), 1) < HEAD_DIM

    def pair(m, carry):
        outs = []
        for g in range(N_KV_HEADS):
            q = q_ref[m + GQA_GROUP * g]
            s = lax.dot_general(q, k_ref[...], (((1,), (1,)), ((), ())), preferred_element_type=F32)
            p = jnp.exp2(s - jnp.max(s, axis=-1, keepdims=True)).astype(BF16)
            o = jnp.dot(p, v_ref[g], preferred_element_type=F32)
            l = o[:, HEAD_DIM:HEAD_DIM + 1] if g == 0 else o[:, 0:1]
            outs.append(o / l)
        acc_ref[m] = jnp.where(first, outs[0], outs[1])
        return carry

    lax.fori_loop(0, GQA_GROUP, pair, 0)
    o_ref[...] = jnp.concatenate([acc_ref[m] for m in range(GQA_GROUP)], axis=1).astype(BF16)


def _attention(q, k_all, v_all, s_q, tq):
    t = q.shape[1]
    b, skv, _ = k_all.shape
    nq = s_q // tq
    return pl.pallas_call(
        _attn_kernel,
        out_shape=jax.ShapeDtypeStruct((t, ATTN_WIDTH), BF16),
        grid=(b, nq),
        in_specs=[pl.BlockSpec((N_HEADS, tq, LANES), lambda bi, i: (0, bi * nq + i, 0)),
                  pl.BlockSpec((None, skv, LANES), lambda bi, i: (bi, 0, 0)),
                  pl.BlockSpec((N_KV_HEADS, None, skv, LANES), lambda bi, i: (0, bi, 0, 0))],
        out_specs=pl.BlockSpec((tq, ATTN_WIDTH), lambda bi, i: (bi * nq + i, 0)),
        scratch_shapes=[pltpu.VMEM((GQA_GROUP, tq, LANES), F32)],
        compiler_params=_cparams("parallel", "parallel"),
        name="attention",
    )(q, k_all, v_all)


def _fill_window(win_ref, cur_ref, prev_ref, next_ref, is_first, is_last):
    tm = cur_ref.shape[0]
    win_ref[0:HALO, :] = jnp.where(is_first, 0.0, prev_ref[...].astype(F32))
    win_ref[HALO:HALO + tm, :] = cur_ref[...].astype(F32)
    win_ref[HALO + tm:2 * HALO + tm, :] = jnp.where(is_last, 0.0, next_ref[...].astype(F32))


def _fill_shifted(win_ref, sh_ref):
    n = sh_ref.shape[1]
    for s in range(1, SUBLANES):
        sh_ref[s - 1] = win_ref[s:s + n, :]


def _dwconv_chunk(win_ref, sh_ref, w_ref, r0, rows, taps):
    off = HALO - taps // 2
    acc = None
    for kk in range(taps):
        q, s = divmod(off + kk, SUBLANES)
        if sh_ref is None:
            src = win_ref[r0 + off + kk:r0 + off + kk + rows, :]
        elif s == 0:
            src = win_ref[r0 + SUBLANES * q:r0 + SUBLANES * q + rows, :]
        else:
            src = sh_ref[s - 1, r0 + SUBLANES * q:r0 + SUBLANES * q + rows, :]
        term = src.reshape(rows // SUBLANES, SUBLANES, -1) * w_ref[SUBLANES * kk:SUBLANES * (kk + 1), :]
        acc = term if acc is None else acc + term
    return acc.reshape(rows, -1)


def _residual_tail(x_ref, o, gate_ref, g2_ref, sh2_ref, sc2_ref, xo_ref, hp_ref):
    x_new = x_ref[...] + gate_ref[...] * o
    xo_ref[...] = x_new
    hp_ref[...] = _pack_halves(_rms_mod(x_new, g2_ref[...], sh2_ref[...], sc2_ref[...]))


def _outproj_even_kernel(x_ref, at_ref, a_ref, ap_ref, an_ref, dw_ref, db_ref, lg_ref, lb_ref, w_ref,
                         gate_ref, g2_ref, sh2_ref, sc2_ref, xo_ref, hp_ref, win_ref, cv_ref, shw_ref,
                         *, tiles_per_seq):
    tm = x_ref.shape[0]
    i = pl.program_id(0) % tiles_per_seq
    _fill_window(win_ref, a_ref, ap_ref, an_ref, i == 0, i == tiles_per_seq - 1)
    _fill_shifted(win_ref, shw_ref)
    for r0 in range(0, tm, CONV_ROWS):
        y = _dwconv_chunk(win_ref, shw_ref, dw_ref, r0, CONV_ROWS, CONV_A_K) + db_ref[...]
        mu = jnp.mean(y, axis=-1, keepdims=True)
        yc = y - mu
        var = jnp.mean(yc * yc, axis=-1, keepdims=True)
        yn = yc * lax.rsqrt(var + EPS) * lg_ref[...] + lb_ref[...]
        cv_ref[r0:r0 + CONV_ROWS, :] = (yn * _sigmoid(yn)).astype(BF16)
    aw = at_ref.shape[1]
    o = (jnp.dot(at_ref[...], w_ref[0:aw, :], preferred_element_type=F32)
         + jnp.dot(cv_ref[...], w_ref[aw:, :], preferred_element_type=F32))
    _residual_tail(x_ref, o, gate_ref, g2_ref, sh2_ref, sc2_ref, xo_ref, hp_ref)


def _outproj_odd_kernel(x_ref, gb_ref, u_ref, up_ref, un_ref, cw_ref, w_ref,
                        gate_ref, g2_ref, sh2_ref, sc2_ref, xo_ref, hp_ref, win_ref, cv_ref,
                        *, tiles_per_seq):
    tm = x_ref.shape[0]
    i = pl.program_id(0) % tiles_per_seq
    _fill_window(win_ref, u_ref, up_ref, un_ref, i == 0, i == tiles_per_seq - 1)
    for r0 in range(0, tm, CONV_ROWS):
        y = _dwconv_chunk(win_ref, None, cw_ref, r0, CONV_ROWS, SC_K)
        cv_ref[r0:r0 + CONV_ROWS, :] = (gb_ref[r0:r0 + CONV_ROWS, :].astype(F32) * y).astype(BF16)
    o = jnp.dot(cv_ref[...], w_ref[...], preferred_element_type=F32)
    _residual_tail(x_ref, o, gate_ref, g2_ref, sh2_ref, sc2_ref, xo_ref, hp_ref)


def _halo_specs(tm, width, n_rows):
    hb = tm // HALO
    last = n_rows // HALO - 1
    return [pl.BlockSpec((tm, width), lambda i: (i, 0)),
            pl.BlockSpec((HALO, width), lambda i: (jnp.maximum(i * hb - 1, 0), 0)),
            pl.BlockSpec((HALO, width), lambda i: (jnp.minimum((i + 1) * hb, last), 0))]


def _outproj(kernel_fn, name, x, lead, conv_in, params, w, gate, g2, shift2, scale2, rows_per_mod,
             seq_len, tm, shifted_window):
    t, d = x.shape
    cw = conv_in.shape[1]
    scratch = [pltpu.VMEM((tm + 2 * HALO, cw), F32), pltpu.VMEM((tm, cw), BF16)]
    if shifted_window:
        scratch.append(pltpu.VMEM((SUBLANES - 1, tm + 2 * HALO - SUBLANES, cw), F32))
    mod_map = lambda i: ((i * tm) // rows_per_mod, 0, 0)
    row_spec = lambda a: pl.BlockSpec((tm, a.shape[1]), lambda i: (i, 0))
    full_spec = lambda a: pl.BlockSpec(a.shape, lambda i: (0,) * a.ndim)
    in_specs = ([row_spec(x)] + [row_spec(a) for a in lead] + _halo_specs(tm, cw, t)
                + [full_spec(a) for a in params] + [full_spec(w)]
                + [pl.BlockSpec((None, 1, d), mod_map), full_spec(g2),
                   pl.BlockSpec((None, 1, d), mod_map), pl.BlockSpec((None, 1, d), mod_map)])
    return pl.pallas_call(
        functools.partial(kernel_fn, tiles_per_seq=seq_len // tm),
        out_shape=(jax.ShapeDtypeStruct((t, d), F32), jax.ShapeDtypeStruct((t, d // 2), U32)),
        grid=(t // tm,),
        in_specs=in_specs,
        out_specs=(pl.BlockSpec((tm, d), lambda i: (i, 0)), pl.BlockSpec((tm, d // 2), lambda i: (i, 0))),
        scratch_shapes=scratch,
        compiler_params=_cparams("parallel"),
        name=name,
    )(x, *lead, conv_in, conv_in, conv_in, *params, w, gate, g2, shift2, scale2)


def _router_kernel(hp_ref, rw_ref, bias_ref, tri_ref, route_ref, cnt_ref, base_ref):
    tm = hp_ref.shape[0]

    @pl.when(pl.program_id(0) == 0)
    def _():
        base_ref[...] = jnp.zeros_like(base_ref)

    lo, hi = _unpack_halves(hp_ref[...])
    h = jnp.concatenate([lo, hi], axis=1).astype(BF16)
    logits = jnp.dot(h, rw_ref[...], preferred_element_type=F32)
    lt = logits.T[0:N_EXPERTS, :]
    aff = _sigmoid(lt)
    sel = aff + bias_ref[...]
    rows = lambda a, e: a[e:e + 1, :]

    best = None
    for g in range(N_GROUPS):
        a, b, c, d = (rows(sel, EXPERTS_PER_GROUP * g + j) for j in range(EXPERTS_PER_GROUP))
        score = jnp.maximum(jnp.maximum(a + b, c + d), jnp.maximum(a, b) + jnp.maximum(c, d))
        if best is None:
            best, grp = score, jnp.zeros((1, tm), jnp.int32)
        else:
            better = score > best
            grp = jnp.where(better, g, grp)
            best = jnp.where(better, score, best)

    def in_group(a, j):
        out = rows(a, j)
        for g in range(1, N_GROUPS):
            out = jnp.where(grp == g, rows(a, EXPERTS_PER_GROUP * g + j), out)
        return out

    s = [in_group(sel, j) for j in range(EXPERTS_PER_GROUP)]
    f = [in_group(aff, j) for j in range(EXPERTS_PER_GROUP)]

    def first_argmax(vals):
        m = functools.reduce(jnp.maximum, vals)
        idx = jnp.full((1, tm), len(vals) - 1, jnp.int32)
        for j in range(len(vals) - 2, -1, -1):
            idx = jnp.where(vals[j] == m, j, idx)
        return idx

    i1 = first_argmax(s)
    i2 = first_argmax([jnp.where(i1 == j, -jnp.inf, s[j]) for j in range(EXPERTS_PER_GROUP)])
    e_lo = jnp.minimum(i1, i2)
    e_hi = jnp.maximum(i1, i2)
    pick = lambda vals, idx: functools.reduce(
        lambda acc, j: jnp.where(idx == j, vals[j], acc), range(1, len(vals)), vals[0])
    f_lo, f_hi = pick(f, e_lo), pick(f, e_hi)
    denom = f_lo + f_hi
    pair = jnp.where(e_lo == 0, e_hi - 1, jnp.where(e_lo == 1, e_hi + 1, N_PAIRS - 1))
    cls = grp * N_PAIRS + pair

    onehot = (lax.broadcasted_iota(jnp.int32, (CLASS_ROWS, tm), 0) == cls)
    onehot_f = onehot.astype(F32)
    before = jnp.dot(onehot_f.astype(BF16), tri_ref[...], preferred_element_type=F32) + base_ref[:, 0:1]
    rank = jnp.sum(onehot_f * before, axis=0, keepdims=True)
    base_ref[...] = base_ref[...] + jnp.sum(onehot_f, axis=1, keepdims=True)

    route_ref[...] = jnp.concatenate(
        [cls.astype(F32), rank, f_lo / denom, f_hi / denom, jnp.zeros((4, tm), F32)], axis=0)
    cnt_ref[...] = base_ref[...]


def _router(hp, rw, bias, tri, tm):
    t, hw = hp.shape
    return pl.pallas_call(
        _router_kernel,
        out_shape=(jax.ShapeDtypeStruct((8, t), F32), jax.ShapeDtypeStruct((CLASS_ROWS, LANES), F32)),
        grid=(t // tm,),
        in_specs=[pl.BlockSpec((tm, hw), lambda i: (i, 0)),
                  pl.BlockSpec(rw.shape, lambda i: (0, 0)),
                  pl.BlockSpec(bias.shape, lambda i: (0, 0)),
                  pl.BlockSpec((tm, tm), lambda i: (0, 0))],
        out_specs=(pl.BlockSpec((8, tm), lambda i: (0, i)),
                   pl.BlockSpec((CLASS_ROWS, LANES), lambda i: (0, 0))),
        scratch_shapes=[pltpu.VMEM((CLASS_ROWS, LANES), F32)],
        compiler_params=_cparams("arbitrary"),
        name="router",
    )(hp, rw, bias, tri)


def _dispatch_kernel(pos_ref, hp_ref, zero_ref, hs_ref, sem):
    del zero_ref
    tm = hp_ref.shape[0]
    base = pl.program_id(0) * tm

    for r in range(tm):
        pltpu.async_copy(hp_ref.at[pl.ds(r, 1)], hs_ref.at[pl.ds(pos_ref[base + r], 1)], sem,
                         priority=r % DMA_QUEUES)
    pltpu.make_async_copy(hp_ref, hs_ref.at[pl.ds(0, tm)], sem).wait()


def _dispatch(pos, hp, n_rows, tm):
    t, hw = hp.shape
    zeros = jnp.zeros((n_rows, hw), U32)
    return pl.pallas_call(
        _dispatch_kernel,
        out_shape=jax.ShapeDtypeStruct((n_rows, hw), U32),
        grid_spec=pltpu.PrefetchScalarGridSpec(
            num_scalar_prefetch=1,
            grid=(t // tm,),
            in_specs=[pl.BlockSpec((tm, hw), lambda i, pos: (i, 0)),
                      pl.BlockSpec(memory_space=pl.ANY)],
            out_specs=pl.BlockSpec(memory_space=pl.ANY),
            scratch_shapes=[pltpu.SemaphoreType.DMA(())]),
        input_output_aliases={2: 0},
        compiler_params=_cparams("arbitrary"),
        name="dispatch",
    )(pos, hp, zeros)


def _moe_kernel(elo_ref, ehi_ref, nused_ref, hs_ref, wgu_lo, wd_lo, wgu_hi, wd_hi, ys_ref):
    del elo_ref, ehi_ref

    @pl.when(pl.program_id(0) >= nused_ref[0])
    def _():
        ys_ref[...] = jnp.zeros_like(ys_ref)

    @pl.when(pl.program_id(0) < nused_ref[0])
    def _():
        lo, hi = _unpack_halves(hs_ref[...])
        h = jnp.concatenate([lo, hi], axis=1).astype(BF16)

        def expert(wgu_ref, wd_ref):
            gu = jnp.dot(h, wgu_ref[...], preferred_element_type=F32)
            g, u = gu[:, :D_EXPERT], gu[:, D_EXPERT:]
            act = (g * _sigmoid(g) * u).astype(BF16)
            return jnp.dot(act, wd_ref[...], preferred_element_type=F32)

        ya = lax.bitcast_convert_type(expert(wgu_lo, wd_lo).astype(BF16).astype(F32), U32)
        yb = lax.bitcast_convert_type(expert(wgu_hi, wd_hi).astype(BF16).astype(F32), U32)
        ys_ref[...] = (ya >> 16) | (yb & jnp.uint32(0xFFFF0000))


def _moe(e_lo, e_hi, n_used, hs, wgu, wd):
    n_rows, hw = hs.shape
    d = wd.shape[2]
    n_blocks = n_rows // MOE_BLOCK
    row_map = lambda j, lo, hi, nu: (jnp.minimum(j, nu[0] - 1), 0)
    return pl.pallas_call(
        _moe_kernel,
        out_shape=jax.ShapeDtypeStruct((n_rows, d), U32),
        grid_spec=pltpu.PrefetchScalarGridSpec(
            num_scalar_prefetch=3,
            grid=(n_blocks,),
            in_specs=[pl.BlockSpec((MOE_BLOCK, hw), row_map),
                      pl.BlockSpec((None,) + wgu.shape[1:], lambda j, lo, hi, nu: (lo[j], 0, 0)),
                      pl.BlockSpec((None,) + wd.shape[1:], lambda j, lo, hi, nu: (lo[j], 0, 0)),
                      pl.BlockSpec((None,) + wgu.shape[1:], lambda j, lo, hi, nu: (hi[j], 0, 0)),
                      pl.BlockSpec((None,) + wd.shape[1:], lambda j, lo, hi, nu: (hi[j], 0, 0))],
            out_specs=pl.BlockSpec((MOE_BLOCK, d), lambda j, lo, hi, nu: (j, 0))),
        compiler_params=_cparams("arbitrary"),
        name="experts",
    )(e_lo, e_hi, n_used, hs, wgu, wd, wgu, wd)


def _combine_kernel(pos_ref, x_ref, rc_ref, gate_ref, ys_ref, xo_ref, buf_a, buf_b, sems):
    tm = buf_a.shape[0]

    def consume(half, x_new):
        xo_ref[half * tm:(half + 1) * tm, :] = x_new

    _combine_halves(pos_ref, x_ref, rc_ref, gate_ref, ys_ref, (buf_a, buf_b), sems, consume)


def _combine(pos, x, rc, gate, ys, rows_per_mod, tm):
    t, d = x.shape
    return pl.pallas_call(
        _combine_kernel,
        out_shape=jax.ShapeDtypeStruct((t, d), F32),
        grid_spec=pltpu.PrefetchScalarGridSpec(
            num_scalar_prefetch=1,
            grid=(t // (2 * tm),),
            in_specs=_combine_in_specs(tm, d, rc.shape[1], rows_per_mod),
            out_specs=pl.BlockSpec((2 * tm, d), lambda i, pos: (i, 0)),
            scratch_shapes=_combine_scratch(tm, d)),
        compiler_params=_cparams("arbitrary"),
        name="combine",
    )(pos, x, rc, gate, ys)


def _deinterleave(base):
    return [base + 2 * p for p in range(HEAD_DIM // 2)] + [base + 2 * p + 1 for p in range(HEAD_DIM // 2)]


def _even_in_perm(n_cols):
    cols = []
    half = N_HEADS // N_KV_HEADS
    for m in range(half):
        cols += _deinterleave(HEAD_DIM * m) + _deinterleave(HEAD_DIM * (m + half))
    for kh in range(N_KV_HEADS):
        cols += _deinterleave(ATTN_WIDTH + HEAD_DIM * kh)
    cols += list(range(QK_WIDTH, n_cols))
    return np.asarray(cols, np.int32)


def _even_out_perm(n_rows):
    rows = []
    half = N_HEADS // N_KV_HEADS
    for m in range(half):
        rows += list(range(HEAD_DIM * m, HEAD_DIM * (m + 1)))
        rows += list(range(HEAD_DIM * (m + half), HEAD_DIM * (m + half + 1)))
    rows += list(range(ATTN_WIDTH, n_rows))
    return np.asarray(rows, np.int32)


def _rope_tables(n_tokens):
    rows = n_tokens // GRID_W
    row = jnp.repeat(jnp.arange(rows, dtype=F32), GRID_W)
    col = jnp.tile(jnp.arange(GRID_W, dtype=F32), rows)
    inv = ROPE_THETA ** (-jnp.arange(ROPE_PAIRS_PER_AXIS, dtype=F32) / ROPE_PAIRS_PER_AXIS)
    ang = jnp.concatenate([row[:, None] * inv, col[:, None] * inv], axis=-1)
    cos, sin = jnp.cos(ang), jnp.sin(ang)
    reps = LANES // HEAD_DIM
    return (jnp.tile(jnp.concatenate([cos, cos], axis=-1), (1, reps)),
            jnp.tile(jnp.concatenate([-sin, sin], axis=-1), (1, reps)))


def _head_mean_matrix():
    blk = np.kron(np.eye(LANES // HEAD_DIM), np.full((HEAD_DIM, HEAD_DIM), 1.0 / HEAD_DIM))
    return jnp.asarray(np.concatenate([blk, blk], axis=0), BF16)


def _moe_layer(tokens_hp, router_wp, router_b, tri, wgu, wd, tm):
    t = tokens_hp.shape[0]
    route, counts = _router(tokens_hp, router_wp, router_b, tri, tm)
    cls = route[0].astype(jnp.int32)
    rank = route[1].astype(jnp.int32)
    cnt = counts[:N_CLASSES, 0].astype(jnp.int32)
    padded = (cnt + MOE_BLOCK - 1) // MOE_BLOCK * MOE_BLOCK
    pad_end = jnp.cumsum(padded)
    pos = (pad_end - padded)[cls] + rank
    n_blocks = -(-t // MOE_BLOCK) + N_CLASSES
    n_used = pad_end[-1] // MOE_BLOCK
    blk = jnp.minimum(jnp.arange(n_blocks, dtype=jnp.int32), n_used - 1)
    bcls = jnp.sum((pad_end[None, :] <= (blk * MOE_BLOCK)[:, None]).astype(jnp.int32), axis=1)
    bcls = jnp.minimum(bcls, N_CLASSES - 1)
    e_lo = EXPERTS_PER_GROUP * (bcls // N_PAIRS) + jnp.asarray(PAIR_LO, jnp.int32)[bcls % N_PAIRS]
    e_hi = EXPERTS_PER_GROUP * (bcls // N_PAIRS) + jnp.asarray(PAIR_HI, jnp.int32)[bcls % N_PAIRS]
    hs = _dispatch(pos, tokens_hp, n_blocks * MOE_BLOCK, math.gcd(tm, ROW_TILE))
    ys = _moe(e_lo, e_hi, n_used.reshape(1).astype(jnp.int32), hs, wgu, wd)
    return pos, route.T, ys


def kernel(x, c, ctx, c_ctx, w_mod, b_mod, norm_mix_g, norm_ffn_g, ev_w_in, ev_q_norm_g, ev_k_norm_g,
           ev_dw_w, ev_dw_b, ev_ln_g, ev_ln_b, ev_w_out, od_w_in, od_conv_w, od_w_out, router_w,
           router_bias, moe_w_gate, moe_w_up, moe_w_down):
    b, s, d = x.shape
    n_ctx = ctx.shape[1]
    depth = w_mod.shape[0]
    last_attn = (depth - 1) - ((depth - 1) % 2)
    tm = min(ROW_TILE, s)
    tmz = min(ROW_TILE, n_ctx)
    tq = min(Q_TILE, s)
    tqz = min(Q_TILE, n_ctx)
    assert s % tm == 0 and n_ctx % tmz == 0 and s % tq == 0 and n_ctx % tqz == 0 and s % GRID_W == 0

    r = -(-(b + 1) // 8) * 8
    c_all = jnp.zeros((r, d), F32).at[:b].set(c).at[b].set(c_ctx)
    mods = _modulation(c_all, w_mod, b_mod)

    def mod_x(l, j):
        return mods[l, :b, j * d:(j + 1) * d].reshape(b, 1, d)

    def mod_z(l, j):
        return mods[l, b:b + 1, j * d:(j + 1) * d].reshape(1, 1, d)

    cos_t, sin_t = _rope_tables(s)
    gmat = _head_mean_matrix()
    in_perm = _even_in_perm(ev_w_in.shape[2])
    out_perm = _even_out_perm(ev_w_out.shape[1])
    head_perm = np.asarray(_deinterleave(0), np.int32)
    router_wp = jnp.zeros((d, LANES), F32).at[:, :N_EXPERTS].set(router_w).astype(BF16)
    router_b = router_bias.astype(F32).reshape(N_EXPERTS, 1)
    tri_n = min(ROUTER_TILE, b * s)
    tri = jnp.asarray(np.triu(np.ones((tri_n, tri_n), np.float32), 1), BF16)

    xf = x.reshape(b * s, d)
    zf = ctx.reshape(b * n_ctx, d)
    pending = None
    ctm = min(Q_TILE, s, n_ctx)
    for l in range(depth):
        i = l // 2
        ctx_in = l <= last_attn
        ctx_out = l < last_attn
        gmix = norm_mix_g[l].reshape(1, d)
        gffn = norm_ffn_g[l].reshape(1, d)
        if l % 2 == 0:
            w_in = ev_w_in[i][:, in_perm].astype(BF16)
            w_out = ev_w_out[i][out_perm, :].astype(BF16)
            gq = ev_q_norm_g[i][head_perm] * (ATTN_SCALE * LOG2E)
            gk = ev_k_norm_g[i][head_perm]
            gqk = jnp.concatenate([jnp.tile(gq, N_HEADS), jnp.tile(gk, N_KV_HEADS)]).reshape(1, QK_WIDTH)
            conv_params = [jnp.repeat(ev_dw_w[i], SUBLANES, axis=0), ev_dw_b[i].reshape(1, -1),
                           ev_ln_g[i].reshape(1, -1),
                           ev_ln_b[i].reshape(1, -1)]
            res = _inproj_even(xf, pending, gmix, mod_x(l, 0), mod_x(l, 1), s, w_in, gqk, cos_t, sin_t, gmat,
                               tm, True)
            if pending is not None:
                xf, res = res[0], res[1:]
            q, k, v, a = res
            k_all = k.reshape(b, s, KV_WIDTH)
            v_all = v.reshape(N_KV_HEADS, b, s, LANES)
            if ctx_in:
                qz, kz, vz, az = _inproj_even(zf, None, gmix, mod_z(l, 0), mod_z(l, 1), b * n_ctx, w_in, gqk,
                                              cos_t, sin_t, gmat, tmz, False)
                kz = kz.reshape(b, n_ctx, KV_WIDTH)
                vz = vz.reshape(N_KV_HEADS, b, n_ctx, LANES)
                k_all = jnp.concatenate([k_all, kz], axis=1)
                v_all = jnp.concatenate([v_all, vz], axis=2)
            attn = _attention(q, k_all, v_all, s, tq)
            xf, hp = _outproj(_outproj_even_kernel, "outproj_even", xf, [attn], a, conv_params, w_out,
                              mod_x(l, 2), gffn, mod_x(l, 3), mod_x(l, 4), s, s, tm, True)
            if ctx_out:
                attn_z = _attention(qz, kz, vz, n_ctx, tqz)
                zf, hpz = _outproj(_outproj_even_kernel, "outproj_even_ctx", zf, [attn_z], az, conv_params,
                                   w_out, mod_z(l, 2), gffn, mod_z(l, 3), mod_z(l, 4), b * n_ctx, n_ctx, tmz,
                                   True)
        else:
            w_in = od_w_in[i].astype(BF16)
            w_out = od_w_out[i].astype(BF16)
            conv_params = [jnp.repeat(od_conv_w[i], SUBLANES, axis=0)]
            res = _inproj_odd(xf, pending, gmix, mod_x(l, 0), mod_x(l, 1), s, w_in, tm)
            if pending is not None:
                xf, res = res[0], res[1:]
            gb, u = res
            xf, hp = _outproj(_outproj_odd_kernel, "outproj_odd", xf, [gb], u, conv_params, w_out,
                              mod_x(l, 2), gffn, mod_x(l, 3), mod_x(l, 4), s, s, tm, False)
            if ctx_out:
                gbz, uz = _inproj_odd(zf, None, gmix, mod_z(l, 0), mod_z(l, 1), b * n_ctx, w_in, tmz)
                zf, hpz = _outproj(_outproj_odd_kernel, "outproj_odd_ctx", zf, [gbz], uz, conv_params, w_out,
                                   mod_z(l, 2), gffn, mod_z(l, 3), mod_z(l, 4), b * n_ctx, n_ctx, tmz, False)

        wgu = jnp.concatenate([moe_w_gate[l], moe_w_up[l]], axis=-1).astype(BF16)
        wd = moe_w_down[l].astype(BF16)
        tokens = jnp.concatenate([hp, hpz], axis=0) if ctx_out else hp
        rtm = math.gcd(tri_n, math.gcd(b * s, b * n_ctx)) if ctx_out else tri_n
        pos, rc, ys = _moe_layer(tokens, router_wp, router_b, tri[:rtm, :rtm], wgu, wd, rtm)
        n_lat = b * s
        pending = (pos[:n_lat], rc[:n_lat], mod_x(l, 5), ys)
        if ctx_out:
            zf = _combine(pos[n_lat:], zf, rc[n_lat:], mod_z(l, 5), ys, b * n_ctx, ctm)
    pos, rc, gate, ys = pending
    return _combine(pos, xf, rc, gate, ys, s, ctm).reshape(b, s, d)
```

```python
import functools
import math

import numpy as np
import jax
import jax.numpy as jnp
from jax import lax
from jax.experimental import pallas as pl
from jax.experimental.pallas import tpu as pltpu

F32 = jnp.float32
BF16 = jnp.bfloat16
U32 = jnp.uint32

GRID_W = 64
EPS = 1e-6
N_HEADS = 8
N_KV_HEADS = 2
GQA_GROUP = N_HEADS // N_KV_HEADS
HEAD_DIM = 64
ATTN_WIDTH = N_HEADS * HEAD_DIM
KV_WIDTH = N_KV_HEADS * HEAD_DIM
QK_WIDTH = ATTN_WIDTH + KV_WIDTH
ATTN_SCALE = HEAD_DIM ** -0.5
ROPE_THETA = 10000.0
ROPE_PAIRS_PER_AXIS = HEAD_DIM // 4
CONV_A_K = 31
SC_K = 3
N_EXPERTS = 16
N_GROUPS = 4
EXPERTS_PER_GROUP = N_EXPERTS // N_GROUPS
D_EXPERT = 512
PAIR_LO = (0, 0, 0, 1, 1, 2)
PAIR_HI = (1, 2, 3, 2, 3, 3)
N_PAIRS = len(PAIR_LO)
N_CLASSES = N_GROUPS * N_PAIRS
CLASS_ROWS = 32

LANES = 128
SUBLANES = 8
MXU_COLS = 256
HALO = 16
VMEM_LIMIT = 48 * 1024 * 1024
ROW_TILE = 512
Q_TILE = 256
ROUTER_TILE = 1024
DMA_QUEUES = 2
MOE_BLOCK = 512
CONV_ROWS = 64
LOG2E = math.log2(math.e)


def _cparams(*sem):
    return pltpu.CompilerParams(dimension_semantics=sem, vmem_limit_bytes=VMEM_LIMIT)


def _sigmoid(x):
    return 1.0 / (1.0 + jnp.exp(-x))


def _rms_mod(x, g, shift, scale):
    y = x * lax.rsqrt(jnp.mean(x * x, axis=-1, keepdims=True) + EPS) * g
    return y * (1.0 + scale) + shift


def _pack_halves(h):
    w = h.shape[1] // 2
    lo = lax.bitcast_convert_type(h[:, :w].astype(BF16).astype(F32), U32)
    hi = lax.bitcast_convert_type(h[:, w:].astype(BF16).astype(F32), U32)
    return (lo >> 16) | (hi & jnp.uint32(0xFFFF0000))


def _unpack_halves(w):
    lo = lax.bitcast_convert_type(w << 16, F32)
    hi = lax.bitcast_convert_type(w & jnp.uint32(0xFFFF0000), F32)
    return lo, hi


def _issue_row_gathers(pos_ref, ys_ref, ybuf, sems, tile, slot, rows):
    tm = ybuf.shape[1]
    for r in rows:
        pltpu.async_copy(ys_ref.at[pl.ds(pos_ref[tile * tm + r], 1)], ybuf.at[slot, pl.ds(r, 1)],
                         sems.at[slot], priority=r % DMA_QUEUES)


def _wait_row_gathers(ys_ref, ybuf, sems, slot):
    pltpu.make_async_copy(ys_ref.at[pl.ds(0, ybuf.shape[1])], ybuf.at[slot], sems.at[slot]).wait()


def _combined_residual(pos_ref, x_ref, rc_ref, gate_ref, ys_ref, ybuf, sems):
    tm = ybuf.shape[1]
    i = pl.program_id(0)
    n = pl.num_programs(0)

    @pl.when(i == 0)
    def _():
        _issue_row_gathers(pos_ref, ys_ref, ybuf, sems, 0, 0, range(tm))

    _wait_row_gathers(ys_ref, ybuf, sems, i % 2)
    ya, yb = _unpack_halves(ybuf[i % 2])
    rc = rc_ref[...]
    x_new = x_ref[...] + gate_ref[...] * (rc[:, 2:3] * ya + rc[:, 3:4] * yb)

    def issue(k, n_slices):
        _issue_row_gathers(pos_ref, ys_ref, ybuf, sems, jnp.minimum(i + 1, n - 1), (i + 1) % 2,
                           range(k * tm // n_slices, (k + 1) * tm // n_slices))
        if k == n_slices - 1:
            @pl.when(i == n - 1)
            def _():
                _wait_row_gathers(ys_ref, ybuf, sems, (i + 1) % 2)

    return x_new, issue


def _combine_in_specs(tm, d, rc_width, rows_per_mod):
    return [pl.BlockSpec((tm, d), lambda i, pos: (i, 0)),
            pl.BlockSpec((tm, rc_width), lambda i, pos: (i, 0)),
            pl.BlockSpec((None, 1, d), lambda i, pos: ((i * tm) // rows_per_mod, 0, 0)),
            pl.BlockSpec(memory_space=pl.ANY)]


def _combine_scratch(tm, d):
    return [pltpu.VMEM((2, tm, d), U32), pltpu.SemaphoreType.DMA((2,))]


def _mod_kernel(c_ref, w_ref, b_ref, o_ref):
    c = c_ref[...]
    a = (c * _sigmoid(c)).astype(BF16)
    o_ref[...] = jnp.dot(a, w_ref[...].astype(BF16), preferred_element_type=F32) + b_ref[...]


def _modulation(c_all, w_mod, b_mod):
    depth, d, n = w_mod.shape
    r = c_all.shape[0]
    nt = n // d
    return pl.pallas_call(
        _mod_kernel,
        out_shape=jax.ShapeDtypeStruct((depth, r, n), F32),
        grid=(depth, nt),
        in_specs=[pl.BlockSpec((r, d), lambda l, j: (0, 0)),
                  pl.BlockSpec((None, d, d), lambda l, j: (l, 0, j)),
                  pl.BlockSpec((None, 1, d), lambda l, j: (l, 0, j))],
        out_specs=pl.BlockSpec((None, r, d), lambda l, j: (l, 0, j)),
        compiler_params=_cparams("parallel", "parallel"),
        name="modulation",
    )(c_all, w_mod, b_mod.reshape(depth, 1, n))


def _inproj_kernel(*refs, body, n_in, combine):
    if combine:
        pos_ref, x_ref, rc_ref, gate_ref, ys_ref = refs[:5]
        ybuf, sems = refs[-2:]
        rest = refs[5:-2]
        x, issue = _combined_residual(pos_ref, x_ref, rc_ref, gate_ref, ys_ref, ybuf, sems)
        rest[3 + n_in][...] = x
        ins, outs = rest[:3 + n_in], rest[4 + n_in:]
    else:
        x = refs[0][...]
        issue = lambda k, n_slices: None
        ins, outs = refs[1:4 + n_in], refs[4 + n_in:]
    h = _rms_mod(x, ins[0][...], ins[1][...], ins[2][...]).astype(BF16)
    body(h, issue, *ins[3:], *outs)


def _inproj(body, name, x, pending, g, shift, scale, rows_per_mod, extra, extra_specs, out_shape, out_specs,
            tm):
    t, d = x.shape
    mod_map = lambda i, *_: ((i * tm) // rows_per_mod, 0, 0)
    in_specs = [pl.BlockSpec((1, d), lambda i, *_: (0, 0)), pl.BlockSpec((None, 1, d), mod_map),
                pl.BlockSpec((None, 1, d), mod_map)] + list(extra_specs)
    kern = functools.partial(_inproj_kernel, body=body, n_in=len(extra), combine=pending is not None)
    if pending is None:
        return pl.pallas_call(
            kern, out_shape=tuple(out_shape), grid=(t // tm,),
            in_specs=[pl.BlockSpec((tm, d), lambda i: (i, 0))] + in_specs, out_specs=tuple(out_specs),
            compiler_params=_cparams("parallel"), name=name,
        )(x, g, shift, scale, *extra)
    pos, rc, gate, ys = pending
    return pl.pallas_call(
        kern,
        out_shape=(jax.ShapeDtypeStruct((t, d), F32),) + tuple(out_shape),
        grid_spec=pltpu.PrefetchScalarGridSpec(
            num_scalar_prefetch=1, grid=(t // tm,),
            in_specs=_combine_in_specs(tm, d, rc.shape[1], rows_per_mod) + in_specs,
            out_specs=(pl.BlockSpec((tm, d), lambda i, pos: (i, 0)),) + tuple(out_specs),
            scratch_shapes=_combine_scratch(tm, d)),
        compiler_params=_cparams("arbitrary"), name=name + "_combine",
    )(pos, x, rc, gate, ys, g, shift, scale, *extra)


def _inproj_even_body(h, issue, w_ref, gqk_ref, cos_ref, sin_ref, gm_ref, q_ref, k_ref, v_ref, a_ref, *, rope):
    tm = h.shape[0]
    lane = lax.broadcasted_iota(jnp.int32, (tm, LANES), 1)
    first_head = lane < HEAD_DIM
    first_half = (lane % HEAD_DIM) < (HEAD_DIM // 2)
    c0 = QK_WIDTH + KV_WIDTH
    cw = (w_ref.shape[1] - c0) // 2
    issue(0, 1)
    p = jnp.dot(h, w_ref[...], preferred_element_type=F32)
    for m in range(QK_WIDTH // LANES):
        y = p[:, LANES * m:LANES * (m + 1)]
        sq = y * y
        sq_hi = sq.astype(BF16)
        sq_lo = (sq - sq_hi.astype(F32)).astype(BF16)
        ms = jnp.dot(jnp.concatenate([sq_hi, sq_lo], axis=1), gm_ref[...], preferred_element_type=F32)
        y = y * lax.rsqrt(ms + EPS) * gqk_ref[:, LANES * m:LANES * (m + 1)]
        if rope:
            partner = jnp.where(first_half, pltpu.roll(y, LANES - HEAD_DIM // 2, 1),
                                pltpu.roll(y, HEAD_DIM // 2, 1))
            y = y * cos_ref[...] + partner * sin_ref[...]
        if m < ATTN_WIDTH // LANES:
            q_ref[m] = jnp.where(first_head, y, 0.0).astype(BF16)
            q_ref[m + ATTN_WIDTH // LANES] = jnp.where(first_head, 0.0, y).astype(BF16)
        else:
            k_ref[...] = y.astype(BF16)
    v = p[:, QK_WIDTH:c0]
    v_ref[0] = jnp.where(first_head, v, 1.0).astype(BF16)
    v_ref[1] = jnp.where(first_head, 1.0, v).astype(BF16)
    a_ref[...] = (p[:, c0:c0 + cw] * _sigmoid(p[:, c0 + cw:])).astype(BF16)


def _inproj_even(x, pending, g, shift, scale, rows_per_mod, w, gqk, cos_t, sin_t, gmat, tm, rope):
    t, d = x.shape
    n = w.shape[1]
    cw = (n - QK_WIDTH - KV_WIDTH) // 2
    n_pos = cos_t.shape[0] // tm
    return _inproj(
        functools.partial(_inproj_even_body, rope=rope), "inproj_even", x, pending, g, shift, scale,
        rows_per_mod, [w, gqk, cos_t, sin_t, gmat],
        [pl.BlockSpec((d, n), lambda i, *_: (0, 0)),
         pl.BlockSpec((1, QK_WIDTH), lambda i, *_: (0, 0)),
         pl.BlockSpec((tm, LANES), lambda i, *_: (i % n_pos, 0)),
         pl.BlockSpec((tm, LANES), lambda i, *_: (i % n_pos, 0)),
         pl.BlockSpec((2 * LANES, LANES), lambda i, *_: (0, 0))],
        [jax.ShapeDtypeStruct((N_HEADS, t, LANES), BF16), jax.ShapeDtypeStruct((t, KV_WIDTH), BF16),
         jax.ShapeDtypeStruct((N_KV_HEADS, t, LANES), BF16), jax.ShapeDtypeStruct((t, cw), BF16)],
        [pl.BlockSpec((N_HEADS, tm, LANES), lambda i, *_: (0, i, 0)),
         pl.BlockSpec((tm, KV_WIDTH), lambda i, *_: (i, 0)),
         pl.BlockSpec((N_KV_HEADS, tm, LANES), lambda i, *_: (0, i, 0)),
         pl.BlockSpec((tm, cw), lambda i, *_: (i, 0))],
        tm)


def _inproj_odd_body(h, issue, w_ref, gb_ref, u_ref):
    w = w_ref.shape[1] // 3
    n_phase = w // MXU_COLS
    proj = lambda lo: jnp.dot(h, w_ref[:, lo:lo + MXU_COLS], preferred_element_type=F32)
    for c in range(n_phase):
        issue(c, n_phase)
        cols = slice(MXU_COLS * c, MXU_COLS * (c + 1))
        gb_ref[:, cols] = proj(MXU_COLS * c).astype(BF16)
        u_ref[:, cols] = (proj(w + MXU_COLS * c) * proj(2 * w + MXU_COLS * c)).astype(BF16)


def _inproj_odd(x, pending, g, shift, scale, rows_per_mod, w, tm):
    t, d = x.shape
    n = w.shape[1]
    return _inproj(
        _inproj_odd_body, "inproj_odd", x, pending, g, shift, scale, rows_per_mod, [w],
        [pl.BlockSpec((d, n), lambda i, *_: (0, 0))],
        [jax.ShapeDtypeStruct((t, n // 3), BF16), jax.ShapeDtypeStruct((t, n // 3), BF16)],
        [pl.BlockSpec((tm, n // 3), lambda i, *_: (i, 0)), pl.BlockSpec((tm, n // 3), lambda i, *_: (i, 0))],
        tm)


def _attn_kernel(q_ref, k_ref, v_ref, o_ref, acc_ref):
    tq = q_ref.shape[1]
    first = lax.broadcasted_iota(jnp.int32, (tq, LANES), 1) < HEAD_DIM

    def pair(m, carry):
        outs = []
        for g in range(N_KV_HEADS):
            q = q_ref[m + GQA_GROUP * g]
            s = lax.dot_general(q, k_ref[...], (((1,), (1,)), ((), ())), preferred_element_type=F32)
            p = jnp.exp2(s - jnp.max(s, axis=-1, keepdims=True)).astype(BF16)
            o = jnp.dot(p, v_ref[g], preferred_element_type=F32)
            l = o[:, HEAD_DIM:HEAD_DIM + 1] if g == 0 else o[:, 0:1]
            outs.append(o / l)
        acc_ref[m] = jnp.where(first, outs[0], outs[1])
        return carry

    lax.fori_loop(0, GQA_GROUP, pair, 0)
    o_ref[...] = jnp.concatenate([acc_ref[m] for m in range(GQA_GROUP)], axis=1).astype(BF16)


def _attention(q, k_all, v_all, s_q, tq):
    t = q.shape[1]
    b, skv, _ = k_all.shape
    nq = s_q // tq
    return pl.pallas_call(
        _attn_kernel,
        out_shape=jax.ShapeDtypeStruct((t, ATTN_WIDTH), BF16),
        grid=(b, nq),
        in_specs=[pl.BlockSpec((N_HEADS, tq, LANES), lambda bi, i: (0, bi * nq + i, 0)),
                  pl.BlockSpec((None, skv, LANES), lambda bi, i: (bi, 0, 0)),
                  pl.BlockSpec((N_KV_HEADS, None, skv, LANES), lambda bi, i: (0, bi, 0, 0))],
        out_specs=pl.BlockSpec((tq, ATTN_WIDTH), lambda bi, i: (bi * nq + i, 0)),
        scratch_shapes=[pltpu.VMEM((GQA_GROUP, tq, LANES), F32)],
        compiler_params=_cparams("parallel", "parallel"),
        name="attention",
    )(q, k_all, v_all)


def _fill_window(win_ref, cur_ref, prev_ref, next_ref, is_first, is_last):
    tm = cur_ref.shape[0]
    win_ref[0:HALO, :] = jnp.where(is_first, 0.0, prev_ref[...].astype(F32))
    win_ref[HALO:HALO + tm, :] = cur_ref[...].astype(F32)
    win_ref[HALO + tm:2 * HALO + tm, :] = jnp.where(is_last, 0.0, next_ref[...].astype(F32))


def _fill_shifted(win_ref, sh_ref):
    n = sh_ref.shape[1]
    for s in range(1, SUBLANES):
        sh_ref[s - 1] = win_ref[s:s + n, :]


def _dwconv_chunk(win_ref, sh_ref, w_ref, r0, rows, taps):
    off = HALO - taps // 2
    acc = None
    for kk in range(taps):
        q, s = divmod(off + kk, SUBLANES)
        if sh_ref is None:
            src = win_ref[r0 + off + kk:r0 + off + kk + rows, :]
        elif s == 0:
            src = win_ref[r0 + SUBLANES * q:r0 + SUBLANES * q + rows, :]
        else:
            src = sh_ref[s - 1, r0 + SUBLANES * q:r0 + SUBLANES * q + rows, :]
        term = src.reshape(rows // SUBLANES, SUBLANES, -1) * w_ref[SUBLANES * kk:SUBLANES * (kk + 1), :]
        acc = term if acc is None else acc + term
    return acc.reshape(rows, -1)


def _residual_tail(x_ref, o, gate_ref, g2_ref, sh2_ref, sc2_ref, xo_ref, hp_ref):
    x_new = x_ref[...] + gate_ref[...] * o
    xo_ref[...] = x_new
    hp_ref[...] = _pack_halves(_rms_mod(x_new, g2_ref[...], sh2_ref[...], sc2_ref[...]))


def _outproj_even_kernel(x_ref, at_ref, a_ref, ap_ref, an_ref, dw_ref, db_ref, lg_ref, lb_ref, w_ref,
                         gate_ref, g2_ref, sh2_ref, sc2_ref, xo_ref, hp_ref, win_ref, cv_ref, shw_ref,
                         *, tiles_per_seq):
    tm = x_ref.shape[0]
    i = pl.program_id(0) % tiles_per_seq
    _fill_window(win_ref, a_ref, ap_ref, an_ref, i == 0, i == tiles_per_seq - 1)
    _fill_shifted(win_ref, shw_ref)
    for r0 in range(0, tm, CONV_ROWS):
        y = _dwconv_chunk(win_ref, shw_ref, dw_ref, r0, CONV_ROWS, CONV_A_K) + db_ref[...]
        mu = jnp.mean(y, axis=-1, keepdims=True)
        yc = y - mu
        var = jnp.mean(yc * yc, axis=-1, keepdims=True)
        yn = yc * lax.rsqrt(var + EPS) * lg_ref[...] + lb_ref[...]
        cv_ref[r0:r0 + CONV_ROWS, :] = (yn * _sigmoid(yn)).astype(BF16)
    aw = at_ref.shape[1]
    o = (jnp.dot(at_ref[...], w_ref[0:aw, :], preferred_element_type=F32)
         + jnp.dot(cv_ref[...], w_ref[aw:, :], preferred_element_type=F32))
    _residual_tail(x_ref, o, gate_ref, g2_ref, sh2_ref, sc2_ref, xo_ref, hp_ref)


def _outproj_odd_kernel(x_ref, gb_ref, u_ref, up_ref, un_ref, cw_ref, w_ref,
                        gate_ref, g2_ref, sh2_ref, sc2_ref, xo_ref, hp_ref, win_ref, cv_ref,
                        *, tiles_per_seq):
    tm = x_ref.shape[0]
    i = pl.program_id(0) % tiles_per_seq
    _fill_window(win_ref, u_ref, up_ref, un_ref, i == 0, i == tiles_per_seq - 1)
    for r0 in range(0, tm, CONV_ROWS):
        y = _dwconv_chunk(win_ref, None, cw_ref, r0, CONV_ROWS, SC_K)
        cv_ref[r0:r0 + CONV_ROWS, :] = (gb_ref[r0:r0 + CONV_ROWS, :].astype(F32) * y).astype(BF16)
    o = jnp.dot(cv_ref[...], w_ref[...], preferred_element_type=F32)
    _residual_tail(x_ref, o, gate_ref, g2_ref, sh2_ref, sc2_ref, xo_ref, hp_ref)


def _halo_specs(tm, width, n_rows):
    hb = tm // HALO
    last = n_rows // HALO - 1
    return [pl.BlockSpec((tm, width), lambda i: (i, 0)),
            pl.BlockSpec((HALO, width), lambda i: (jnp.maximum(i * hb - 1, 0), 0)),
            pl.BlockSpec((HALO, width), lambda i: (jnp.minimum((i + 1) * hb, last), 0))]


def _outproj(kernel_fn, name, x, lead, conv_in, params, w, gate, g2, shift2, scale2, rows_per_mod,
             seq_len, tm, shifted_window):
    t, d = x.shape
    cw = conv_in.shape[1]
    scratch = [pltpu.VMEM((tm + 2 * HALO, cw), F32), pltpu.VMEM((tm, cw), BF16)]
    if shifted_window:
        scratch.append(pltpu.VMEM((SUBLANES - 1, tm + 2 * HALO - SUBLANES, cw), F32))
    mod_map = lambda i: ((i * tm) // rows_per_mod, 0, 0)
    row_spec = lambda a: pl.BlockSpec((tm, a.shape[1]), lambda i: (i, 0))
    full_spec = lambda a: pl.BlockSpec(a.shape, lambda i: (0,) * a.ndim)
    in_specs = ([row_spec(x)] + [row_spec(a) for a in lead] + _halo_specs(tm, cw, t)
                + [full_spec(a) for a in params] + [full_spec(w)]
                + [pl.BlockSpec((None, 1, d), mod_map), full_spec(g2),
                   pl.BlockSpec((None, 1, d), mod_map), pl.BlockSpec((None, 1, d), mod_map)])
    return pl.pallas_call(
        functools.partial(kernel_fn, tiles_per_seq=seq_len // tm),
        out_shape=(jax.ShapeDtypeStruct((t, d), F32), jax.ShapeDtypeStruct((t, d // 2), U32)),
        grid=(t // tm,),
        in_specs=in_specs,
        out_specs=(pl.BlockSpec((tm, d), lambda i: (i, 0)), pl.BlockSpec((tm, d // 2), lambda i: (i, 0))),
        scratch_shapes=scratch,
        compiler_params=_cparams("parallel"),
        name=name,
    )(x, *lead, conv_in, conv_in, conv_in, *params, w, gate, g2, shift2, scale2)


def _router_kernel(hp_ref, rw_ref, bias_ref, tri_ref, route_ref, cnt_ref, base_ref):
    tm = hp_ref.shape[0]

    @pl.when(pl.program_id(0) == 0)
    def _():
        base_ref[...] = jnp.zeros_like(base_ref)

    lo, hi = _unpack_halves(hp_ref[...])
    h = jnp.concatenate([lo, hi], axis=1).astype(BF16)
    logits = jnp.dot(h, rw_ref[...], preferred_element_type=F32)
    lt = logits.T[0:N_EXPERTS, :]
    aff = _sigmoid(lt)
    sel = aff + bias_ref[...]
    rows = lambda a, e: a[e:e + 1, :]

    best = None
    for g in range(N_GROUPS):
        a, b, c, d = (rows(sel, EXPERTS_PER_GROUP * g + j) for j in range(EXPERTS_PER_GROUP))
        score = jnp.maximum(jnp.maximum(a + b, c + d), jnp.maximum(a, b) + jnp.maximum(c, d))
        if best is None:
            best, grp = score, jnp.zeros((1, tm), jnp.int32)
        else:
            better = score > best
            grp = jnp.where(better, g, grp)
            best = jnp.where(better, score, best)

    def in_group(a, j):
        out = rows(a, j)
        for g in range(1, N_GROUPS):
            out = jnp.where(grp == g, rows(a, EXPERTS_PER_GROUP * g + j), out)
        return out

    s = [in_group(sel, j) for j in range(EXPERTS_PER_GROUP)]
    f = [in_group(aff, j) for j in range(EXPERTS_PER_GROUP)]

    def first_argmax(vals):
        m = functools.reduce(jnp.maximum, vals)
        idx = jnp.full((1, tm), len(vals) - 1, jnp.int32)
        for j in range(len(vals) - 2, -1, -1):
            idx = jnp.where(vals[j] == m, j, idx)
        return idx

    i1 = first_argmax(s)
    i2 = first_argmax([jnp.where(i1 == j, -jnp.inf, s[j]) for j in range(EXPERTS_PER_GROUP)])
    e_lo = jnp.minimum(i1, i2)
    e_hi = jnp.maximum(i1, i2)
    pick = lambda vals, idx: functools.reduce(
        lambda acc, j: jnp.where(idx == j, vals[j], acc), range(1, len(vals)), vals[0])
    f_lo, f_hi = pick(f, e_lo), pick(f, e_hi)
    denom = f_lo + f_hi
    pair = jnp.where(e_lo == 0, e_hi - 1, jnp.where(e_lo == 1, e_hi + 1, N_PAIRS - 1))
    cls = grp * N_PAIRS + pair

    onehot = (lax.broadcasted_iota(jnp.int32, (CLASS_ROWS, tm), 0) == cls)
    onehot_f = onehot.astype(F32)
    before = jnp.dot(onehot_f.astype(BF16), tri_ref[...], preferred_element_type=F32) + base_ref[:, 0:1]
    rank = jnp.sum(onehot_f * before, axis=0, keepdims=True)
    base_ref[...] = base_ref[...] + jnp.sum(onehot_f, axis=1, keepdims=True)

    route_ref[...] = jnp.concatenate(
        [cls.astype(F32), rank, f_lo / denom, f_hi / denom, jnp.zeros((4, tm), F32)], axis=0)
    cnt_ref[...] = base_ref[...]


def _router(hp, rw, bias, tri, tm):
    t, hw = hp.shape
    return pl.pallas_call(
        _router_kernel,
        out_shape=(jax.ShapeDtypeStruct((8, t), F32), jax.ShapeDtypeStruct((CLASS_ROWS, LANES), F32)),
        grid=(t // tm,),
        in_specs=[pl.BlockSpec((tm, hw), lambda i: (i, 0)),
                  pl.BlockSpec(rw.shape, lambda i: (0, 0)),
                  pl.BlockSpec(bias.shape, lambda i: (0, 0)),
                  pl.BlockSpec((tm, tm), lambda i: (0, 0))],
        out_specs=(pl.BlockSpec((8, tm), lambda i: (0, i)),
                   pl.BlockSpec((CLASS_ROWS, LANES), lambda i: (0, 0))),
        scratch_shapes=[pltpu.VMEM((CLASS_ROWS, LANES), F32)],
        compiler_params=_cparams("arbitrary"),
        name="router",
    )(hp, rw, bias, tri)


def _dispatch_kernel(pos_ref, hp_ref, init_ref, hs_ref, sem):
    del init_ref
    tm = hp_ref.shape[0]
    base = pl.program_id(0) * tm

    for r in range(tm):
        pltpu.async_copy(hp_ref.at[pl.ds(r, 1)], hs_ref.at[pl.ds(pos_ref[base + r], 1)], sem,
                         priority=r % DMA_QUEUES)
    pltpu.make_async_copy(hp_ref, hs_ref.at[pl.ds(0, tm)], sem).wait()


def _dispatch(pos, hp, hs_init, tm):
    t, hw = hp.shape
    n_rows = hs_init.shape[0]
    return pl.pallas_call(
        _dispatch_kernel,
        out_shape=jax.ShapeDtypeStruct((n_rows, hw), U32),
        grid_spec=pltpu.PrefetchScalarGridSpec(
            num_scalar_prefetch=1,
            grid=(t // tm,),
            in_specs=[pl.BlockSpec((tm, hw), lambda i, pos: (i, 0)),
                      pl.BlockSpec(memory_space=pl.ANY)],
            out_specs=pl.BlockSpec(memory_space=pl.ANY),
            scratch_shapes=[pltpu.SemaphoreType.DMA(())]),
        input_output_aliases={2: 0},
        compiler_params=_cparams("arbitrary"),
        name="dispatch",
    )(pos, hp, hs_init)


def _moe_kernel(elo_ref, ehi_ref, nused_ref, hs_ref, wgu_lo, wd_lo, wgu_hi, wd_hi, ys_ref):
    del elo_ref, ehi_ref

    @pl.when(pl.program_id(0) >= nused_ref[0])
    def _():
        ys_ref[...] = jnp.zeros_like(ys_ref)

    @pl.when(pl.program_id(0) < nused_ref[0])
    def _():
        lo, hi = _unpack_halves(hs_ref[...])
        h = jnp.concatenate([lo, hi], axis=1).astype(BF16)

        def expert(wgu_ref, wd_ref):
            gu = jnp.dot(h, wgu_ref[...], preferred_element_type=F32)
            g, u = gu[:, :D_EXPERT], gu[:, D_EXPERT:]
            act = (g * _sigmoid(g) * u).astype(BF16)
            return jnp.dot(act, wd_ref[...], preferred_element_type=F32)

        ya = lax.bitcast_convert_type(expert(wgu_lo, wd_lo).astype(BF16).astype(F32), U32)
        yb = lax.bitcast_convert_type(expert(wgu_hi, wd_hi).astype(BF16).astype(F32), U32)
        ys_ref[...] = (ya >> 16) | (yb & jnp.uint32(0xFFFF0000))


def _moe(e_lo, e_hi, n_used, hs, wgu, wd):
    n_rows, hw = hs.shape
    d = wd.shape[2]
    n_blocks = n_rows // MOE_BLOCK
    row_map = lambda j, lo, hi, nu: (jnp.minimum(j, nu[0] - 1), 0)
    return pl.pallas_call(
        _moe_kernel,
        out_shape=jax.ShapeDtypeStruct((n_rows, d), U32),
        grid_spec=pltpu.PrefetchScalarGridSpec(
            num_scalar_prefetch=3,
            grid=(n_blocks,),
            in_specs=[pl.BlockSpec((MOE_BLOCK, hw), row_map),
                      pl.BlockSpec((None,) + wgu.shape[1:], lambda j, lo, hi, nu: (lo[j], 0, 0)),
                      pl.BlockSpec((None,) + wd.shape[1:], lambda j, lo, hi, nu: (lo[j], 0, 0)),
                      pl.BlockSpec((None,) + wgu.shape[1:], lambda j, lo, hi, nu: (hi[j], 0, 0)),
                      pl.BlockSpec((None,) + wd.shape[1:], lambda j, lo, hi, nu: (hi[j], 0, 0))],
            out_specs=pl.BlockSpec((MOE_BLOCK, d), lambda j, lo, hi, nu: (j, 0))),
        compiler_params=_cparams("arbitrary"),
        name="experts",
    )(e_lo, e_hi, n_used, hs, wgu, wd, wgu, wd)


def _combine_kernel(pos_ref, x_ref, rc_ref, gate_ref, ys_ref, xo_ref, ybuf, sems):
    x_new, issue = _combined_residual(pos_ref, x_ref, rc_ref, gate_ref, ys_ref, ybuf, sems)
    xo_ref[...] = x_new
    issue(0, 1)


def _combine(pos, x, rc, gate, ys, rows_per_mod, tm):
    t, d = x.shape
    return pl.pallas_call(
        _combine_kernel,
        out_shape=jax.ShapeDtypeStruct((t, d), F32),
        grid_spec=pltpu.PrefetchScalarGridSpec(
            num_scalar_prefetch=1,
            grid=(t // tm,),
            in_specs=_combine_in_specs(tm, d, rc.shape[1], rows_per_mod),
            out_specs=pl.BlockSpec((tm, d), lambda i, pos: (i, 0)),
            scratch_shapes=_combine_scratch(tm, d)),
        compiler_params=_cparams("arbitrary"),
        name="combine",
    )(pos, x, rc, gate, ys)


def _deinterleave(base):
    return [base + 2 * p for p in range(HEAD_DIM // 2)] + [base + 2 * p + 1 for p in range(HEAD_DIM // 2)]


def _even_in_perm(n_cols):
    cols = []
    half = N_HEADS // N_KV_HEADS
    for m in range(half):
        cols += _deinterleave(HEAD_DIM * m) + _deinterleave(HEAD_DIM * (m + half))
    for kh in range(N_KV_HEADS):
        cols += _deinterleave(ATTN_WIDTH + HEAD_DIM * kh)
    cols += list(range(QK_WIDTH, n_cols))
    return np.asarray(cols, np.int32)


def _even_out_perm(n_rows):
    rows = []
    half = N_HEADS // N_KV_HEADS
    for m in range(half):
        rows += list(range(HEAD_DIM * m, HEAD_DIM * (m + 1)))
        rows += list(range(HEAD_DIM * (m + half), HEAD_DIM * (m + half + 1)))
    rows += list(range(ATTN_WIDTH, n_rows))
    return np.asarray(rows, np.int32)


def _rope_tables(n_tokens):
    rows = n_tokens // GRID_W
    row = jnp.repeat(jnp.arange(rows, dtype=F32), GRID_W)
    col = jnp.tile(jnp.arange(GRID_W, dtype=F32), rows)
    inv = ROPE_THETA ** (-jnp.arange(ROPE_PAIRS_PER_AXIS, dtype=F32) / ROPE_PAIRS_PER_AXIS)
    ang = jnp.concatenate([row[:, None] * inv, col[:, None] * inv], axis=-1)
    cos, sin = jnp.cos(ang), jnp.sin(ang)
    reps = LANES // HEAD_DIM
    return (jnp.tile(jnp.concatenate([cos, cos], axis=-1), (1, reps)),
            jnp.tile(jnp.concatenate([-sin, sin], axis=-1), (1, reps)))


def _head_mean_matrix():
    blk = np.kron(np.eye(LANES // HEAD_DIM), np.full((HEAD_DIM, HEAD_DIM), 1.0 / HEAD_DIM))
    return jnp.asarray(np.concatenate([blk, blk], axis=0), BF16)


def _moe_layer(tokens_hp, hs_buf, router_wp, router_b, tri, wgu, wd, tm):
    t = tokens_hp.shape[0]
    route, counts = _router(tokens_hp, router_wp, router_b, tri, tm)
    cls = route[0].astype(jnp.int32)
    rank = route[1].astype(jnp.int32)
    cnt = counts[:N_CLASSES, 0].astype(jnp.int32)
    padded = (cnt + MOE_BLOCK - 1) // MOE_BLOCK * MOE_BLOCK
    pad_end = jnp.cumsum(padded)
    pos = (pad_end - padded)[cls] + rank
    n_blocks = hs_buf.shape[0] // MOE_BLOCK
    assert n_blocks >= -(-t // MOE_BLOCK) + N_CLASSES
    n_used = pad_end[-1] // MOE_BLOCK
    blk = jnp.minimum(jnp.arange(n_blocks, dtype=jnp.int32), n_used - 1)
    bcls = jnp.sum((pad_end[None, :] <= (blk * MOE_BLOCK)[:, None]).astype(jnp.int32), axis=1)
    bcls = jnp.minimum(bcls, N_CLASSES - 1)
    e_lo = EXPERTS_PER_GROUP * (bcls // N_PAIRS) + jnp.asarray(PAIR_LO, jnp.int32)[bcls % N_PAIRS]
    e_hi = EXPERTS_PER_GROUP * (bcls // N_PAIRS) + jnp.asarray(PAIR_HI, jnp.int32)[bcls % N_PAIRS]
    hs = _dispatch(pos, tokens_hp, hs_buf, math.gcd(tm, ROW_TILE))
    ys = _moe(e_lo, e_hi, n_used.reshape(1).astype(jnp.int32), hs, wgu, wd)
    return pos, route.T, ys, hs


def kernel(x, c, ctx, c_ctx, w_mod, b_mod, norm_mix_g, norm_ffn_g, ev_w_in, ev_q_norm_g, ev_k_norm_g,
           ev_dw_w, ev_dw_b, ev_ln_g, ev_ln_b, ev_w_out, od_w_in, od_conv_w, od_w_out, router_w,
           router_bias, moe_w_gate, moe_w_up, moe_w_down):
    b, s, d = x.shape
    n_ctx = ctx.shape[1]
    depth = w_mod.shape[0]
    last_attn = (depth - 1) - ((depth - 1) % 2)
    tm = min(ROW_TILE, s)
    tmz = min(ROW_TILE, n_ctx)
    tq = min(Q_TILE, s)
    tqz = min(Q_TILE, n_ctx)
    assert s % tm == 0 and n_ctx % tmz == 0 and s % tq == 0 and n_ctx % tqz == 0 and s % GRID_W == 0

    r = -(-(b + 1) // 8) * 8
    c_all = jnp.zeros((r, d), F32).at[:b].set(c).at[b].set(c_ctx)
    mods = _modulation(c_all, w_mod, b_mod)

    def mod_x(l, j):
        return mods[l, :b, j * d:(j + 1) * d].reshape(b, 1, d)

    def mod_z(l, j):
        return mods[l, b:b + 1, j * d:(j + 1) * d].reshape(1, 1, d)

    cos_t, sin_t = _rope_tables(s)
    gmat = _head_mean_matrix()
    in_perm = _even_in_perm(ev_w_in.shape[2])
    out_perm = _even_out_perm(ev_w_out.shape[1])
    head_perm = np.asarray(_deinterleave(0), np.int32)
    router_wp = jnp.zeros((d, LANES), F32).at[:, :N_EXPERTS].set(router_w).astype(BF16)
    router_b = router_bias.astype(F32).reshape(N_EXPERTS, 1)
    tri_n = min(ROUTER_TILE, b * s)
    tri = jnp.asarray(np.triu(np.ones((tri_n, tri_n), np.float32), 1), BF16)

    xf = x.reshape(b * s, d)
    zf = ctx.reshape(b * n_ctx, d)
    pending = None
    ctm = min(Q_TILE, s, n_ctx)
    n_routed = b * s + (b * n_ctx if last_attn > 0 else 0)
    hs_buf = jnp.zeros(((-(-n_routed // MOE_BLOCK) + N_CLASSES) * MOE_BLOCK, d // 2), U32)
    for l in range(depth):
        i = l // 2
        ctx_in = l <= last_attn
        ctx_out = l < last_attn
        gmix = norm_mix_g[l].reshape(1, d)
        gffn = norm_ffn_g[l].reshape(1, d)
        if l % 2 == 0:
            w_in = ev_w_in[i][:, in_perm].astype(BF16)
            w_out = ev_w_out[i][out_perm, :].astype(BF16)
            gq = ev_q_norm_g[i][head_perm] * (ATTN_SCALE * LOG2E)
            gk = ev_k_norm_g[i][head_perm]
            gqk = jnp.concatenate([jnp.tile(gq, N_HEADS), jnp.tile(gk, N_KV_HEADS)]).reshape(1, QK_WIDTH)
            conv_params = [jnp.repeat(ev_dw_w[i], SUBLANES, axis=0), ev_dw_b[i].reshape(1, -1),
                           ev_ln_g[i].reshape(1, -1),
                           ev_ln_b[i].reshape(1, -1)]
            res = _inproj_even(xf, pending, gmix, mod_x(l, 0), mod_x(l, 1), s, w_in, gqk, cos_t, sin_t, gmat,
                               tm, True)
            if pending is not None:
                xf, res = res[0], res[1:]
            q, k, v, a = res
            k_all = k.reshape(b, s, KV_WIDTH)
            v_all = v.reshape(N_KV_HEADS, b, s, LANES)
            if ctx_in:
                qz, kz, vz, az = _inproj_even(zf, None, gmix, mod_z(l, 0), mod_z(l, 1), b * n_ctx, w_in, gqk,
                                              cos_t, sin_t, gmat, tmz, False)
                kz = kz.reshape(b, n_ctx, KV_WIDTH)
                vz = vz.reshape(N_KV_HEADS, b, n_ctx, LANES)
                k_all = jnp.concatenate([k_all, kz], axis=1)
                v_all = jnp.concatenate([v_all, vz], axis=2)
            attn = _attention(q, k_all, v_all, s, tq)
            xf, hp = _outproj(_outproj_even_kernel, "outproj_even", xf, [attn], a, conv_params, w_out,
                              mod_x(l, 2), gffn, mod_x(l, 3), mod_x(l, 4), s, s, tm, True)
            if ctx_out:
                attn_z = _attention(qz, kz, vz, n_ctx, tqz)
                zf, hpz = _outproj(_outproj_even_kernel, "outproj_even_ctx", zf, [attn_z], az, conv_params,
                                   w_out, mod_z(l, 2), gffn, mod_z(l, 3), mod_z(l, 4), b * n_ctx, n_ctx, tmz,
                                   True)
        else:
            w_in = od_w_in[i].astype(BF16)
            w_out = od_w_out[i].astype(BF16)
            conv_params = [jnp.repeat(od_conv_w[i], SUBLANES, axis=0)]
            res = _inproj_odd(xf, pending, gmix, mod_x(l, 0), mod_x(l, 1), s, w_in, tm)
            if pending is not None:
                xf, res = res[0], res[1:]
            gb, u = res
            xf, hp = _outproj(_outproj_odd_kernel, "outproj_odd", xf, [gb], u, conv_params, w_out,
                              mod_x(l, 2), gffn, mod_x(l, 3), mod_x(l, 4), s, s, tm, False)
            if ctx_out:
                gbz, uz = _inproj_odd(zf, None, gmix, mod_z(l, 0), mod_z(l, 1), b * n_ctx, w_in, tmz)
                zf, hpz = _outproj(_outproj_odd_kernel, "outproj_odd_ctx", zf, [gbz], uz, conv_params, w_out,
                                   mod_z(l, 2), gffn, mod_z(l, 3), mod_z(l, 4), b * n_ctx, n_ctx, tmz, False)

        wgu = jnp.concatenate([moe_w_gate[l], moe_w_up[l]], axis=-1).astype(BF16)
        wd = moe_w_down[l].astype(BF16)
        tokens = jnp.concatenate([hp, hpz], axis=0) if ctx_out else hp
        rtm = math.gcd(tri_n, math.gcd(b * s, b * n_ctx)) if ctx_out else tri_n
        pos, rc, ys, hs_buf = _moe_layer(tokens, hs_buf, router_wp, router_b, tri[:rtm, :rtm], wgu, wd, rtm)
        n_lat = b * s
        pending = (pos[:n_lat], rc[:n_lat], mod_x(l, 5), ys)
        if ctx_out:
            zf = _combine(pos[n_lat:], zf, rc[n_lat:], mod_z(l, 5), ys, b * n_ctx, ctm)
    pos, rc, gate, ys = pending
    return _combine(pos, xf, rc, gate, ys, s, ctm).reshape(b, s, d)
```

```python
import functools
import math

import numpy as np
import jax
import jax.numpy as jnp
from jax import lax
from jax.experimental import pallas as pl
from jax.experimental.pallas import tpu as pltpu

F32 = jnp.float32
BF16 = jnp.bfloat16
U32 = jnp.uint32

GRID_W = 64
EPS = 1e-6
N_HEADS = 8
N_KV_HEADS = 2
GQA_GROUP = N_HEADS // N_KV_HEADS
HEAD_DIM = 64
ATTN_WIDTH = N_HEADS * HEAD_DIM
KV_WIDTH = N_KV_HEADS * HEAD_DIM
QK_WIDTH = ATTN_WIDTH + KV_WIDTH
ATTN_SCALE = HEAD_DIM ** -0.5
ROPE_THETA = 10000.0
ROPE_PAIRS_PER_AXIS = HEAD_DIM // 4
CONV_A_K = 31
SC_K = 3
N_EXPERTS = 16
N_GROUPS = 4
EXPERTS_PER_GROUP = N_EXPERTS // N_GROUPS
D_EXPERT = 512
PAIR_LO = (0, 0, 0, 1, 1, 2)
PAIR_HI = (1, 2, 3, 2, 3, 3)
N_PAIRS = len(PAIR_LO)
N_CLASSES = N_GROUPS * N_PAIRS
CLASS_ROWS = 32

LANES = 128
SUBLANES = 8
MXU_COLS = 256
HALO = 16
VMEM_LIMIT = 48 * 1024 * 1024
ROW_TILE = 512
Q_TILE = 256
ROUTER_TILE = 1024
DMA_QUEUES = 2
MOE_BLOCK = 512
CONV_ROWS = 64
LOG2E = math.log2(math.e)


def _cparams(*sem):
    return pltpu.CompilerParams(dimension_semantics=sem, vmem_limit_bytes=VMEM_LIMIT)


def _sigmoid(x):
    return 1.0 / (1.0 + jnp.exp(-x))


def _rms_mod(x, g, shift, scale):
    y = x * lax.rsqrt(jnp.mean(x * x, axis=-1, keepdims=True) + EPS) * g
    return y * (1.0 + scale) + shift


def _pack_halves(h):
    w = h.shape[1] // 2
    lo = lax.bitcast_convert_type(h[:, :w].astype(BF16).astype(F32), U32)
    hi = lax.bitcast_convert_type(h[:, w:].astype(BF16).astype(F32), U32)
    return (lo >> 16) | (hi & jnp.uint32(0xFFFF0000))


def _unpack_halves(w):
    lo = lax.bitcast_convert_type(w << 16, F32)
    hi = lax.bitcast_convert_type(w & jnp.uint32(0xFFFF0000), F32)
    return lo, hi


def _issue_row_gathers(pos_ref, ys_ref, ybuf, sems, tile, slot, rows):
    tm = ybuf.shape[1]
    for r in rows:
        pltpu.async_copy(ys_ref.at[pos_ref[tile * tm + r]], ybuf.at[slot, r], sems.at[slot],
                         priority=r % DMA_QUEUES)


def _wait_row_gathers(ys_ref, ybuf, sems, slot):
    pltpu.make_async_copy(ys_ref.at[pl.ds(0, ybuf.shape[1])], ybuf.at[slot], sems.at[slot]).wait()


def _combined_residual(pos_ref, x_ref, rc_ref, gate_ref, ys_ref, ybuf, sems):
    tm = ybuf.shape[1]
    i = pl.program_id(0)
    n = pl.num_programs(0)

    @pl.when(i == 0)
    def _():
        _issue_row_gathers(pos_ref, ys_ref, ybuf, sems, 0, 0, range(tm))

    _wait_row_gathers(ys_ref, ybuf, sems, i % 2)
    rc = rc_ref[...]
    w_lo, w_hi = rc[:, 2:3], rc[:, 3:4]
    y = jnp.concatenate([w_lo * ya + w_hi * yb
                         for ya, yb in map(_unpack_halves, _token_tiles_to_row_chunks(ybuf[i % 2]))], axis=1)
    x_new = x_ref[...] + gate_ref[...] * y

    def issue(k, n_slices):
        _issue_row_gathers(pos_ref, ys_ref, ybuf, sems, jnp.minimum(i + 1, n - 1), (i + 1) % 2,
                           range(k * tm // n_slices, (k + 1) * tm // n_slices))
        if k == n_slices - 1:
            @pl.when(i == n - 1)
            def _():
                _wait_row_gathers(ys_ref, ybuf, sems, (i + 1) % 2)

    return x_new, issue


def _combine_in_specs(tm, d, rc_width, rows_per_mod):
    return [pl.BlockSpec((tm, d), lambda i, pos: (i, 0)),
            pl.BlockSpec((tm, rc_width), lambda i, pos: (i, 0)),
            pl.BlockSpec((None, 1, d), lambda i, pos: ((i * tm) // rows_per_mod, 0, 0)),
            pl.BlockSpec(memory_space=pl.ANY)]


def _combine_scratch(tm, d):
    return [pltpu.VMEM((2, tm, d // LANES, LANES), U32), pltpu.SemaphoreType.DMA((2,))]


def _mod_kernel(c_ref, w_ref, b_ref, o_ref):
    c = c_ref[...]
    a = (c * _sigmoid(c)).astype(BF16)
    o_ref[...] = jnp.dot(a, w_ref[...].astype(BF16), preferred_element_type=F32) + b_ref[...]


def _modulation(c_all, w_mod, b_mod):
    depth, d, n = w_mod.shape
    r = c_all.shape[0]
    nt = n // d
    return pl.pallas_call(
        _mod_kernel,
        out_shape=jax.ShapeDtypeStruct((depth, r, n), F32),
        grid=(depth, nt),
        in_specs=[pl.BlockSpec((r, d), lambda l, j: (0, 0)),
                  pl.BlockSpec((None, d, d), lambda l, j: (l, 0, j)),
                  pl.BlockSpec((None, 1, d), lambda l, j: (l, 0, j))],
        out_specs=pl.BlockSpec((None, r, d), lambda l, j: (l, 0, j)),
        compiler_params=_cparams("parallel", "parallel"),
        name="modulation",
    )(c_all, w_mod, b_mod.reshape(depth, 1, n))


def _inproj_kernel(*refs, body, n_in, combine):
    if combine:
        pos_ref, x_ref, rc_ref, gate_ref, ys_ref = refs[:5]
        ybuf, sems = refs[-2:]
        rest = refs[5:-2]
        x, issue = _combined_residual(pos_ref, x_ref, rc_ref, gate_ref, ys_ref, ybuf, sems)
        rest[3 + n_in][...] = x
        ins, outs = rest[:3 + n_in], rest[4 + n_in:]
    else:
        x = refs[0][...]
        issue = lambda k, n_slices: None
        ins, outs = refs[1:4 + n_in], refs[4 + n_in:]
    h = _rms_mod(x, ins[0][...], ins[1][...], ins[2][...]).astype(BF16)
    body(h, issue, *ins[3:], *outs)


def _inproj(body, name, x, pending, g, shift, scale, rows_per_mod, extra, extra_specs, out_shape, out_specs,
            tm):
    t, d = x.shape
    mod_map = lambda i, *_: ((i * tm) // rows_per_mod, 0, 0)
    in_specs = [pl.BlockSpec((1, d), lambda i, *_: (0, 0)), pl.BlockSpec((None, 1, d), mod_map),
                pl.BlockSpec((None, 1, d), mod_map)] + list(extra_specs)
    kern = functools.partial(_inproj_kernel, body=body, n_in=len(extra), combine=pending is not None)
    if pending is None:
        return pl.pallas_call(
            kern, out_shape=tuple(out_shape), grid=(t // tm,),
            in_specs=[pl.BlockSpec((tm, d), lambda i: (i, 0))] + in_specs, out_specs=tuple(out_specs),
            compiler_params=_cparams("parallel"), name=name,
        )(x, g, shift, scale, *extra)
    pos, rc, gate, ys = pending
    return pl.pallas_call(
        kern,
        out_shape=(jax.ShapeDtypeStruct((t, d), F32),) + tuple(out_shape),
        grid_spec=pltpu.PrefetchScalarGridSpec(
            num_scalar_prefetch=1, grid=(t // tm,),
            in_specs=_combine_in_specs(tm, d, rc.shape[1], rows_per_mod) + in_specs,
            out_specs=(pl.BlockSpec((tm, d), lambda i, pos: (i, 0)),) + tuple(out_specs),
            scratch_shapes=_combine_scratch(tm, d)),
        compiler_params=_cparams("arbitrary"), name=name + "_combine",
    )(pos, x, rc, gate, ys, g, shift, scale, *extra)


def _inproj_even_body(h, issue, w_ref, gqk_ref, cos_ref, sin_ref, gm_ref, q_ref, k_ref, v_ref, a_ref, *, rope):
    tm = h.shape[0]
    lane = lax.broadcasted_iota(jnp.int32, (tm, LANES), 1)
    first_head = lane < HEAD_DIM
    first_half = (lane % HEAD_DIM) < (HEAD_DIM // 2)
    c0 = QK_WIDTH + KV_WIDTH
    cw = (w_ref.shape[1] - c0) // 2
    issue(0, 1)
    p = jnp.dot(h, w_ref[...], preferred_element_type=F32)
    for m in range(QK_WIDTH // LANES):
        y = p[:, LANES * m:LANES * (m + 1)]
        sq = y * y
        sq_hi = sq.astype(BF16)
        sq_lo = (sq - sq_hi.astype(F32)).astype(BF16)
        ms = jnp.dot(jnp.concatenate([sq_hi, sq_lo], axis=1), gm_ref[...], preferred_element_type=F32)
        y = y * lax.rsqrt(ms + EPS) * gqk_ref[:, LANES * m:LANES * (m + 1)]
        if rope:
            partner = jnp.where(first_half, pltpu.roll(y, LANES - HEAD_DIM // 2, 1),
                                pltpu.roll(y, HEAD_DIM // 2, 1))
            y = y * cos_ref[...] + partner * sin_ref[...]
        if m < ATTN_WIDTH // LANES:
            q_ref[m] = jnp.where(first_head, y, 0.0).astype(BF16)
            q_ref[m + ATTN_WIDTH // LANES] = jnp.where(first_head, 0.0, y).astype(BF16)
        else:
            k_ref[...] = y.astype(BF16)
    v = p[:, QK_WIDTH:c0]
    v_ref[0] = jnp.where(first_head, v, 1.0).astype(BF16)
    v_ref[1] = jnp.where(first_head, 1.0, v).astype(BF16)
    a_ref[...] = (p[:, c0:c0 + cw] * _sigmoid(p[:, c0 + cw:])).astype(BF16)


def _inproj_even(x, pending, g, shift, scale, rows_per_mod, w, gqk, cos_t, sin_t, gmat, tm, rope):
    t, d = x.shape
    n = w.shape[1]
    cw = (n - QK_WIDTH - KV_WIDTH) // 2
    n_pos = cos_t.shape[0] // tm
    return _inproj(
        functools.partial(_inproj_even_body, rope=rope), "inproj_even", x, pending, g, shift, scale,
        rows_per_mod, [w, gqk, cos_t, sin_t, gmat],
        [pl.BlockSpec((d, n), lambda i, *_: (0, 0)),
         pl.BlockSpec((1, QK_WIDTH), lambda i, *_: (0, 0)),
         pl.BlockSpec((tm, LANES), lambda i, *_: (i % n_pos, 0)),
         pl.BlockSpec((tm, LANES), lambda i, *_: (i % n_pos, 0)),
         pl.BlockSpec((2 * LANES, LANES), lambda i, *_: (0, 0))],
        [jax.ShapeDtypeStruct((N_HEADS, t, LANES), BF16), jax.ShapeDtypeStruct((t, KV_WIDTH), BF16),
         jax.ShapeDtypeStruct((N_KV_HEADS, t, LANES), BF16), jax.ShapeDtypeStruct((t, cw), BF16)],
        [pl.BlockSpec((N_HEADS, tm, LANES), lambda i, *_: (0, i, 0)),
         pl.BlockSpec((tm, KV_WIDTH), lambda i, *_: (i, 0)),
         pl.BlockSpec((N_KV_HEADS, tm, LANES), lambda i, *_: (0, i, 0)),
         pl.BlockSpec((tm, cw), lambda i, *_: (i, 0))],
        tm)


def _inproj_odd_body(h, issue, w_ref, gb_ref, u_ref):
    w = w_ref.shape[1] // 3
    n_phase = w // MXU_COLS
    proj = lambda lo: jnp.dot(h, w_ref[:, lo:lo + MXU_COLS], preferred_element_type=F32)
    for c in range(n_phase):
        issue(c, n_phase)
        cols = slice(MXU_COLS * c, MXU_COLS * (c + 1))
        gb_ref[:, cols] = proj(MXU_COLS * c).astype(BF16)
        u_ref[:, cols] = (proj(w + MXU_COLS * c) * proj(2 * w + MXU_COLS * c)).astype(BF16)


def _inproj_odd(x, pending, g, shift, scale, rows_per_mod, w, tm):
    t, d = x.shape
    n = w.shape[1]
    return _inproj(
        _inproj_odd_body, "inproj_odd", x, pending, g, shift, scale, rows_per_mod, [w],
        [pl.BlockSpec((d, n), lambda i, *_: (0, 0))],
        [jax.ShapeDtypeStruct((t, n // 3), BF16), jax.ShapeDtypeStruct((t, n // 3), BF16)],
        [pl.BlockSpec((tm, n // 3), lambda i, *_: (i, 0)), pl.BlockSpec((tm, n // 3), lambda i, *_: (i, 0))],
        tm)


def _attn_kernel(q_ref, k_ref, v_ref, o_ref, acc_ref):
    tq = q_ref.shape[1]
    first = lax.broadcasted_iota(jnp.int32, (tq, LANES), 1) < HEAD_DIM

    def pair(m, carry):
        outs = []
        for g in range(N_KV_HEADS):
            q = q_ref[m + GQA_GROUP * g]
            s = lax.dot_general(q, k_ref[...], (((1,), (1,)), ((), ())), preferred_element_type=F32)
            p = jnp.exp2(s - jnp.max(s, axis=-1, keepdims=True)).astype(BF16)
            o = jnp.dot(p, v_ref[g], preferred_element_type=F32)
            l = o[:, HEAD_DIM:HEAD_DIM + 1] if g == 0 else o[:, 0:1]
            outs.append(o / l)
        acc_ref[m] = jnp.where(first, outs[0], outs[1])
        return carry

    lax.fori_loop(0, GQA_GROUP, pair, 0)
    o_ref[...] = jnp.concatenate([acc_ref[m] for m in range(GQA_GROUP)], axis=1).astype(BF16)


def _attention(q, k_all, v_all, s_q, tq):
    t = q.shape[1]
    b, skv, _ = k_all.shape
    nq = s_q // tq
    return pl.pallas_call(
        _attn_kernel,
        out_shape=jax.ShapeDtypeStruct((t, ATTN_WIDTH), BF16),
        grid=(b, nq),
        in_specs=[pl.BlockSpec((N_HEADS, tq, LANES), lambda bi, i: (0, bi * nq + i, 0)),
                  pl.BlockSpec((None, skv, LANES), lambda bi, i: (bi, 0, 0)),
                  pl.BlockSpec((N_KV_HEADS, None, skv, LANES), lambda bi, i: (0, bi, 0, 0))],
        out_specs=pl.BlockSpec((tq, ATTN_WIDTH), lambda bi, i: (bi * nq + i, 0)),
        scratch_shapes=[pltpu.VMEM((GQA_GROUP, tq, LANES), F32)],
        compiler_params=_cparams("parallel", "parallel"),
        name="attention",
    )(q, k_all, v_all)


def _fill_window(win_ref, cur_ref, prev_ref, next_ref, is_first, is_last):
    tm = cur_ref.shape[0]
    win_ref[0:HALO, :] = jnp.where(is_first, 0.0, prev_ref[...].astype(F32))
    win_ref[HALO:HALO + tm, :] = cur_ref[...].astype(F32)
    win_ref[HALO + tm:2 * HALO + tm, :] = jnp.where(is_last, 0.0, next_ref[...].astype(F32))


def _fill_shifted(win_ref, sh_ref):
    n = sh_ref.shape[1]
    for s in range(1, SUBLANES):
        sh_ref[s - 1] = win_ref[s:s + n, :]


def _dwconv_chunk(win_ref, sh_ref, w_ref, r0, rows, taps):
    off = HALO - taps // 2
    acc = None
    for kk in range(taps):
        q, s = divmod(off + kk, SUBLANES)
        if sh_ref is None:
            src = win_ref[r0 + off + kk:r0 + off + kk + rows, :]
        elif s == 0:
            src = win_ref[r0 + SUBLANES * q:r0 + SUBLANES * q + rows, :]
        else:
            src = sh_ref[s - 1, r0 + SUBLANES * q:r0 + SUBLANES * q + rows, :]
        term = src.reshape(rows // SUBLANES, SUBLANES, -1) * w_ref[SUBLANES * kk:SUBLANES * (kk + 1), :]
        acc = term if acc is None else acc + term
    return acc.reshape(rows, -1)


def _residual_tail(x_ref, o, gate_ref, g2_ref, sh2_ref, sc2_ref, xo_ref, hp_ref):
    x_new = x_ref[...] + gate_ref[...] * o
    xo_ref[...] = x_new
    hp_ref[...] = _pack_halves(_rms_mod(x_new, g2_ref[...], sh2_ref[...], sc2_ref[...]))


def _outproj_even_kernel(x_ref, at_ref, a_ref, ap_ref, an_ref, dw_ref, db_ref, lg_ref, lb_ref, w_ref,
                         gate_ref, g2_ref, sh2_ref, sc2_ref, xo_ref, hp_ref, win_ref, cv_ref, shw_ref,
                         *, tiles_per_seq):
    tm = x_ref.shape[0]
    i = pl.program_id(0) % tiles_per_seq
    _fill_window(win_ref, a_ref, ap_ref, an_ref, i == 0, i == tiles_per_seq - 1)
    _fill_shifted(win_ref, shw_ref)
    for r0 in range(0, tm, CONV_ROWS):
        y = _dwconv_chunk(win_ref, shw_ref, dw_ref, r0, CONV_ROWS, CONV_A_K) + db_ref[...]
        mu = jnp.mean(y, axis=-1, keepdims=True)
        yc = y - mu
        var = jnp.mean(yc * yc, axis=-1, keepdims=True)
        yn = yc * lax.rsqrt(var + EPS) * lg_ref[...] + lb_ref[...]
        cv_ref[r0:r0 + CONV_ROWS, :] = (yn * _sigmoid(yn)).astype(BF16)
    aw = at_ref.shape[1]
    o = (jnp.dot(at_ref[...], w_ref[0:aw, :], preferred_element_type=F32)
         + jnp.dot(cv_ref[...], w_ref[aw:, :], preferred_element_type=F32))
    _residual_tail(x_ref, o, gate_ref, g2_ref, sh2_ref, sc2_ref, xo_ref, hp_ref)


def _outproj_odd_kernel(x_ref, gb_ref, u_ref, up_ref, un_ref, cw_ref, w_ref,
                        gate_ref, g2_ref, sh2_ref, sc2_ref, xo_ref, hp_ref, win_ref, cv_ref,
                        *, tiles_per_seq):
    tm = x_ref.shape[0]
    i = pl.program_id(0) % tiles_per_seq
    _fill_window(win_ref, u_ref, up_ref, un_ref, i == 0, i == tiles_per_seq - 1)
    for r0 in range(0, tm, CONV_ROWS):
        y = _dwconv_chunk(win_ref, None, cw_ref, r0, CONV_ROWS, SC_K)
        cv_ref[r0:r0 + CONV_ROWS, :] = (gb_ref[r0:r0 + CONV_ROWS, :].astype(F32) * y).astype(BF16)
    o = jnp.dot(cv_ref[...], w_ref[...], preferred_element_type=F32)
    _residual_tail(x_ref, o, gate_ref, g2_ref, sh2_ref, sc2_ref, xo_ref, hp_ref)


def _halo_specs(tm, width, n_rows):
    hb = tm // HALO
    last = n_rows // HALO - 1
    return [pl.BlockSpec((tm, width), lambda i: (i, 0)),
            pl.BlockSpec((HALO, width), lambda i: (jnp.maximum(i * hb - 1, 0), 0)),
            pl.BlockSpec((HALO, width), lambda i: (jnp.minimum((i + 1) * hb, last), 0))]


def _outproj(kernel_fn, name, x, lead, conv_in, params, w, gate, g2, shift2, scale2, rows_per_mod,
             seq_len, tm, shifted_window):
    t, d = x.shape
    cw = conv_in.shape[1]
    scratch = [pltpu.VMEM((tm + 2 * HALO, cw), F32), pltpu.VMEM((tm, cw), BF16)]
    if shifted_window:
        scratch.append(pltpu.VMEM((SUBLANES - 1, tm + 2 * HALO - SUBLANES, cw), F32))
    mod_map = lambda i: ((i * tm) // rows_per_mod, 0, 0)
    row_spec = lambda a: pl.BlockSpec((tm, a.shape[1]), lambda i: (i, 0))
    full_spec = lambda a: pl.BlockSpec(a.shape, lambda i: (0,) * a.ndim)
    in_specs = ([row_spec(x)] + [row_spec(a) for a in lead] + _halo_specs(tm, cw, t)
                + [full_spec(a) for a in params] + [full_spec(w)]
                + [pl.BlockSpec((None, 1, d), mod_map), full_spec(g2),
                   pl.BlockSpec((None, 1, d), mod_map), pl.BlockSpec((None, 1, d), mod_map)])
    return pl.pallas_call(
        functools.partial(kernel_fn, tiles_per_seq=seq_len // tm),
        out_shape=(jax.ShapeDtypeStruct((t, d), F32), jax.ShapeDtypeStruct((t, d // 2), U32)),
        grid=(t // tm,),
        in_specs=in_specs,
        out_specs=(pl.BlockSpec((tm, d), lambda i: (i, 0)), pl.BlockSpec((tm, d // 2), lambda i: (i, 0))),
        scratch_shapes=scratch,
        compiler_params=_cparams("parallel"),
        name=name,
    )(x, *lead, conv_in, conv_in, conv_in, *params, w, gate, g2, shift2, scale2)


def _router_kernel(hp_ref, rw_ref, bias_ref, tri_ref, route_ref, cnt_ref, base_ref):
    tm = hp_ref.shape[0]

    @pl.when(pl.program_id(0) == 0)
    def _():
        base_ref[...] = jnp.zeros_like(base_ref)

    lo, hi = _unpack_halves(hp_ref[...])
    h = jnp.concatenate([lo, hi], axis=1).astype(BF16)
    logits = jnp.dot(h, rw_ref[...], preferred_element_type=F32)
    lt = logits.T[0:N_EXPERTS, :]
    aff = _sigmoid(lt)
    sel = aff + bias_ref[...]
    rows = lambda a, e: a[e:e + 1, :]

    best = None
    for g in range(N_GROUPS):
        a, b, c, d = (rows(sel, EXPERTS_PER_GROUP * g + j) for j in range(EXPERTS_PER_GROUP))
        score = jnp.maximum(jnp.maximum(a + b, c + d), jnp.maximum(a, b) + jnp.maximum(c, d))
        if best is None:
            best, grp = score, jnp.zeros((1, tm), jnp.int32)
        else:
            better = score > best
            grp = jnp.where(better, g, grp)
            best = jnp.where(better, score, best)

    def in_group(a, j):
        out = rows(a, j)
        for g in range(1, N_GROUPS):
            out = jnp.where(grp == g, rows(a, EXPERTS_PER_GROUP * g + j), out)
        return out

    s = [in_group(sel, j) for j in range(EXPERTS_PER_GROUP)]
    f = [in_group(aff, j) for j in range(EXPERTS_PER_GROUP)]

    def first_argmax(vals):
        m = functools.reduce(jnp.maximum, vals)
        idx = jnp.full((1, tm), len(vals) - 1, jnp.int32)
        for j in range(len(vals) - 2, -1, -1):
            idx = jnp.where(vals[j] == m, j, idx)
        return idx

    i1 = first_argmax(s)
    i2 = first_argmax([jnp.where(i1 == j, -jnp.inf, s[j]) for j in range(EXPERTS_PER_GROUP)])
    e_lo = jnp.minimum(i1, i2)
    e_hi = jnp.maximum(i1, i2)
    pick = lambda vals, idx: functools.reduce(
        lambda acc, j: jnp.where(idx == j, vals[j], acc), range(1, len(vals)), vals[0])
    f_lo, f_hi = pick(f, e_lo), pick(f, e_hi)
    denom = f_lo + f_hi
    pair = jnp.where(e_lo == 0, e_hi - 1, jnp.where(e_lo == 1, e_hi + 1, N_PAIRS - 1))
    cls = grp * N_PAIRS + pair

    onehot = (lax.broadcasted_iota(jnp.int32, (CLASS_ROWS, tm), 0) == cls)
    onehot_f = onehot.astype(F32)
    before = jnp.dot(onehot_f.astype(BF16), tri_ref[...], preferred_element_type=F32) + base_ref[:, 0:1]
    rank = jnp.sum(onehot_f * before, axis=0, keepdims=True)
    base_ref[...] = base_ref[...] + jnp.sum(onehot_f, axis=1, keepdims=True)

    route_ref[...] = jnp.concatenate(
        [cls.astype(F32), rank, f_lo / denom, f_hi / denom, jnp.zeros((4, tm), F32)], axis=0)
    cnt_ref[...] = base_ref[...]


def _router(hp, rw, bias, tri, tm):
    t, hw = hp.shape
    return pl.pallas_call(
        _router_kernel,
        out_shape=(jax.ShapeDtypeStruct((8, t), F32), jax.ShapeDtypeStruct((CLASS_ROWS, LANES), F32)),
        grid=(t // tm,),
        in_specs=[pl.BlockSpec((tm, hw), lambda i: (i, 0)),
                  pl.BlockSpec(rw.shape, lambda i: (0, 0)),
                  pl.BlockSpec(bias.shape, lambda i: (0, 0)),
                  pl.BlockSpec((tm, tm), lambda i: (0, 0))],
        out_specs=(pl.BlockSpec((8, tm), lambda i: (0, i)),
                   pl.BlockSpec((CLASS_ROWS, LANES), lambda i: (0, 0))),
        scratch_shapes=[pltpu.VMEM((CLASS_ROWS, LANES), F32)],
        compiler_params=_cparams("arbitrary"),
        name="router",
    )(hp, rw, bias, tri)


def _dispatch_kernel(pos_ref, hp_ref, init_ref, hs_ref, sem):
    del init_ref
    tm = hp_ref.shape[0]
    base = pl.program_id(0) * tm

    for r in range(tm):
        pltpu.async_copy(hp_ref.at[pl.ds(r, 1)], hs_ref.at[pl.ds(pos_ref[base + r], 1)], sem,
                         priority=r % DMA_QUEUES)
    pltpu.make_async_copy(hp_ref, hs_ref.at[pl.ds(0, tm)], sem).wait()


def _dispatch(pos, hp, hs_init, tm):
    t, hw = hp.shape
    n_rows = hs_init.shape[0]
    return pl.pallas_call(
        _dispatch_kernel,
        out_shape=jax.ShapeDtypeStruct((n_rows, hw), U32),
        grid_spec=pltpu.PrefetchScalarGridSpec(
            num_scalar_prefetch=1,
            grid=(t // tm,),
            in_specs=[pl.BlockSpec((tm, hw), lambda i, pos: (i, 0)),
                      pl.BlockSpec(memory_space=pl.ANY)],
            out_specs=pl.BlockSpec(memory_space=pl.ANY),
            scratch_shapes=[pltpu.SemaphoreType.DMA(())]),
        input_output_aliases={2: 0},
        compiler_params=_cparams("arbitrary"),
        name="dispatch",
    )(pos, hp, hs_init)


def _moe_kernel(elo_ref, ehi_ref, nused_ref, hs_ref, wgu_lo, wd_lo, wgu_hi, wd_hi, ys_ref):
    del elo_ref, ehi_ref

    @pl.when(pl.program_id(0) >= nused_ref[0])
    def _():
        ys_ref[...] = jnp.zeros_like(ys_ref)

    @pl.when(pl.program_id(0) < nused_ref[0])
    def _():
        lo, hi = _unpack_halves(hs_ref[...])
        h = jnp.concatenate([lo, hi], axis=1).astype(BF16)

        def expert(wgu_ref, wd_ref):
            gu = jnp.dot(h, wgu_ref[...], preferred_element_type=F32)
            g, u = gu[:, :D_EXPERT], gu[:, D_EXPERT:]
            act = (g * _sigmoid(g) * u).astype(BF16)
            return jnp.dot(act, wd_ref[...], preferred_element_type=F32)

        ya = lax.bitcast_convert_type(expert(wgu_lo, wd_lo).astype(BF16).astype(F32), U32)
        yb = lax.bitcast_convert_type(expert(wgu_hi, wd_hi).astype(BF16).astype(F32), U32)
        ys_ref[...] = _rows_to_token_tiles((ya >> 16) | (yb & jnp.uint32(0xFFFF0000)))


def _rows_to_token_tiles(y):
    n = y.shape[0]
    chunks = [y[:, LANES * s:LANES * (s + 1)].reshape(n // SUBLANES, SUBLANES, LANES) for s in range(SUBLANES)]
    return jnp.swapaxes(jnp.stack(chunks, axis=1), 1, 2).reshape(n, SUBLANES, LANES)


def _token_tiles_to_row_chunks(y3):
    n = y3.shape[0]
    y4 = jnp.swapaxes(y3.reshape(n // SUBLANES, SUBLANES, SUBLANES, LANES), 1, 2)
    return [y4[:, s].reshape(n, LANES) for s in range(SUBLANES)]


def _moe(e_lo, e_hi, n_used, hs, wgu, wd):
    n_rows, hw = hs.shape
    d = wd.shape[2]
    assert d == SUBLANES * LANES
    n_blocks = n_rows // MOE_BLOCK
    row_map = lambda j, lo, hi, nu: (jnp.minimum(j, nu[0] - 1), 0)
    return pl.pallas_call(
        _moe_kernel,
        out_shape=jax.ShapeDtypeStruct((n_rows, SUBLANES, LANES), U32),
        grid_spec=pltpu.PrefetchScalarGridSpec(
            num_scalar_prefetch=3,
            grid=(n_blocks,),
            in_specs=[pl.BlockSpec((MOE_BLOCK, hw), row_map),
                      pl.BlockSpec((None,) + wgu.shape[1:], lambda j, lo, hi, nu: (lo[j], 0, 0)),
                      pl.BlockSpec((None,) + wd.shape[1:], lambda j, lo, hi, nu: (lo[j], 0, 0)),
                      pl.BlockSpec((None,) + wgu.shape[1:], lambda j, lo, hi, nu: (hi[j], 0, 0)),
                      pl.BlockSpec((None,) + wd.shape[1:], lambda j, lo, hi, nu: (hi[j], 0, 0))],
            out_specs=pl.BlockSpec((MOE_BLOCK, SUBLANES, LANES), lambda j, lo, hi, nu: (j, 0, 0))),
        compiler_params=_cparams("arbitrary"),
        name="experts",
    )(e_lo, e_hi, n_used, hs, wgu, wd, wgu, wd)


def _combine_kernel(pos_ref, x_ref, rc_ref, gate_ref, ys_ref, xo_ref, ybuf, sems):
    x_new, issue = _combined_residual(pos_ref, x_ref, rc_ref, gate_ref, ys_ref, ybuf, sems)
    xo_ref[...] = x_new
    issue(0, 1)


def _combine(pos, x, rc, gate, ys, rows_per_mod, tm):
    t, d = x.shape
    return pl.pallas_call(
        _combine_kernel,
        out_shape=jax.ShapeDtypeStruct((t, d), F32),
        grid_spec=pltpu.PrefetchScalarGridSpec(
            num_scalar_prefetch=1,
            grid=(t // tm,),
            in_specs=_combine_in_specs(tm, d, rc.shape[1], rows_per_mod),
            out_specs=pl.BlockSpec((tm, d), lambda i, pos: (i, 0)),
            scratch_shapes=_combine_scratch(tm, d)),
        compiler_params=_cparams("arbitrary"),
        name="combine",
    )(pos, x, rc, gate, ys)


def _deinterleave(base):
    return [base + 2 * p for p in range(HEAD_DIM // 2)] + [base + 2 * p + 1 for p in range(HEAD_DIM // 2)]


def _even_in_perm(n_cols):
    cols = []
    half = N_HEADS // N_KV_HEADS
    for m in range(half):
        cols += _deinterleave(HEAD_DIM * m) + _deinterleave(HEAD_DIM * (m + half))
    for kh in range(N_KV_HEADS):
        cols += _deinterleave(ATTN_WIDTH + HEAD_DIM * kh)
    cols += list(range(QK_WIDTH, n_cols))
    return np.asarray(cols, np.int32)


def _even_out_perm(n_rows):
    rows = []
    half = N_HEADS // N_KV_HEADS
    for m in range(half):
        rows += list(range(HEAD_DIM * m, HEAD_DIM * (m + 1)))
        rows += list(range(HEAD_DIM * (m + half), HEAD_DIM * (m + half + 1)))
    rows += list(range(ATTN_WIDTH, n_rows))
    return np.asarray(rows, np.int32)


def _rope_tables(n_tokens):
    rows = n_tokens // GRID_W
    row = jnp.repeat(jnp.arange(rows, dtype=F32), GRID_W)
    col = jnp.tile(jnp.arange(GRID_W, dtype=F32), rows)
    inv = ROPE_THETA ** (-jnp.arange(ROPE_PAIRS_PER_AXIS, dtype=F32) / ROPE_PAIRS_PER_AXIS)
    ang = jnp.concatenate([row[:, None] * inv, col[:, None] * inv], axis=-1)
    cos, sin = jnp.cos(ang), jnp.sin(ang)
    reps = LANES // HEAD_DIM
    return (jnp.tile(jnp.concatenate([cos, cos], axis=-1), (1, reps)),
            jnp.tile(jnp.concatenate([-sin, sin], axis=-1), (1, reps)))


def _head_mean_matrix():
    blk = np.kron(np.eye(LANES // HEAD_DIM), np.full((HEAD_DIM, HEAD_DIM), 1.0 / HEAD_DIM))
    return jnp.asarray(np.concatenate([blk, blk], axis=0), BF16)


def _moe_layer(tokens_hp, hs_buf, router_wp, router_b, tri, wgu, wd, tm):
    t = tokens_hp.shape[0]
    route, counts = _router(tokens_hp, router_wp, router_b, tri, tm)
    cls = route[0].astype(jnp.int32)
    rank = route[1].astype(jnp.int32)
    cnt = counts[:N_CLASSES, 0].astype(jnp.int32)
    padded = (cnt + MOE_BLOCK - 1) // MOE_BLOCK * MOE_BLOCK
    pad_end = jnp.cumsum(padded)
    pos = (pad_end - padded)[cls] + rank
    n_blocks = hs_buf.shape[0] // MOE_BLOCK
    assert n_blocks >= -(-t // MOE_BLOCK) + N_CLASSES
    n_used = pad_end[-1] // MOE_BLOCK
    blk = jnp.minimum(jnp.arange(n_blocks, dtype=jnp.int32), n_used - 1)
    bcls = jnp.sum((pad_end[None, :] <= (blk * MOE_BLOCK)[:, None]).astype(jnp.int32), axis=1)
    bcls = jnp.minimum(bcls, N_CLASSES - 1)
    e_lo = EXPERTS_PER_GROUP * (bcls // N_PAIRS) + jnp.asarray(PAIR_LO, jnp.int32)[bcls % N_PAIRS]
    e_hi = EXPERTS_PER_GROUP * (bcls // N_PAIRS) + jnp.asarray(PAIR_HI, jnp.int32)[bcls % N_PAIRS]
    hs = _dispatch(pos, tokens_hp, hs_buf, math.gcd(tm, ROW_TILE))
    ys = _moe(e_lo, e_hi, n_used.reshape(1).astype(jnp.int32), hs, wgu, wd)
    return pos, route.T, ys, hs


def kernel(x, c, ctx, c_ctx, w_mod, b_mod, norm_mix_g, norm_ffn_g, ev_w_in, ev_q_norm_g, ev_k_norm_g,
           ev_dw_w, ev_dw_b, ev_ln_g, ev_ln_b, ev_w_out, od_w_in, od_conv_w, od_w_out, router_w,
           router_bias, moe_w_gate, moe_w_up, moe_w_down):
    b, s, d = x.shape
    n_ctx = ctx.shape[1]
    depth = w_mod.shape[0]
    last_attn = (depth - 1) - ((depth - 1) % 2)
    tm = min(ROW_TILE, s)
    tmz = min(ROW_TILE, n_ctx)
    tq = min(Q_TILE, s)
    tqz = min(Q_TILE, n_ctx)
    assert s % tm == 0 and n_ctx % tmz == 0 and s % tq == 0 and n_ctx % tqz == 0 and s % GRID_W == 0

    r = -(-(b + 1) // 8) * 8
    c_all = jnp.zeros((r, d), F32).at[:b].set(c).at[b].set(c_ctx)
    mods = _modulation(c_all, w_mod, b_mod)

    def mod_x(l, j):
        return mods[l, :b, j * d:(j + 1) * d].reshape(b, 1, d)

    def mod_z(l, j):
        return mods[l, b:b + 1, j * d:(j + 1) * d].reshape(1, 1, d)

    cos_t, sin_t = _rope_tables(s)
    gmat = _head_mean_matrix()
    in_perm = _even_in_perm(ev_w_in.shape[2])
    out_perm = _even_out_perm(ev_w_out.shape[1])
    head_perm = np.asarray(_deinterleave(0), np.int32)
    router_wp = jnp.zeros((d, LANES), F32).at[:, :N_EXPERTS].set(router_w).astype(BF16)
    router_b = router_bias.astype(F32).reshape(N_EXPERTS, 1)
    tri_n = min(ROUTER_TILE, b * s)
    tri = jnp.asarray(np.triu(np.ones((tri_n, tri_n), np.float32), 1), BF16)

    xf = x.reshape(b * s, d)
    zf = ctx.reshape(b * n_ctx, d)
    pending = None
    ctm = min(Q_TILE, s, n_ctx)
    n_routed = b * s + (b * n_ctx if last_attn > 0 else 0)
    hs_buf = jnp.zeros(((-(-n_routed // MOE_BLOCK) + N_CLASSES) * MOE_BLOCK, d // 2), U32)
    for l in range(depth):
        i = l // 2
        ctx_in = l <= last_attn
        ctx_out = l < last_attn
        gmix = norm_mix_g[l].reshape(1, d)
        gffn = norm_ffn_g[l].reshape(1, d)
        if l % 2 == 0:
            w_in = ev_w_in[i][:, in_perm].astype(BF16)
            w_out = ev_w_out[i][out_perm, :].astype(BF16)
            gq = ev_q_norm_g[i][head_perm] * (ATTN_SCALE * LOG2E)
            gk = ev_k_norm_g[i][head_perm]
            gqk = jnp.concatenate([jnp.tile(gq, N_HEADS), jnp.tile(gk, N_KV_HEADS)]).reshape(1, QK_WIDTH)
            conv_params = [jnp.repeat(ev_dw_w[i], SUBLANES, axis=0), ev_dw_b[i].reshape(1, -1),
                           ev_ln_g[i].reshape(1, -1),
                           ev_ln_b[i].reshape(1, -1)]
            res = _inproj_even(xf, pending, gmix, mod_x(l, 0), mod_x(l, 1), s, w_in, gqk, cos_t, sin_t, gmat,
                               tm, True)
            if pending is not None:
                xf, res = res[0], res[1:]
            q, k, v, a = res
            k_all = k.reshape(b, s, KV_WIDTH)
            v_all = v.reshape(N_KV_HEADS, b, s, LANES)
            if ctx_in:
                qz, kz, vz, az = _inproj_even(zf, None, gmix, mod_z(l, 0), mod_z(l, 1), b * n_ctx, w_in, gqk,
                                              cos_t, sin_t, gmat, tmz, False)
                kz = kz.reshape(b, n_ctx, KV_WIDTH)
                vz = vz.reshape(N_KV_HEADS, b, n_ctx, LANES)
                k_all = jnp.concatenate([k_all, kz], axis=1)
                v_all = jnp.concatenate([v_all, vz], axis=2)
            attn = _attention(q, k_all, v_all, s, tq)
            xf, hp = _outproj(_outproj_even_kernel, "outproj_even", xf, [attn], a, conv_params, w_out,
                              mod_x(l, 2), gffn, mod_x(l, 3), mod_x(l, 4), s, s, tm, True)
            if ctx_out:
                attn_z = _attention(qz, kz, vz, n_ctx, tqz)
                zf, hpz = _outproj(_outproj_even_kernel, "outproj_even_ctx", zf, [attn_z], az, conv_params,
                                   w_out, mod_z(l, 2), gffn, mod_z(l, 3), mod_z(l, 4), b * n_ctx, n_ctx, tmz,
                                   True)
        else:
            w_in = od_w_in[i].astype(BF16)
            w_out = od_w_out[i].astype(BF16)
            conv_params = [jnp.repeat(od_conv_w[i], SUBLANES, axis=0)]
            res = _inproj_odd(xf, pending, gmix, mod_x(l, 0), mod_x(l, 1), s, w_in, tm)
            if pending is not None:
                xf, res = res[0], res[1:]
            gb, u = res
            xf, hp = _outproj(_outproj_odd_kernel, "outproj_odd", xf, [gb], u, conv_params, w_out,
                              mod_x(l, 2), gffn, mod_x(l, 3), mod_x(l, 4), s, s, tm, False)
            if ctx_out:
                gbz, uz = _inproj_odd(zf, None, gmix, mod_z(l, 0), mod_z(l, 1), b * n_ctx, w_in, tmz)
                zf, hpz = _outproj(_outproj_odd_kernel, "outproj_odd_ctx", zf, [gbz], uz, conv_params, w_out,
                                   mod_z(l, 2), gffn, mod_z(l, 3), mod_z(l, 4), b * n_ctx, n_ctx, tmz, False)

        wgu = jnp.concatenate([moe_w_gate[l], moe_w_up[l]], axis=-1).astype(BF16)
        wd = moe_w_down[l].astype(BF16)
        tokens = jnp.concatenate([hp, hpz], axis=0) if ctx_out else hp
        rtm = math.gcd(tri_n, math.gcd(b * s, b * n_ctx)) if ctx_out else tri_n
        pos, rc, ys, hs_buf = _moe_layer(tokens, hs_buf, router_wp, router_b, tri[:rtm, :rtm], wgu, wd, rtm)
        n_lat = b * s
        pending = (pos[:n_lat], rc[:n_lat], mod_x(l, 5), ys)
        if ctx_out:
            zf = _combine(pos[n_lat:], zf, rc[n_lat:], mod_z(l, 5), ys, b * n_ctx, ctm)
    pos, rc, gate, ys = pending
    return _combine(pos, xf, rc, gate, ys, s, ctm).reshape(b, s, d)
```

```python
import functools
import math

import numpy as np
import jax
import jax.numpy as jnp
from jax import lax
from jax.experimental import pallas as pl
from jax.experimental.pallas import tpu as pltpu

F32 = jnp.float32
BF16 = jnp.bfloat16
U32 = jnp.uint32

GRID_W = 64
EPS = 1e-6
N_HEADS = 8
N_KV_HEADS = 2
GQA_GROUP = N_HEADS // N_KV_HEADS
HEAD_DIM = 64
ATTN_WIDTH = N_HEADS * HEAD_DIM
KV_WIDTH = N_KV_HEADS * HEAD_DIM
QK_WIDTH = ATTN_WIDTH + KV_WIDTH
ATTN_SCALE = HEAD_DIM ** -0.5
ROPE_THETA = 10000.0
ROPE_PAIRS_PER_AXIS = HEAD_DIM // 4
CONV_A_K = 31
SC_K = 3
N_EXPERTS = 16
N_GROUPS = 4
EXPERTS_PER_GROUP = N_EXPERTS // N_GROUPS
D_EXPERT = 512
PAIR_LO = (0, 0, 0, 1, 1, 2)
PAIR_HI = (1, 2, 3, 2, 3, 3)
N_PAIRS = len(PAIR_LO)
N_CLASSES = N_GROUPS * N_PAIRS
CLASS_ROWS = 32

LANES = 128
SUBLANES = 8
MXU_COLS = 256
HALO = 16
VMEM_LIMIT = 48 * 1024 * 1024
ROW_TILE = 512
Q_TILE = 256
ROUTER_TILE = 1024
DMA_QUEUES = 2
MOE_BLOCK = 512
CONV_ROWS = 64
LOG2E = math.log2(math.e)


def _cparams(*sem):
    return pltpu.CompilerParams(dimension_semantics=sem, vmem_limit_bytes=VMEM_LIMIT)


def _sigmoid(x):
    return 1.0 / (1.0 + jnp.exp(-x))


def _rms_mod(x, g, shift, scale):
    y = x * lax.rsqrt(jnp.mean(x * x, axis=-1, keepdims=True) + EPS) * g
    return y * (1.0 + scale) + shift


def _pack_halves(h):
    w = h.shape[1] // 2
    lo = lax.bitcast_convert_type(h[:, :w].astype(BF16).astype(F32), U32)
    hi = lax.bitcast_convert_type(h[:, w:].astype(BF16).astype(F32), U32)
    return (lo >> 16) | (hi & jnp.uint32(0xFFFF0000))


def _unpack_halves(w):
    lo = lax.bitcast_convert_type(w << 16, F32)
    hi = lax.bitcast_convert_type(w & jnp.uint32(0xFFFF0000), F32)
    return lo, hi


def _issue_row_gathers(pos_ref, ys_ref, ybuf, sems, tile, slot, rows):
    tm = ybuf.shape[1]
    for r in rows:
        pltpu.async_copy(ys_ref.at[pl.ds(pos_ref[tile * tm + r], 1)], ybuf.at[slot, pl.ds(r, 1)],
                         sems.at[slot], priority=r % DMA_QUEUES)


def _wait_row_gathers(ys_ref, ybuf, sems, slot):
    pltpu.make_async_copy(ys_ref.at[pl.ds(0, ybuf.shape[1])], ybuf.at[slot], sems.at[slot]).wait()


def _combined_residual(pos_ref, x_ref, rc_ref, gate_ref, ys_ref, ybuf, sems):
    tm = ybuf.shape[1]
    i = pl.program_id(0)
    n = pl.num_programs(0)

    @pl.when(i == 0)
    def _():
        _issue_row_gathers(pos_ref, ys_ref, ybuf, sems, 0, 0, range(tm))

    _wait_row_gathers(ys_ref, ybuf, sems, i % 2)
    ya, yb = _unpack_halves(ybuf[i % 2])
    rc = rc_ref[...].T
    x_new = x_ref[...] + gate_ref[...] * (rc[:, 2:3] * ya + rc[:, 3:4] * yb)

    def issue(k, n_slices):
        _issue_row_gathers(pos_ref, ys_ref, ybuf, sems, jnp.minimum(i + 1, n - 1), (i + 1) % 2,
                           range(k * tm // n_slices, (k + 1) * tm // n_slices))
        if k == n_slices - 1:
            @pl.when(i == n - 1)
            def _():
                _wait_row_gathers(ys_ref, ybuf, sems, (i + 1) % 2)

    return x_new, issue


def _combine_in_specs(tm, d, route_rows, first_tile, rows_per_mod):
    return [pl.BlockSpec((tm, d), lambda i, pos: (i, 0)),
            pl.BlockSpec((route_rows, tm), lambda i, pos: (0, first_tile + i)),
            pl.BlockSpec((None, 1, d), lambda i, pos: ((i * tm) // rows_per_mod, 0, 0)),
            pl.BlockSpec(memory_space=pl.ANY)]


def _combine_scratch(tm, d):
    return [pltpu.VMEM((2, tm, d), U32), pltpu.SemaphoreType.DMA((2,))]


def _mod_kernel(c_ref, w_ref, b_ref, o_ref):
    c = c_ref[...]
    a = (c * _sigmoid(c)).astype(BF16)
    o_ref[...] = jnp.dot(a, w_ref[...].astype(BF16), preferred_element_type=F32) + b_ref[...]


def _modulation(c_all, w_mod, b_mod):
    depth, d, n = w_mod.shape
    r = c_all.shape[0]
    nt = n // d
    return pl.pallas_call(
        _mod_kernel,
        out_shape=jax.ShapeDtypeStruct((depth, r, n), F32),
        grid=(depth, nt),
        in_specs=[pl.BlockSpec((r, d), lambda l, j: (0, 0)),
                  pl.BlockSpec((None, d, d), lambda l, j: (l, 0, j)),
                  pl.BlockSpec((None, 1, d), lambda l, j: (l, 0, j))],
        out_specs=pl.BlockSpec((None, r, d), lambda l, j: (l, 0, j)),
        compiler_params=_cparams("parallel", "parallel"),
        name="modulation",
    )(c_all, w_mod, b_mod.reshape(depth, 1, n))


def _inproj_kernel(*refs, body, n_in, combine):
    if combine:
        pos_ref, x_ref, rc_ref, gate_ref, ys_ref = refs[:5]
        ybuf, sems = refs[-2:]
        rest = refs[5:-2]
        x, issue = _combined_residual(pos_ref, x_ref, rc_ref, gate_ref, ys_ref, ybuf, sems)
        rest[3 + n_in][...] = x
        ins, outs = rest[:3 + n_in], rest[4 + n_in:]
    else:
        x = refs[0][...]
        issue = lambda k, n_slices: None
        ins, outs = refs[1:4 + n_in], refs[4 + n_in:]
    h = _rms_mod(x, ins[0][...], ins[1][...], ins[2][...]).astype(BF16)
    body(h, issue, *ins[3:], *outs)


def _inproj(body, name, x, pending, g, shift, scale, rows_per_mod, extra, extra_specs, out_shape, out_specs,
            tm):
    t, d = x.shape
    mod_map = lambda i, *_: ((i * tm) // rows_per_mod, 0, 0)
    in_specs = [pl.BlockSpec((1, d), lambda i, *_: (0, 0)), pl.BlockSpec((None, 1, d), mod_map),
                pl.BlockSpec((None, 1, d), mod_map)] + list(extra_specs)
    kern = functools.partial(_inproj_kernel, body=body, n_in=len(extra), combine=pending is not None)
    if pending is None:
        return pl.pallas_call(
            kern, out_shape=tuple(out_shape), grid=(t // tm,),
            in_specs=[pl.BlockSpec((tm, d), lambda i: (i, 0))] + in_specs, out_specs=tuple(out_specs),
            compiler_params=_cparams("parallel"), name=name,
        )(x, g, shift, scale, *extra)
    pos, rc, gate, ys = pending
    return pl.pallas_call(
        kern,
        out_shape=(jax.ShapeDtypeStruct((t, d), F32),) + tuple(out_shape),
        grid_spec=pltpu.PrefetchScalarGridSpec(
            num_scalar_prefetch=1, grid=(t // tm,),
            in_specs=_combine_in_specs(tm, d, rc.shape[0], 0, rows_per_mod) + in_specs,
            out_specs=(pl.BlockSpec((tm, d), lambda i, pos: (i, 0)),) + tuple(out_specs),
            scratch_shapes=_combine_scratch(tm, d)),
        compiler_params=_cparams("arbitrary"), name=name + "_combine",
    )(pos, x, rc, gate, ys, g, shift, scale, *extra)


def _inproj_even_body(h, issue, w_ref, gqk_ref, cos_ref, sin_ref, gm_ref, q_ref, k_ref, v_ref, a_ref, *, rope):
    tm = h.shape[0]
    lane = lax.broadcasted_iota(jnp.int32, (tm, LANES), 1)
    first_head = lane < HEAD_DIM
    first_half = (lane % HEAD_DIM) < (HEAD_DIM // 2)
    c0 = QK_WIDTH + KV_WIDTH
    cw = (w_ref.shape[1] - c0) // 2
    issue(0, 1)
    p = jnp.dot(h, w_ref[...], preferred_element_type=F32)
    for m in range(QK_WIDTH // LANES):
        y = p[:, LANES * m:LANES * (m + 1)]
        sq = y * y
        sq_hi = sq.astype(BF16)
        sq_lo = (sq - sq_hi.astype(F32)).astype(BF16)
        ms = jnp.dot(jnp.concatenate([sq_hi, sq_lo], axis=1), gm_ref[...], preferred_element_type=F32)
        y = y * lax.rsqrt(ms + EPS) * gqk_ref[:, LANES * m:LANES * (m + 1)]
        if rope:
            partner = jnp.where(first_half, pltpu.roll(y, LANES - HEAD_DIM // 2, 1),
                                pltpu.roll(y, HEAD_DIM // 2, 1))
            y = y * cos_ref[...] + partner * sin_ref[...]
        if m < ATTN_WIDTH // LANES:
            q_ref[m] = jnp.where(first_head, y, 0.0).astype(BF16)
            q_ref[m + ATTN_WIDTH // LANES] = jnp.where(first_head, 0.0, y).astype(BF16)
        else:
            k_ref[...] = y.astype(BF16)
    v = p[:, QK_WIDTH:c0]
    v_ref[0] = jnp.where(first_head, v, 1.0).astype(BF16)
    v_ref[1] = jnp.where(first_head, 1.0, v).astype(BF16)
    a_ref[...] = (p[:, c0:c0 + cw] * _sigmoid(p[:, c0 + cw:])).astype(BF16)


def _inproj_even(x, pending, g, shift, scale, rows_per_mod, w, gqk, cos_t, sin_t, gmat, tm, rope):
    t, d = x.shape
    n = w.shape[1]
    cw = (n - QK_WIDTH - KV_WIDTH) // 2
    n_pos = cos_t.shape[0] // tm
    return _inproj(
        functools.partial(_inproj_even_body, rope=rope), "inproj_even", x, pending, g, shift, scale,
        rows_per_mod, [w, gqk, cos_t, sin_t, gmat],
        [pl.BlockSpec((d, n), lambda i, *_: (0, 0)),
         pl.BlockSpec((1, QK_WIDTH), lambda i, *_: (0, 0)),
         pl.BlockSpec((tm, LANES), lambda i, *_: (i % n_pos, 0)),
         pl.BlockSpec((tm, LANES), lambda i, *_: (i % n_pos, 0)),
         pl.BlockSpec((2 * LANES, LANES), lambda i, *_: (0, 0))],
        [jax.ShapeDtypeStruct((N_HEADS, t, LANES), BF16), jax.ShapeDtypeStruct((t, KV_WIDTH), BF16),
         jax.ShapeDtypeStruct((N_KV_HEADS, t, LANES), BF16), jax.ShapeDtypeStruct((t, cw), BF16)],
        [pl.BlockSpec((N_HEADS, tm, LANES), lambda i, *_: (0, i, 0)),
         pl.BlockSpec((tm, KV_WIDTH), lambda i, *_: (i, 0)),
         pl.BlockSpec((N_KV_HEADS, tm, LANES), lambda i, *_: (0, i, 0)),
         pl.BlockSpec((tm, cw), lambda i, *_: (i, 0))],
        tm)


def _inproj_odd_body(h, issue, w_ref, gb_ref, u_ref):
    w = w_ref.shape[1] // 3
    n_phase = w // MXU_COLS
    proj = lambda lo: jnp.dot(h, w_ref[:, lo:lo + MXU_COLS], preferred_element_type=F32)
    for c in range(n_phase):
        issue(c, n_phase)
        cols = slice(MXU_COLS * c, MXU_COLS * (c + 1))
        gb_ref[:, cols] = proj(MXU_COLS * c).astype(BF16)
        u_ref[:, cols] = (proj(w + MXU_COLS * c) * proj(2 * w + MXU_COLS * c)).astype(BF16)


def _inproj_odd(x, pending, g, shift, scale, rows_per_mod, w, tm):
    t, d = x.shape
    n = w.shape[1]
    return _inproj(
        _inproj_odd_body, "inproj_odd", x, pending, g, shift, scale, rows_per_mod, [w],
        [pl.BlockSpec((d, n), lambda i, *_: (0, 0))],
        [jax.ShapeDtypeStruct((t, n // 3), BF16), jax.ShapeDtypeStruct((t, n // 3), BF16)],
        [pl.BlockSpec((tm, n // 3), lambda i, *_: (i, 0)), pl.BlockSpec((tm, n // 3), lambda i, *_: (i, 0))],
        tm)


def _attn_kernel(q_ref, k_ref, v_ref, o_ref, acc_ref):
    tq = q_ref.shape[1]
    first = lax.broadcasted_iota(jnp.int32, (tq, LANES), 1) < HEAD_DIM

    def pair(m, carry):
        outs = []
        for g in range(N_KV_HEADS):
            q = q_ref[m + GQA_GROUP * g]
            s = lax.dot_general(q, k_ref[...], (((1,), (1,)), ((), ())), preferred_element_type=F32)
            p = jnp.exp2(s - jnp.max(s, axis=-1, keepdims=True)).astype(BF16)
            o = jnp.dot(p, v_ref[g], preferred_element_type=F32)
            l = o[:, HEAD_DIM:HEAD_DIM + 1] if g == 0 else o[:, 0:1]
            outs.append(o / l)
        acc_ref[m] = jnp.where(first, outs[0], outs[1])
        return carry

    lax.fori_loop(0, GQA_GROUP, pair, 0)
    o_ref[...] = jnp.concatenate([acc_ref[m] for m in range(GQA_GROUP)], axis=1).astype(BF16)


def _attention(q, k_all, v_all, s_q, tq):
    t = q.shape[1]
    b, skv, _ = k_all.shape
    nq = s_q // tq
    return pl.pallas_call(
        _attn_kernel,
        out_shape=jax.ShapeDtypeStruct((t, ATTN_WIDTH), BF16),
        grid=(b, nq),
        in_specs=[pl.BlockSpec((N_HEADS, tq, LANES), lambda bi, i: (0, bi * nq + i, 0)),
                  pl.BlockSpec((None, skv, LANES), lambda bi, i: (bi, 0, 0)),
                  pl.BlockSpec((N_KV_HEADS, None, skv, LANES), lambda bi, i: (0, bi, 0, 0))],
        out_specs=pl.BlockSpec((tq, ATTN_WIDTH), lambda bi, i: (bi * nq + i, 0)),
        scratch_shapes=[pltpu.VMEM((GQA_GROUP, tq, LANES), F32)],
        compiler_params=_cparams("parallel", "parallel"),
        name="attention",
    )(q, k_all, v_all)


def _fill_window(win_ref, cur_ref, prev_ref, next_ref, is_first, is_last):
    tm = cur_ref.shape[0]
    win_ref[0:HALO, :] = jnp.where(is_first, 0.0, prev_ref[...].astype(F32))
    win_ref[HALO:HALO + tm, :] = cur_ref[...].astype(F32)
    win_ref[HALO + tm:2 * HALO + tm, :] = jnp.where(is_last, 0.0, next_ref[...].astype(F32))


def _fill_shifted(win_ref, sh_ref):
    n = sh_ref.shape[1]
    for s in range(1, SUBLANES):
        sh_ref[s - 1] = win_ref[s:s + n, :]


def _dwconv_chunk(win_ref, sh_ref, w_ref, r0, rows, taps):
    off = HALO - taps // 2
    acc = None
    for kk in range(taps):
        q, s = divmod(off + kk, SUBLANES)
        if sh_ref is None:
            src = win_ref[r0 + off + kk:r0 + off + kk + rows, :]
        elif s == 0:
            src = win_ref[r0 + SUBLANES * q:r0 + SUBLANES * q + rows, :]
        else:
            src = sh_ref[s - 1, r0 + SUBLANES * q:r0 + SUBLANES * q + rows, :]
        term = src.reshape(rows // SUBLANES, SUBLANES, -1) * w_ref[SUBLANES * kk:SUBLANES * (kk + 1), :]
        acc = term if acc is None else acc + term
    return acc.reshape(rows, -1)


def _residual_tail(x_ref, o, gate_ref, g2_ref, sh2_ref, sc2_ref, xo_ref, hp_ref):
    x_new = x_ref[...] + gate_ref[...] * o
    xo_ref[...] = x_new
    hp_ref[...] = _pack_halves(_rms_mod(x_new, g2_ref[...], sh2_ref[...], sc2_ref[...]))


def _outproj_even_kernel(x_ref, at_ref, a_ref, ap_ref, an_ref, dw_ref, db_ref, lg_ref, lb_ref, w_ref,
                         gate_ref, g2_ref, sh2_ref, sc2_ref, xo_ref, hp_ref, win_ref, cv_ref, shw_ref,
                         *, tiles_per_seq):
    tm = x_ref.shape[0]
    i = pl.program_id(0) % tiles_per_seq
    _fill_window(win_ref, a_ref, ap_ref, an_ref, i == 0, i == tiles_per_seq - 1)
    _fill_shifted(win_ref, shw_ref)
    for r0 in range(0, tm, CONV_ROWS):
        y = _dwconv_chunk(win_ref, shw_ref, dw_ref, r0, CONV_ROWS, CONV_A_K) + db_ref[...]
        mu = jnp.mean(y, axis=-1, keepdims=True)
        yc = y - mu
        var = jnp.mean(yc * yc, axis=-1, keepdims=True)
        yn = yc * lax.rsqrt(var + EPS) * lg_ref[...] + lb_ref[...]
        cv_ref[r0:r0 + CONV_ROWS, :] = (yn * _sigmoid(yn)).astype(BF16)
    aw = at_ref.shape[1]
    o = (jnp.dot(at_ref[...], w_ref[0:aw, :], preferred_element_type=F32)
         + jnp.dot(cv_ref[...], w_ref[aw:, :], preferred_element_type=F32))
    _residual_tail(x_ref, o, gate_ref, g2_ref, sh2_ref, sc2_ref, xo_ref, hp_ref)


def _outproj_odd_kernel(x_ref, gb_ref, u_ref, up_ref, un_ref, cw_ref, w_ref,
                        gate_ref, g2_ref, sh2_ref, sc2_ref, xo_ref, hp_ref, win_ref, cv_ref,
                        *, tiles_per_seq):
    tm = x_ref.shape[0]
    i = pl.program_id(0) % tiles_per_seq
    _fill_window(win_ref, u_ref, up_ref, un_ref, i == 0, i == tiles_per_seq - 1)
    for r0 in range(0, tm, CONV_ROWS):
        y = _dwconv_chunk(win_ref, None, cw_ref, r0, CONV_ROWS, SC_K)
        cv_ref[r0:r0 + CONV_ROWS, :] = (gb_ref[r0:r0 + CONV_ROWS, :].astype(F32) * y).astype(BF16)
    o = jnp.dot(cv_ref[...], w_ref[...], preferred_element_type=F32)
    _residual_tail(x_ref, o, gate_ref, g2_ref, sh2_ref, sc2_ref, xo_ref, hp_ref)


def _halo_specs(tm, width, n_rows):
    hb = tm // HALO
    last = n_rows // HALO - 1
    return [pl.BlockSpec((tm, width), lambda i: (i, 0)),
            pl.BlockSpec((HALO, width), lambda i: (jnp.maximum(i * hb - 1, 0), 0)),
            pl.BlockSpec((HALO, width), lambda i: (jnp.minimum((i + 1) * hb, last), 0))]


def _outproj(kernel_fn, name, x, lead, conv_in, params, w, gate, g2, shift2, scale2, rows_per_mod,
             seq_len, tm, shifted_window):
    t, d = x.shape
    cw = conv_in.shape[1]
    scratch = [pltpu.VMEM((tm + 2 * HALO, cw), F32), pltpu.VMEM((tm, cw), BF16)]
    if shifted_window:
        scratch.append(pltpu.VMEM((SUBLANES - 1, tm + 2 * HALO - SUBLANES, cw), F32))
    mod_map = lambda i: ((i * tm) // rows_per_mod, 0, 0)
    row_spec = lambda a: pl.BlockSpec((tm, a.shape[1]), lambda i: (i, 0))
    full_spec = lambda a: pl.BlockSpec(a.shape, lambda i: (0,) * a.ndim)
    in_specs = ([row_spec(x)] + [row_spec(a) for a in lead] + _halo_specs(tm, cw, t)
                + [full_spec(a) for a in params] + [full_spec(w)]
                + [pl.BlockSpec((None, 1, d), mod_map), full_spec(g2),
                   pl.BlockSpec((None, 1, d), mod_map), pl.BlockSpec((None, 1, d), mod_map)])
    return pl.pallas_call(
        functools.partial(kernel_fn, tiles_per_seq=seq_len // tm),
        out_shape=(jax.ShapeDtypeStruct((t, d), F32), jax.ShapeDtypeStruct((t, d // 2), U32)),
        grid=(t // tm,),
        in_specs=in_specs,
        out_specs=(pl.BlockSpec((tm, d), lambda i: (i, 0)), pl.BlockSpec((tm, d // 2), lambda i: (i, 0))),
        scratch_shapes=scratch,
        compiler_params=_cparams("parallel"),
        name=name,
    )(x, *lead, conv_in, conv_in, conv_in, *params, w, gate, g2, shift2, scale2)


def _part_tile_maps(part_tiles):
    maps, start = [], 0
    for n in part_tiles:
        maps.append(lambda i, *_, start=start, n=n: (jnp.clip(i - start, 0, n - 1), 0))
        start += n
    return maps


def _router_kernel(*refs, part_tiles):
    hp_refs = refs[:len(part_tiles)]
    rw_ref, bias_ref, tri_ref, route_ref, cnt_ref, base_ref = refs[len(part_tiles):]
    tm = hp_refs[0].shape[0]

    @pl.when(pl.program_id(0) == 0)
    def _():
        base_ref[...] = jnp.zeros_like(base_ref)

    hp, start = hp_refs[0][...], 0
    for k in range(1, len(part_tiles)):
        start += part_tiles[k - 1]
        hp = jnp.where(pl.program_id(0) >= start, hp_refs[k][...], hp)
    lo, hi = _unpack_halves(hp)
    h = jnp.concatenate([lo, hi], axis=1).astype(BF16)
    logits = jnp.dot(h, rw_ref[...], preferred_element_type=F32)
    lt = logits.T[0:N_EXPERTS, :]
    aff = _sigmoid(lt)
    sel = aff + bias_ref[...]
    rows = lambda a, e: a[e:e + 1, :]

    best = None
    for g in range(N_GROUPS):
        a, b, c, d = (rows(sel, EXPERTS_PER_GROUP * g + j) for j in range(EXPERTS_PER_GROUP))
        score = jnp.maximum(jnp.maximum(a + b, c + d), jnp.maximum(a, b) + jnp.maximum(c, d))
        if best is None:
            best, grp = score, jnp.zeros((1, tm), jnp.int32)
        else:
            better = score > best
            grp = jnp.where(better, g, grp)
            best = jnp.where(better, score, best)

    def in_group(a, j):
        out = rows(a, j)
        for g in range(1, N_GROUPS):
            out = jnp.where(grp == g, rows(a, EXPERTS_PER_GROUP * g + j), out)
        return out

    s = [in_group(sel, j) for j in range(EXPERTS_PER_GROUP)]
    f = [in_group(aff, j) for j in range(EXPERTS_PER_GROUP)]

    def first_argmax(vals):
        m = functools.reduce(jnp.maximum, vals)
        idx = jnp.full((1, tm), len(vals) - 1, jnp.int32)
        for j in range(len(vals) - 2, -1, -1):
            idx = jnp.where(vals[j] == m, j, idx)
        return idx

    i1 = first_argmax(s)
    i2 = first_argmax([jnp.where(i1 == j, -jnp.inf, s[j]) for j in range(EXPERTS_PER_GROUP)])
    e_lo = jnp.minimum(i1, i2)
    e_hi = jnp.maximum(i1, i2)
    pick = lambda vals, idx: functools.reduce(
        lambda acc, j: jnp.where(idx == j, vals[j], acc), range(1, len(vals)), vals[0])
    f_lo, f_hi = pick(f, e_lo), pick(f, e_hi)
    denom = f_lo + f_hi
    pair = jnp.where(e_lo == 0, e_hi - 1, jnp.where(e_lo == 1, e_hi + 1, N_PAIRS - 1))
    cls = grp * N_PAIRS + pair

    onehot = (lax.broadcasted_iota(jnp.int32, (CLASS_ROWS, tm), 0) == cls)
    onehot_f = onehot.astype(F32)
    before = jnp.dot(onehot_f.astype(BF16), tri_ref[...], preferred_element_type=F32) + base_ref[:, 0:1]
    rank = jnp.sum(onehot_f * before, axis=0, keepdims=True)
    base_ref[...] = base_ref[...] + jnp.sum(onehot_f, axis=1, keepdims=True)

    route_ref[...] = jnp.concatenate(
        [cls.astype(F32), rank, f_lo / denom, f_hi / denom, jnp.zeros((4, tm), F32)], axis=0)
    cnt_ref[...] = base_ref[...]


def _router(parts, rw, bias, tri, tm):
    hw = parts[0].shape[1]
    part_tiles = tuple(p.shape[0] // tm for p in parts)
    t = tm * sum(part_tiles)
    return pl.pallas_call(
        functools.partial(_router_kernel, part_tiles=part_tiles),
        out_shape=(jax.ShapeDtypeStruct((8, t), F32), jax.ShapeDtypeStruct((CLASS_ROWS, LANES), F32)),
        grid=(t // tm,),
        in_specs=[pl.BlockSpec((tm, hw), m) for m in _part_tile_maps(part_tiles)] + [
                  pl.BlockSpec(rw.shape, lambda i: (0, 0)),
                  pl.BlockSpec(bias.shape, lambda i: (0, 0)),
                  pl.BlockSpec((tm, tm), lambda i: (0, 0))],
        out_specs=(pl.BlockSpec((8, tm), lambda i: (0, i)),
                   pl.BlockSpec((CLASS_ROWS, LANES), lambda i: (0, 0))),
        scratch_shapes=[pltpu.VMEM((CLASS_ROWS, LANES), F32)],
        compiler_params=_cparams("arbitrary"),
        name="router",
    )(*parts, rw, bias, tri)


def _dispatch_kernel(pos_ref, *refs, part_tiles):
    hp_refs = refs[:len(part_tiles)]
    init_ref, hs_ref, sem = refs[len(part_tiles):]
    del init_ref
    tm = hp_refs[0].shape[0]
    i = pl.program_id(0)

    start = 0
    for hp_ref, n in zip(hp_refs, part_tiles):
        @pl.when((i >= start) & (i < start + n))
        def _(hp_ref=hp_ref):
            for r in range(tm):
                pltpu.async_copy(hp_ref.at[pl.ds(r, 1)], hs_ref.at[pl.ds(pos_ref[i * tm + r], 1)], sem,
                                 priority=r % DMA_QUEUES)
        start += n
    pltpu.make_async_copy(hp_refs[0], hs_ref.at[pl.ds(0, tm)], sem).wait()


def _dispatch(pos, parts, hs_init, tm):
    hw = parts[0].shape[1]
    part_tiles = tuple(p.shape[0] // tm for p in parts)
    n_rows = hs_init.shape[0]
    return pl.pallas_call(
        functools.partial(_dispatch_kernel, part_tiles=part_tiles),
        out_shape=jax.ShapeDtypeStruct((n_rows, hw), U32),
        grid_spec=pltpu.PrefetchScalarGridSpec(
            num_scalar_prefetch=1,
            grid=(sum(part_tiles),),
            in_specs=[pl.BlockSpec((tm, hw), m) for m in _part_tile_maps(part_tiles)]
            + [pl.BlockSpec(memory_space=pl.ANY)],
            out_specs=pl.BlockSpec(memory_space=pl.ANY),
            scratch_shapes=[pltpu.SemaphoreType.DMA(())]),
        input_output_aliases={1 + len(parts): 0},
        compiler_params=_cparams("arbitrary"),
        name="dispatch",
    )(pos, *parts, hs_init)


def _moe_kernel(elo_ref, ehi_ref, nused_ref, hs_ref, wgu_lo, wd_lo, wgu_hi, wd_hi, ys_ref):
    del elo_ref, ehi_ref

    @pl.when(pl.program_id(0) >= nused_ref[0])
    def _():
        ys_ref[...] = jnp.zeros_like(ys_ref)

    @pl.when(pl.program_id(0) < nused_ref[0])
    def _():
        lo, hi = _unpack_halves(hs_ref[...])
        h = jnp.concatenate([lo, hi], axis=1).astype(BF16)

        def expert(wgu_ref, wd_ref):
            gu = jnp.dot(h, wgu_ref[...], preferred_element_type=F32)
            g, u = gu[:, :D_EXPERT], gu[:, D_EXPERT:]
            act = (g * _sigmoid(g) * u).astype(BF16)
            return jnp.dot(act, wd_ref[...], preferred_element_type=F32)

        ya = lax.bitcast_convert_type(expert(wgu_lo, wd_lo).astype(BF16).astype(F32), U32)
        yb = lax.bitcast_convert_type(expert(wgu_hi, wd_hi).astype(BF16).astype(F32), U32)
        ys_ref[...] = (ya >> 16) | (yb & jnp.uint32(0xFFFF0000))


def _moe(e_lo, e_hi, n_used, hs, wgu, wd):
    n_rows, hw = hs.shape
    d = wd.shape[2]
    n_blocks = n_rows // MOE_BLOCK
    row_map = lambda j, lo, hi, nu: (jnp.minimum(j, nu[0] - 1), 0)
    return pl.pallas_call(
        _moe_kernel,
        out_shape=jax.ShapeDtypeStruct((n_rows, d), U32),
        grid_spec=pltpu.PrefetchScalarGridSpec(
            num_scalar_prefetch=3,
            grid=(n_blocks,),
            in_specs=[pl.BlockSpec((MOE_BLOCK, hw), row_map),
                      pl.BlockSpec((None,) + wgu.shape[1:], lambda j, lo, hi, nu: (lo[j], 0, 0)),
                      pl.BlockSpec((None,) + wd.shape[1:], lambda j, lo, hi, nu: (lo[j], 0, 0)),
                      pl.BlockSpec((None,) + wgu.shape[1:], lambda j, lo, hi, nu: (hi[j], 0, 0)),
                      pl.BlockSpec((None,) + wd.shape[1:], lambda j, lo, hi, nu: (hi[j], 0, 0))],
            out_specs=pl.BlockSpec((MOE_BLOCK, d), lambda j, lo, hi, nu: (j, 0))),
        compiler_params=_cparams("arbitrary"),
        name="experts",
    )(e_lo, e_hi, n_used, hs, wgu, wd, wgu, wd)


def _combine_kernel(pos_ref, x_ref, rc_ref, gate_ref, ys_ref, xo_ref, ybuf, sems):
    x_new, issue = _combined_residual(pos_ref, x_ref, rc_ref, gate_ref, ys_ref, ybuf, sems)
    xo_ref[...] = x_new
    issue(0, 1)


def _combine(pos, x, rc, first_tile, gate, ys, rows_per_mod, tm):
    t, d = x.shape
    return pl.pallas_call(
        _combine_kernel,
        out_shape=jax.ShapeDtypeStruct((t, d), F32),
        grid_spec=pltpu.PrefetchScalarGridSpec(
            num_scalar_prefetch=1,
            grid=(t // tm,),
            in_specs=_combine_in_specs(tm, d, rc.shape[0], first_tile, rows_per_mod),
            out_specs=pl.BlockSpec((tm, d), lambda i, pos: (i, 0)),
            scratch_shapes=_combine_scratch(tm, d)),
        compiler_params=_cparams("arbitrary"),
        name="combine",
    )(pos, x, rc, gate, ys)


def _deinterleave(base):
    return [base + 2 * p for p in range(HEAD_DIM // 2)] + [base + 2 * p + 1 for p in range(HEAD_DIM // 2)]


def _even_in_perm(n_cols):
    cols = []
    half = N_HEADS // N_KV_HEADS
    for m in range(half):
        cols += _deinterleave(HEAD_DIM * m) + _deinterleave(HEAD_DIM * (m + half))
    for kh in range(N_KV_HEADS):
        cols += _deinterleave(ATTN_WIDTH + HEAD_DIM * kh)
    cols += list(range(QK_WIDTH, n_cols))
    return np.asarray(cols, np.int32)


def _even_out_perm(n_rows):
    rows = []
    half = N_HEADS // N_KV_HEADS
    for m in range(half):
        rows += list(range(HEAD_DIM * m, HEAD_DIM * (m + 1)))
        rows += list(range(HEAD_DIM * (m + half), HEAD_DIM * (m + half + 1)))
    rows += list(range(ATTN_WIDTH, n_rows))
    return np.asarray(rows, np.int32)


def _rope_tables(n_tokens):
    rows = n_tokens // GRID_W
    row = jnp.repeat(jnp.arange(rows, dtype=F32), GRID_W)
    col = jnp.tile(jnp.arange(GRID_W, dtype=F32), rows)
    inv = ROPE_THETA ** (-jnp.arange(ROPE_PAIRS_PER_AXIS, dtype=F32) / ROPE_PAIRS_PER_AXIS)
    ang = jnp.concatenate([row[:, None] * inv, col[:, None] * inv], axis=-1)
    cos, sin = jnp.cos(ang), jnp.sin(ang)
    reps = LANES // HEAD_DIM
    return (jnp.tile(jnp.concatenate([cos, cos], axis=-1), (1, reps)),
            jnp.tile(jnp.concatenate([-sin, sin], axis=-1), (1, reps)))


def _head_mean_matrix():
    blk = np.kron(np.eye(LANES // HEAD_DIM), np.full((HEAD_DIM, HEAD_DIM), 1.0 / HEAD_DIM))
    return jnp.asarray(np.concatenate([blk, blk], axis=0), BF16)


def _moe_layer(parts, hs_buf, router_wp, router_b, tri, wgu, wd, tm):
    t = sum(p.shape[0] for p in parts)
    route, counts = _router(parts, router_wp, router_b, tri, tm)
    cls = route[0].astype(jnp.int32)
    rank = route[1].astype(jnp.int32)
    cnt = counts[:N_CLASSES, 0].astype(jnp.int32)
    padded = (cnt + MOE_BLOCK - 1) // MOE_BLOCK * MOE_BLOCK
    pad_end = jnp.cumsum(padded)
    pos = (pad_end - padded)[cls] + rank
    n_blocks = hs_buf.shape[0] // MOE_BLOCK
    assert n_blocks >= -(-t // MOE_BLOCK) + N_CLASSES
    n_used = pad_end[-1] // MOE_BLOCK
    blk = jnp.minimum(jnp.arange(n_blocks, dtype=jnp.int32), n_used - 1)
    bcls = jnp.sum((pad_end[None, :] <= (blk * MOE_BLOCK)[:, None]).astype(jnp.int32), axis=1)
    bcls = jnp.minimum(bcls, N_CLASSES - 1)
    e_lo = EXPERTS_PER_GROUP * (bcls // N_PAIRS) + jnp.asarray(PAIR_LO, jnp.int32)[bcls % N_PAIRS]
    e_hi = EXPERTS_PER_GROUP * (bcls // N_PAIRS) + jnp.asarray(PAIR_HI, jnp.int32)[bcls % N_PAIRS]
    hs = _dispatch(pos, parts, hs_buf, math.gcd(tm, ROW_TILE))
    ys = _moe(e_lo, e_hi, n_used.reshape(1).astype(jnp.int32), hs, wgu, wd)
    return pos, route, ys, hs


def kernel(x, c, ctx, c_ctx, w_mod, b_mod, norm_mix_g, norm_ffn_g, ev_w_in, ev_q_norm_g, ev_k_norm_g,
           ev_dw_w, ev_dw_b, ev_ln_g, ev_ln_b, ev_w_out, od_w_in, od_conv_w, od_w_out, router_w,
           router_bias, moe_w_gate, moe_w_up, moe_w_down):
    b, s, d = x.shape
    n_ctx = ctx.shape[1]
    depth = w_mod.shape[0]
    last_attn = (depth - 1) - ((depth - 1) % 2)
    tm = min(ROW_TILE, s)
    tmz = min(ROW_TILE, n_ctx)
    tq = min(Q_TILE, s)
    tqz = min(Q_TILE, n_ctx)
    assert s % tm == 0 and n_ctx % tmz == 0 and s % tq == 0 and n_ctx % tqz == 0 and s % GRID_W == 0

    r = -(-(b + 1) // 8) * 8
    c_all = jnp.zeros((r, d), F32).at[:b].set(c).at[b].set(c_ctx)
    mods = _modulation(c_all, w_mod, b_mod)

    def mod_x(l, j):
        return mods[l, :b, j * d:(j + 1) * d].reshape(b, 1, d)

    def mod_z(l, j):
        return mods[l, b:b + 1, j * d:(j + 1) * d].reshape(1, 1, d)

    cos_t, sin_t = _rope_tables(s)
    gmat = _head_mean_matrix()
    in_perm = _even_in_perm(ev_w_in.shape[2])
    out_perm = _even_out_perm(ev_w_out.shape[1])
    head_perm = np.asarray(_deinterleave(0), np.int32)
    router_wp = jnp.zeros((d, LANES), F32).at[:, :N_EXPERTS].set(router_w).astype(BF16)
    router_b = router_bias.astype(F32).reshape(N_EXPERTS, 1)
    tri_n = min(ROUTER_TILE, b * s)
    tri = jnp.asarray(np.triu(np.ones((tri_n, tri_n), np.float32), 1), BF16)

    xf = x.reshape(b * s, d)
    zf = ctx.reshape(b * n_ctx, d)
    pending = None
    ctm = min(Q_TILE, s, n_ctx)
    n_routed = b * s + (b * n_ctx if last_attn > 0 else 0)
    hs_buf = jnp.zeros(((-(-n_routed // MOE_BLOCK) + N_CLASSES) * MOE_BLOCK, d // 2), U32)
    for l in range(depth):
        i = l // 2
        ctx_in = l <= last_attn
        ctx_out = l < last_attn
        gmix = norm_mix_g[l].reshape(1, d)
        gffn = norm_ffn_g[l].reshape(1, d)
        if l % 2 == 0:
            w_in = ev_w_in[i][:, in_perm].astype(BF16)
            w_out = ev_w_out[i][out_perm, :].astype(BF16)
            gq = ev_q_norm_g[i][head_perm] * (ATTN_SCALE * LOG2E)
            gk = ev_k_norm_g[i][head_perm]
            gqk = jnp.concatenate([jnp.tile(gq, N_HEADS), jnp.tile(gk, N_KV_HEADS)]).reshape(1, QK_WIDTH)
            conv_params = [jnp.repeat(ev_dw_w[i], SUBLANES, axis=0), ev_dw_b[i].reshape(1, -1),
                           ev_ln_g[i].reshape(1, -1),
                           ev_ln_b[i].reshape(1, -1)]
            res = _inproj_even(xf, pending, gmix, mod_x(l, 0), mod_x(l, 1), s, w_in, gqk, cos_t, sin_t, gmat,
                               tm, True)
            if pending is not None:
                xf, res = res[0], res[1:]
            q, k, v, a = res
            k_all = k.reshape(b, s, KV_WIDTH)
            v_all = v.reshape(N_KV_HEADS, b, s, LANES)
            if ctx_in:
                qz, kz, vz, az = _inproj_even(zf, None, gmix, mod_z(l, 0), mod_z(l, 1), b * n_ctx, w_in, gqk,
                                              cos_t, sin_t, gmat, tmz, False)
                kz = kz.reshape(b, n_ctx, KV_WIDTH)
                vz = vz.reshape(N_KV_HEADS, b, n_ctx, LANES)
                k_all = jnp.concatenate([k_all, kz], axis=1)
                v_all = jnp.concatenate([v_all, vz], axis=2)
            attn = _attention(q, k_all, v_all, s, tq)
            xf, hp = _outproj(_outproj_even_kernel, "outproj_even", xf, [attn], a, conv_params, w_out,
                              mod_x(l, 2), gffn, mod_x(l, 3), mod_x(l, 4), s, s, tm, True)
            if ctx_out:
                attn_z = _attention(qz, kz, vz, n_ctx, tqz)
                zf, hpz = _outproj(_outproj_even_kernel, "outproj_even_ctx", zf, [attn_z], az, conv_params,
                                   w_out, mod_z(l, 2), gffn, mod_z(l, 3), mod_z(l, 4), b * n_ctx, n_ctx, tmz,
                                   True)
        else:
            w_in = od_w_in[i].astype(BF16)
            w_out = od_w_out[i].astype(BF16)
            conv_params = [jnp.repeat(od_conv_w[i], SUBLANES, axis=0)]
            res = _inproj_odd(xf, pending, gmix, mod_x(l, 0), mod_x(l, 1), s, w_in, tm)
            if pending is not None:
                xf, res = res[0], res[1:]
            gb, u = res
            xf, hp = _outproj(_outproj_odd_kernel, "outproj_odd", xf, [gb], u, conv_params, w_out,
                              mod_x(l, 2), gffn, mod_x(l, 3), mod_x(l, 4), s, s, tm, False)
            if ctx_out:
                gbz, uz = _inproj_odd(zf, None, gmix, mod_z(l, 0), mod_z(l, 1), b * n_ctx, w_in, tmz)
                zf, hpz = _outproj(_outproj_odd_kernel, "outproj_odd_ctx", zf, [gbz], uz, conv_params, w_out,
                                   mod_z(l, 2), gffn, mod_z(l, 3), mod_z(l, 4), b * n_ctx, n_ctx, tmz, False)

        wgu = jnp.concatenate([moe_w_gate[l], moe_w_up[l]], axis=-1).astype(BF16)
        wd = moe_w_down[l].astype(BF16)
        parts = [hp, hpz] if ctx_out else [hp]
        rtm = math.gcd(tri_n, math.gcd(b * s, b * n_ctx)) if ctx_out else tri_n
        pos, rc, ys, hs_buf = _moe_layer(parts, hs_buf, router_wp, router_b, tri[:rtm, :rtm], wgu, wd, rtm)
        n_lat = b * s
        pending = (pos[:n_lat], rc, mod_x(l, 5), ys)
        if ctx_out:
            zf = _combine(pos[n_lat:], zf, rc, n_lat // ctm, mod_z(l, 5), ys, b * n_ctx, ctm)
    pos, rc, gate, ys = pending
    return _combine(pos, xf, rc, 0, gate, ys, s, ctm).reshape(b, s, d)
```

```python
import functools
import math

import numpy as np
import jax
import jax.numpy as jnp
from jax import lax
from jax.experimental import pallas as pl
from jax.experimental.pallas import tpu as pltpu

F32 = jnp.float32
BF16 = jnp.bfloat16
U32 = jnp.uint32

GRID_W = 64
EPS = 1e-6
N_HEADS = 8
N_KV_HEADS = 2
GQA_GROUP = N_HEADS // N_KV_HEADS
HEAD_DIM = 64
ATTN_WIDTH = N_HEADS * HEAD_DIM
KV_WIDTH = N_KV_HEADS * HEAD_DIM
QK_WIDTH = ATTN_WIDTH + KV_WIDTH
ATTN_SCALE = HEAD_DIM ** -0.5
ROPE_THETA = 10000.0
ROPE_PAIRS_PER_AXIS = HEAD_DIM // 4
CONV_A_K = 31
SC_K = 3
N_EXPERTS = 16
N_GROUPS = 4
EXPERTS_PER_GROUP = N_EXPERTS // N_GROUPS
D_EXPERT = 512
PAIR_LO = (0, 0, 0, 1, 1, 2)
PAIR_HI = (1, 2, 3, 2, 3, 3)
N_PAIRS = len(PAIR_LO)
N_CLASSES = N_GROUPS * N_PAIRS
CLASS_ROWS = 32

LANES = 128
SUBLANES = 8
MXU_COLS = 256
HALO = 16
VMEM_LIMIT = 48 * 1024 * 1024
ROW_TILE = 512
Q_TILE = 256
ROUTER_TILE = 1024
DMA_QUEUES = 2
MOE_BLOCK = 512
CONV_ROWS = 64
LOG2E = math.log2(math.e)


def _cparams(*sem):
    return pltpu.CompilerParams(dimension_semantics=sem, vmem_limit_bytes=VMEM_LIMIT)


def _sigmoid(x):
    return 1.0 / (1.0 + jnp.exp(-x))


def _rms_mod(x, g, shift, scale):
    y = x * lax.rsqrt(jnp.mean(x * x, axis=-1, keepdims=True) + EPS) * g
    return y * (1.0 + scale) + shift


def _pack_halves(h):
    w = h.shape[1] // 2
    lo = lax.bitcast_convert_type(h[:, :w].astype(BF16).astype(F32), U32)
    hi = lax.bitcast_convert_type(h[:, w:].astype(BF16).astype(F32), U32)
    return (lo >> 16) | (hi & jnp.uint32(0xFFFF0000))


def _unpack_halves(w):
    lo = lax.bitcast_convert_type(w << 16, F32)
    hi = lax.bitcast_convert_type(w & jnp.uint32(0xFFFF0000), F32)
    return lo, hi


def _issue_row_gathers(pos_ref, ys_ref, ybuf, sems, tile, slot, rows):
    tm = ybuf.shape[1]
    for r in rows:
        pltpu.async_copy(ys_ref.at[pl.ds(pos_ref[tile * tm + r], 1)], ybuf.at[slot, pl.ds(r, 1)],
                         sems.at[slot], priority=r % DMA_QUEUES)


def _wait_row_gathers(ys_ref, ybuf, sems, slot):
    pltpu.make_async_copy(ys_ref.at[pl.ds(0, ybuf.shape[1])], ybuf.at[slot], sems.at[slot]).wait()


def _combined_residual(pos_ref, x_ref, rc_ref, gate_ref, ys_ref, ybuf, sems):
    tm = ybuf.shape[1]
    i = pl.program_id(0)
    n = pl.num_programs(0)

    @pl.when(i == 0)
    def _():
        _issue_row_gathers(pos_ref, ys_ref, ybuf, sems, 0, 0, range(tm))

    _wait_row_gathers(ys_ref, ybuf, sems, i % 2)
    ya, yb = _unpack_halves(ybuf[i % 2])
    rc = rc_ref[...].T
    x_new = x_ref[...] + gate_ref[...] * (rc[:, 2:3] * ya + rc[:, 3:4] * yb)

    def issue(k, n_slices):
        _issue_row_gathers(pos_ref, ys_ref, ybuf, sems, jnp.minimum(i + 1, n - 1), (i + 1) % 2,
                           range(k * tm // n_slices, (k + 1) * tm // n_slices))
        if k == n_slices - 1:
            @pl.when(i == n - 1)
            def _():
                _wait_row_gathers(ys_ref, ybuf, sems, (i + 1) % 2)

    return x_new, issue


def _combine_in_specs(tm, d, route_rows, first_tile, rows_per_mod):
    return [pl.BlockSpec((tm, d), lambda i, pos: (i, 0)),
            pl.BlockSpec((route_rows, tm), lambda i, pos: (0, first_tile + i)),
            pl.BlockSpec((None, 1, d), lambda i, pos: ((i * tm) // rows_per_mod, 0, 0)),
            pl.BlockSpec(memory_space=pl.ANY)]


def _combine_scratch(tm, d):
    return [pltpu.VMEM((2, tm, d), U32), pltpu.SemaphoreType.DMA((2,))]


def _mod_kernel(c_ref, w_ref, b_ref, o_ref):
    c = c_ref[...]
    a = (c * _sigmoid(c)).astype(BF16)
    o_ref[...] = jnp.dot(a, w_ref[...].astype(BF16), preferred_element_type=F32) + b_ref[...]


def _modulation(c_all, w_mod, b_mod):
    depth, d, n = w_mod.shape
    r = c_all.shape[0]
    nt = n // d
    return pl.pallas_call(
        _mod_kernel,
        out_shape=jax.ShapeDtypeStruct((depth, r, n), F32),
        grid=(depth, nt),
        in_specs=[pl.BlockSpec((r, d), lambda l, j: (0, 0)),
                  pl.BlockSpec((None, d, d), lambda l, j: (l, 0, j)),
                  pl.BlockSpec((None, 1, d), lambda l, j: (l, 0, j))],
        out_specs=pl.BlockSpec((None, r, d), lambda l, j: (l, 0, j)),
        compiler_params=_cparams("parallel", "parallel"),
        name="modulation",
    )(c_all, w_mod, b_mod.reshape(depth, 1, n))


def _inproj_kernel(*refs, body, n_in, combine):
    if combine:
        pos_ref, x_ref, rc_ref, gate_ref, ys_ref = refs[:5]
        ybuf, sems = refs[-2:]
        rest = refs[5:-2]
        x, issue = _combined_residual(pos_ref, x_ref, rc_ref, gate_ref, ys_ref, ybuf, sems)
        rest[3 + n_in][...] = x
        ins, outs = rest[:3 + n_in], rest[4 + n_in:]
    else:
        x = refs[0][...]
        issue = lambda k, n_slices: None
        ins, outs = refs[1:4 + n_in], refs[4 + n_in:]
    h = _rms_mod(x, ins[0][...], ins[1][...], ins[2][...]).astype(BF16)
    body(h, issue, *ins[3:], *outs)


def _inproj(body, name, x, pending, g, shift, scale, rows_per_mod, extra, extra_specs, out_shape, out_specs,
            tm):
    t, d = x.shape
    mod_map = lambda i, *_: ((i * tm) // rows_per_mod, 0, 0)
    in_specs = [pl.BlockSpec((1, d), lambda i, *_: (0, 0)), pl.BlockSpec((None, 1, d), mod_map),
                pl.BlockSpec((None, 1, d), mod_map)] + list(extra_specs)
    kern = functools.partial(_inproj_kernel, body=body, n_in=len(extra), combine=pending is not None)
    if pending is None:
        return pl.pallas_call(
            kern, out_shape=tuple(out_shape), grid=(t // tm,),
            in_specs=[pl.BlockSpec((tm, d), lambda i: (i, 0))] + in_specs, out_specs=tuple(out_specs),
            compiler_params=_cparams("parallel"), name=name,
        )(x, g, shift, scale, *extra)
    pos, rc, gate, ys = pending
    return pl.pallas_call(
        kern,
        out_shape=(jax.ShapeDtypeStruct((t, d), F32),) + tuple(out_shape),
        grid_spec=pltpu.PrefetchScalarGridSpec(
            num_scalar_prefetch=1, grid=(t // tm,),
            in_specs=_combine_in_specs(tm, d, rc.shape[0], 0, rows_per_mod) + in_specs,
            out_specs=(pl.BlockSpec((tm, d), lambda i, pos: (i, 0)),) + tuple(out_specs),
            scratch_shapes=_combine_scratch(tm, d)),
        compiler_params=_cparams("arbitrary"), name=name + "_combine",
    )(pos, x, rc, gate, ys, g, shift, scale, *extra)


def _inproj_even_body(h, issue, w_ref, gqk_ref, cos_ref, sin_ref, gm_ref, q_ref, k_ref, v_ref, a_ref, *, rope):
    tm = h.shape[0]
    lane = lax.broadcasted_iota(jnp.int32, (tm, LANES), 1)
    first_head = lane < HEAD_DIM
    first_half = (lane % HEAD_DIM) < (HEAD_DIM // 2)
    c0 = QK_WIDTH + KV_WIDTH
    cw = (w_ref.shape[1] - c0) // 2
    issue(0, 1)
    p = jnp.dot(h, w_ref[...], preferred_element_type=F32)
    for m in range(QK_WIDTH // LANES):
        y = p[:, LANES * m:LANES * (m + 1)]
        sq = y * y
        sq_hi = sq.astype(BF16)
        sq_lo = (sq - sq_hi.astype(F32)).astype(BF16)
        ms = jnp.dot(jnp.concatenate([sq_hi, sq_lo], axis=1), gm_ref[...], preferred_element_type=F32)
        y = y * lax.rsqrt(ms + EPS) * gqk_ref[:, LANES * m:LANES * (m + 1)]
        if rope:
            partner = jnp.where(first_half, pltpu.roll(y, LANES - HEAD_DIM // 2, 1),
                                pltpu.roll(y, HEAD_DIM // 2, 1))
            y = y * cos_ref[...] + partner * sin_ref[...]
        if m < ATTN_WIDTH // LANES:
            q_ref[m] = jnp.where(first_head, y, 0.0).astype(BF16)
            q_ref[m + ATTN_WIDTH // LANES] = jnp.where(first_head, 0.0, y).astype(BF16)
        else:
            k_ref[...] = y.astype(BF16)
    v = p[:, QK_WIDTH:c0]
    v_ref[0] = jnp.where(first_head, v, 1.0).astype(BF16)
    v_ref[1] = jnp.where(first_head, 1.0, v).astype(BF16)
    a_ref[...] = (p[:, c0:c0 + cw] * _sigmoid(p[:, c0 + cw:])).astype(BF16)


def _inproj_even(x, pending, g, shift, scale, rows_per_mod, w, gqk, cos_t, sin_t, gmat, tm, rope):
    t, d = x.shape
    n = w.shape[1]
    cw = (n - QK_WIDTH - KV_WIDTH) // 2
    n_pos = cos_t.shape[0] // tm
    return _inproj(
        functools.partial(_inproj_even_body, rope=rope), "inproj_even", x, pending, g, shift, scale,
        rows_per_mod, [w, gqk, cos_t, sin_t, gmat],
        [pl.BlockSpec((d, n), lambda i, *_: (0, 0)),
         pl.BlockSpec((1, QK_WIDTH), lambda i, *_: (0, 0)),
         pl.BlockSpec((tm, LANES), lambda i, *_: (i % n_pos, 0)),
         pl.BlockSpec((tm, LANES), lambda i, *_: (i % n_pos, 0)),
         pl.BlockSpec((2 * LANES, LANES), lambda i, *_: (0, 0))],
        [jax.ShapeDtypeStruct((N_HEADS, t, LANES), BF16), jax.ShapeDtypeStruct((t, KV_WIDTH), BF16),
         jax.ShapeDtypeStruct((N_KV_HEADS, t, LANES), BF16), jax.ShapeDtypeStruct((t, cw), BF16)],
        [pl.BlockSpec((N_HEADS, tm, LANES), lambda i, *_: (0, i, 0)),
         pl.BlockSpec((tm, KV_WIDTH), lambda i, *_: (i, 0)),
         pl.BlockSpec((N_KV_HEADS, tm, LANES), lambda i, *_: (0, i, 0)),
         pl.BlockSpec((tm, cw), lambda i, *_: (i, 0))],
        tm)


def _inproj_odd_body(h, issue, w_ref, gb_ref, u_ref):
    w = w_ref.shape[1] // 3
    n_phase = w // MXU_COLS
    proj = lambda lo: jnp.dot(h, w_ref[:, lo:lo + MXU_COLS], preferred_element_type=F32)
    for c in range(n_phase):
        issue(c, n_phase)
        cols = slice(MXU_COLS * c, MXU_COLS * (c + 1))
        gb_ref[:, cols] = proj(MXU_COLS * c).astype(BF16)
        u_ref[:, cols] = (proj(w + MXU_COLS * c) * proj(2 * w + MXU_COLS * c)).astype(BF16)


def _inproj_odd(x, pending, g, shift, scale, rows_per_mod, w, tm):
    t, d = x.shape
    n = w.shape[1]
    return _inproj(
        _inproj_odd_body, "inproj_odd", x, pending, g, shift, scale, rows_per_mod, [w],
        [pl.BlockSpec((d, n), lambda i, *_: (0, 0))],
        [jax.ShapeDtypeStruct((t, n // 3), BF16), jax.ShapeDtypeStruct((t, n // 3), BF16)],
        [pl.BlockSpec((tm, n // 3), lambda i, *_: (i, 0)), pl.BlockSpec((tm, n // 3), lambda i, *_: (i, 0))],
        tm)


def _attn_kernel(q_ref, k_ref, v_ref, o_ref, acc_ref):
    tq = q_ref.shape[1]
    first = lax.broadcasted_iota(jnp.int32, (tq, LANES), 1) < HEAD_DIM

    def pair(m, carry):
        outs = []
        for g in range(N_KV_HEADS):
            q = q_ref[m + GQA_GROUP * g]
            s = lax.dot_general(q, k_ref[...], (((1,), (1,)), ((), ())), preferred_element_type=F32)
            p = jnp.exp2(s - jnp.max(s, axis=-1, keepdims=True)).astype(BF16)
            o = jnp.dot(p, v_ref[g], preferred_element_type=F32)
            l = o[:, HEAD_DIM:HEAD_DIM + 1] if g == 0 else o[:, 0:1]
            outs.append(o / l)
        acc_ref[m] = jnp.where(first, outs[0], outs[1])
        return carry

    lax.fori_loop(0, GQA_GROUP, pair, 0)
    o_ref[...] = jnp.concatenate([acc_ref[m] for m in range(GQA_GROUP)], axis=1).astype(BF16)


def _attention(q, k_all, v_all, s_q, tq):
    t = q.shape[1]
    b, skv, _ = k_all.shape
    nq = s_q // tq
    return pl.pallas_call(
        _attn_kernel,
        out_shape=jax.ShapeDtypeStruct((t, ATTN_WIDTH), BF16),
        grid=(b, nq),
        in_specs=[pl.BlockSpec((N_HEADS, tq, LANES), lambda bi, i: (0, bi * nq + i, 0)),
                  pl.BlockSpec((None, skv, LANES), lambda bi, i: (bi, 0, 0)),
                  pl.BlockSpec((N_KV_HEADS, None, skv, LANES), lambda bi, i: (0, bi, 0, 0))],
        out_specs=pl.BlockSpec((tq, ATTN_WIDTH), lambda bi, i: (bi * nq + i, 0)),
        scratch_shapes=[pltpu.VMEM((GQA_GROUP, tq, LANES), F32)],
        compiler_params=_cparams("parallel", "parallel"),
        name="attention",
    )(q, k_all, v_all)


def _fill_window(win_ref, cur_ref, prev_ref, next_ref, is_first, is_last):
    tm = cur_ref.shape[0]
    win_ref[0:HALO, :] = jnp.where(is_first, 0.0, prev_ref[...].astype(F32))
    win_ref[HALO:HALO + tm, :] = cur_ref[...].astype(F32)
    win_ref[HALO + tm:2 * HALO + tm, :] = jnp.where(is_last, 0.0, next_ref[...].astype(F32))


def _fill_shifted(win_ref, sh_ref):
    n = sh_ref.shape[1]
    for s in range(1, SUBLANES):
        sh_ref[s - 1] = win_ref[s:s + n, :]


def _dwconv_chunk(win_ref, sh_ref, w_ref, r0, rows, taps):
    off = HALO - taps // 2
    acc = None
    for kk in range(taps):
        q, s = divmod(off + kk, SUBLANES)
        if sh_ref is None:
            src = win_ref[r0 + off + kk:r0 + off + kk + rows, :]
        elif s == 0:
            src = win_ref[r0 + SUBLANES * q:r0 + SUBLANES * q + rows, :]
        else:
            src = sh_ref[s - 1, r0 + SUBLANES * q:r0 + SUBLANES * q + rows, :]
        term = src.reshape(rows // SUBLANES, SUBLANES, -1) * w_ref[SUBLANES * kk:SUBLANES * (kk + 1), :]
        acc = term if acc is None else acc + term
    return acc.reshape(rows, -1)


def _residual_tail(x_ref, o, gate_ref, g2_ref, sh2_ref, sc2_ref, xo_ref, hp_ref):
    x_new = x_ref[...] + gate_ref[...] * o
    xo_ref[...] = x_new
    hp_ref[...] = _pack_halves(_rms_mod(x_new, g2_ref[...], sh2_ref[...], sc2_ref[...]))


def _outproj_even_kernel(x_ref, at_ref, a_ref, ap_ref, an_ref, dw_ref, db_ref, lg_ref, lb_ref, w_ref,
                         gate_ref, g2_ref, sh2_ref, sc2_ref, xo_ref, hp_ref, win_ref, cv_ref, shw_ref,
                         *, tiles_per_seq):
    tm = x_ref.shape[0]
    i = pl.program_id(0) % tiles_per_seq
    _fill_window(win_ref, a_ref, ap_ref, an_ref, i == 0, i == tiles_per_seq - 1)
    _fill_shifted(win_ref, shw_ref)
    for r0 in range(0, tm, CONV_ROWS):
        y = _dwconv_chunk(win_ref, shw_ref, dw_ref, r0, CONV_ROWS, CONV_A_K) + db_ref[...]
        mu = jnp.mean(y, axis=-1, keepdims=True)
        yc = y - mu
        var = jnp.mean(yc * yc, axis=-1, keepdims=True)
        yn = yc * lax.rsqrt(var + EPS) * lg_ref[...] + lb_ref[...]
        cv_ref[r0:r0 + CONV_ROWS, :] = (yn * _sigmoid(yn)).astype(BF16)
    aw = at_ref.shape[1]
    o = (jnp.dot(at_ref[...], w_ref[0:aw, :], preferred_element_type=F32)
         + jnp.dot(cv_ref[...], w_ref[aw:, :], preferred_element_type=F32))
    _residual_tail(x_ref, o, gate_ref, g2_ref, sh2_ref, sc2_ref, xo_ref, hp_ref)


def _outproj_odd_kernel(x_ref, gb_ref, u_ref, up_ref, un_ref, cw_ref, w_ref,
                        gate_ref, g2_ref, sh2_ref, sc2_ref, xo_ref, hp_ref, win_ref, cv_ref,
                        *, tiles_per_seq):
    tm = x_ref.shape[0]
    i = pl.program_id(0) % tiles_per_seq
    _fill_window(win_ref, u_ref, up_ref, un_ref, i == 0, i == tiles_per_seq - 1)
    for r0 in range(0, tm, CONV_ROWS):
        y = _dwconv_chunk(win_ref, None, cw_ref, r0, CONV_ROWS, SC_K)
        cv_ref[r0:r0 + CONV_ROWS, :] = (gb_ref[r0:r0 + CONV_ROWS, :].astype(F32) * y).astype(BF16)
    o = jnp.dot(cv_ref[...], w_ref[...], preferred_element_type=F32)
    _residual_tail(x_ref, o, gate_ref, g2_ref, sh2_ref, sc2_ref, xo_ref, hp_ref)


def _halo_specs(tm, width, n_rows):
    hb = tm // HALO
    last = n_rows // HALO - 1
    return [pl.BlockSpec((tm, width), lambda i: (i, 0)),
            pl.BlockSpec((HALO, width), lambda i: (jnp.maximum(i * hb - 1, 0), 0)),
            pl.BlockSpec((HALO, width), lambda i: (jnp.minimum((i + 1) * hb, last), 0))]


def _outproj(kernel_fn, name, x, lead, conv_in, params, w, gate, g2, shift2, scale2, rows_per_mod,
             seq_len, tm, shifted_window):
    t, d = x.shape
    cw = conv_in.shape[1]
    scratch = [pltpu.VMEM((tm + 2 * HALO, cw), F32), pltpu.VMEM((tm, cw), BF16)]
    if shifted_window:
        scratch.append(pltpu.VMEM((SUBLANES - 1, tm + 2 * HALO - SUBLANES, cw), F32))
    mod_map = lambda i: ((i * tm) // rows_per_mod, 0, 0)
    row_spec = lambda a: pl.BlockSpec((tm, a.shape[1]), lambda i: (i, 0))
    full_spec = lambda a: pl.BlockSpec(a.shape, lambda i: (0,) * a.ndim)
    in_specs = ([row_spec(x)] + [row_spec(a) for a in lead] + _halo_specs(tm, cw, t)
                + [full_spec(a) for a in params] + [full_spec(w)]
                + [pl.BlockSpec((None, 1, d), mod_map), full_spec(g2),
                   pl.BlockSpec((None, 1, d), mod_map), pl.BlockSpec((None, 1, d), mod_map)])
    return pl.pallas_call(
        functools.partial(kernel_fn, tiles_per_seq=seq_len // tm),
        out_shape=(jax.ShapeDtypeStruct((t, d), F32), jax.ShapeDtypeStruct((t, d // 2), U32)),
        grid=(t // tm,),
        in_specs=in_specs,
        out_specs=(pl.BlockSpec((tm, d), lambda i: (i, 0)), pl.BlockSpec((tm, d // 2), lambda i: (i, 0))),
        scratch_shapes=scratch,
        compiler_params=_cparams("parallel"),
        name=name,
    )(x, *lead, conv_in, conv_in, conv_in, *params, w, gate, g2, shift2, scale2)


def _part_tile_maps(part_tiles):
    maps, start = [], 0
    for n in part_tiles:
        maps.append(lambda i, *_, start=start, n=n: (jnp.clip(i - start, 0, n - 1), 0))
        start += n
    return maps


def _router_kernel(*refs, part_tiles):
    hp_refs = refs[:len(part_tiles)]
    rw_ref, bias_ref, tri_ref, route_ref, cnt_ref, base_ref = refs[len(part_tiles):]
    tm = hp_refs[0].shape[0]

    @pl.when(pl.program_id(0) == 0)
    def _():
        base_ref[...] = jnp.zeros_like(base_ref)

    hp, start = hp_refs[0][...], 0
    for k in range(1, len(part_tiles)):
        start += part_tiles[k - 1]
        hp = jnp.where(pl.program_id(0) >= start, hp_refs[k][...], hp)
    lo, hi = _unpack_halves(hp)
    h = jnp.concatenate([lo, hi], axis=1).astype(BF16)
    logits = jnp.dot(h, rw_ref[...], preferred_element_type=F32)
    lt = logits.T[0:N_EXPERTS, :]
    aff = _sigmoid(lt)
    sel = aff + bias_ref[...]
    rows = lambda a, e: a[e:e + 1, :]

    best = None
    for g in range(N_GROUPS):
        a, b, c, d = (rows(sel, EXPERTS_PER_GROUP * g + j) for j in range(EXPERTS_PER_GROUP))
        score = jnp.maximum(jnp.maximum(a + b, c + d), jnp.maximum(a, b) + jnp.maximum(c, d))
        if best is None:
            best, grp = score, jnp.zeros((1, tm), jnp.int32)
        else:
            better = score > best
            grp = jnp.where(better, g, grp)
            best = jnp.where(better, score, best)

    def in_group(a, j):
        out = rows(a, j)
        for g in range(1, N_GROUPS):
            out = jnp.where(grp == g, rows(a, EXPERTS_PER_GROUP * g + j), out)
        return out

    s = [in_group(sel, j) for j in range(EXPERTS_PER_GROUP)]
    f = [in_group(aff, j) for j in range(EXPERTS_PER_GROUP)]

    def first_argmax(vals):
        m = functools.reduce(jnp.maximum, vals)
        idx = jnp.full((1, tm), len(vals) - 1, jnp.int32)
        for j in range(len(vals) - 2, -1, -1):
            idx = jnp.where(vals[j] == m, j, idx)
        return idx

    i1 = first_argmax(s)
    i2 = first_argmax([jnp.where(i1 == j, -jnp.inf, s[j]) for j in range(EXPERTS_PER_GROUP)])
    e_lo = jnp.minimum(i1, i2)
    e_hi = jnp.maximum(i1, i2)
    pick = lambda vals, idx: functools.reduce(
        lambda acc, j: jnp.where(idx == j, vals[j], acc), range(1, len(vals)), vals[0])
    f_lo, f_hi = pick(f, e_lo), pick(f, e_hi)
    denom = f_lo + f_hi
    pair = jnp.where(e_lo == 0, e_hi - 1, jnp.where(e_lo == 1, e_hi + 1, N_PAIRS - 1))
    cls = grp * N_PAIRS + pair

    onehot = (lax.broadcasted_iota(jnp.int32, (CLASS_ROWS, tm), 0) == cls)
    onehot_f = onehot.astype(F32)
    before = jnp.dot(onehot_f.astype(BF16), tri_ref[...], preferred_element_type=F32) + base_ref[:, 0:1]
    rank = jnp.sum(onehot_f * before, axis=0, keepdims=True)
    base_ref[...] = base_ref[...] + jnp.sum(onehot_f, axis=1, keepdims=True)

    route_ref[...] = jnp.concatenate(
        [cls.astype(F32), rank, f_lo / denom, f_hi / denom, jnp.zeros((4, tm), F32)], axis=0)
    cnt_ref[...] = base_ref[...]


def _router(parts, rw, bias, tri, tm):
    hw = parts[0].shape[1]
    part_tiles = tuple(p.shape[0] // tm for p in parts)
    t = tm * sum(part_tiles)
    return pl.pallas_call(
        functools.partial(_router_kernel, part_tiles=part_tiles),
        out_shape=(jax.ShapeDtypeStruct((8, t), F32), jax.ShapeDtypeStruct((CLASS_ROWS, LANES), F32)),
        grid=(t // tm,),
        in_specs=[pl.BlockSpec((tm, hw), m) for m in _part_tile_maps(part_tiles)] + [
                  pl.BlockSpec(rw.shape, lambda i: (0, 0)),
                  pl.BlockSpec(bias.shape, lambda i: (0, 0)),
                  pl.BlockSpec((tm, tm), lambda i: (0, 0))],
        out_specs=(pl.BlockSpec((8, tm), lambda i: (0, i)),
                   pl.BlockSpec((CLASS_ROWS, LANES), lambda i: (0, 0))),
        scratch_shapes=[pltpu.VMEM((CLASS_ROWS, LANES), F32)],
        compiler_params=_cparams("arbitrary"),
        name="router",
    )(*parts, rw, bias, tri)


def _dispatch_kernel(pos_ref, *refs, part_tiles):
    hp_refs = refs[:len(part_tiles)]
    init_ref, hs_ref, sem = refs[len(part_tiles):]
    del init_ref
    tm = hp_refs[0].shape[0]
    i = pl.program_id(0)

    start = 0
    for hp_ref, n in zip(hp_refs, part_tiles):
        @pl.when((i >= start) & (i < start + n))
        def _(hp_ref=hp_ref):
            for r in range(tm):
                pltpu.async_copy(hp_ref.at[pl.ds(r, 1)], hs_ref.at[pl.ds(pos_ref[i * tm + r], 1)], sem,
                                 priority=r % DMA_QUEUES)
        start += n
    pltpu.make_async_copy(hp_refs[0], hs_ref.at[pl.ds(0, tm)], sem).wait()


def _dispatch(pos, parts, hs_init, tm):
    hw = parts[0].shape[1]
    part_tiles = tuple(p.shape[0] // tm for p in parts)
    n_rows = hs_init.shape[0]
    return pl.pallas_call(
        functools.partial(_dispatch_kernel, part_tiles=part_tiles),
        out_shape=jax.ShapeDtypeStruct((n_rows, hw), U32),
        grid_spec=pltpu.PrefetchScalarGridSpec(
            num_scalar_prefetch=1,
            grid=(sum(part_tiles),),
            in_specs=[pl.BlockSpec((tm, hw), m) for m in _part_tile_maps(part_tiles)]
            + [pl.BlockSpec(memory_space=pl.ANY)],
            out_specs=pl.BlockSpec(memory_space=pl.ANY),
            scratch_shapes=[pltpu.SemaphoreType.DMA(())]),
        input_output_aliases={1 + len(parts): 0},
        compiler_params=_cparams("arbitrary"),
        name="dispatch",
    )(pos, *parts, hs_init)


def _moe_kernel(elo_ref, ehi_ref, nused_ref, hs_ref, wg_lo, wu_lo, wd_lo, wg_hi, wu_hi, wd_hi, ys_ref):
    del elo_ref, ehi_ref

    @pl.when(pl.program_id(0) >= nused_ref[0])
    def _():
        ys_ref[...] = jnp.zeros_like(ys_ref)

    @pl.when(pl.program_id(0) < nused_ref[0])
    def _():
        lo, hi = _unpack_halves(hs_ref[...])
        h = jnp.concatenate([lo, hi], axis=1).astype(BF16)

        def expert(wg_ref, wu_ref, wd_ref):
            g = jnp.dot(h, wg_ref[...], preferred_element_type=F32)
            u = jnp.dot(h, wu_ref[...], preferred_element_type=F32)
            act = (g * _sigmoid(g) * u).astype(BF16)
            return jnp.dot(act, wd_ref[...], preferred_element_type=F32)

        ya = lax.bitcast_convert_type(expert(wg_lo, wu_lo, wd_lo).astype(BF16).astype(F32), U32)
        yb = lax.bitcast_convert_type(expert(wg_hi, wu_hi, wd_hi).astype(BF16).astype(F32), U32)
        ys_ref[...] = (ya >> 16) | (yb & jnp.uint32(0xFFFF0000))


def _moe(e_lo, e_hi, n_used, hs, wg, wu, wd, layer):
    n_rows, hw = hs.shape
    d = wd.shape[3]
    n_blocks = n_rows // MOE_BLOCK
    row_map = lambda j, lo, hi, nu: (jnp.minimum(j, nu[0] - 1), 0)
    w_spec = lambda w, which: pl.BlockSpec(
        (None, None) + w.shape[2:], lambda j, lo, hi, nu: (layer, (lo, hi)[which][j], 0, 0))
    return pl.pallas_call(
        _moe_kernel,
        out_shape=jax.ShapeDtypeStruct((n_rows, d), U32),
        grid_spec=pltpu.PrefetchScalarGridSpec(
            num_scalar_prefetch=3,
            grid=(n_blocks,),
            in_specs=[pl.BlockSpec((MOE_BLOCK, hw), row_map)]
            + [w_spec(w, which) for which in (0, 1) for w in (wg, wu, wd)],
            out_specs=pl.BlockSpec((MOE_BLOCK, d), lambda j, lo, hi, nu: (j, 0))),
        compiler_params=_cparams("arbitrary"),
        name="experts",
    )(e_lo, e_hi, n_used, hs, wg, wu, wd, wg, wu, wd)


def _combine_kernel(pos_ref, x_ref, rc_ref, gate_ref, ys_ref, xo_ref, ybuf, sems):
    x_new, issue = _combined_residual(pos_ref, x_ref, rc_ref, gate_ref, ys_ref, ybuf, sems)
    xo_ref[...] = x_new
    issue(0, 1)


def _combine(pos, x, rc, first_tile, gate, ys, rows_per_mod, tm):
    t, d = x.shape
    return pl.pallas_call(
        _combine_kernel,
        out_shape=jax.ShapeDtypeStruct((t, d), F32),
        grid_spec=pltpu.PrefetchScalarGridSpec(
            num_scalar_prefetch=1,
            grid=(t // tm,),
            in_specs=_combine_in_specs(tm, d, rc.shape[0], first_tile, rows_per_mod),
            out_specs=pl.BlockSpec((tm, d), lambda i, pos: (i, 0)),
            scratch_shapes=_combine_scratch(tm, d)),
        compiler_params=_cparams("arbitrary"),
        name="combine",
    )(pos, x, rc, gate, ys)


def _deinterleave(base):
    return [base + 2 * p for p in range(HEAD_DIM // 2)] + [base + 2 * p + 1 for p in range(HEAD_DIM // 2)]


def _even_in_perm(n_cols):
    cols = []
    half = N_HEADS // N_KV_HEADS
    for m in range(half):
        cols += _deinterleave(HEAD_DIM * m) + _deinterleave(HEAD_DIM * (m + half))
    for kh in range(N_KV_HEADS):
        cols += _deinterleave(ATTN_WIDTH + HEAD_DIM * kh)
    cols += list(range(QK_WIDTH, n_cols))
    return np.asarray(cols, np.int32)


def _even_out_perm(n_rows):
    rows = []
    half = N_HEADS // N_KV_HEADS
    for m in range(half):
        rows += list(range(HEAD_DIM * m, HEAD_DIM * (m + 1)))
        rows += list(range(HEAD_DIM * (m + half), HEAD_DIM * (m + half + 1)))
    rows += list(range(ATTN_WIDTH, n_rows))
    return np.asarray(rows, np.int32)


def _rope_tables(n_tokens):
    rows = n_tokens // GRID_W
    row = jnp.repeat(jnp.arange(rows, dtype=F32), GRID_W)
    col = jnp.tile(jnp.arange(GRID_W, dtype=F32), rows)
    inv = ROPE_THETA ** (-jnp.arange(ROPE_PAIRS_PER_AXIS, dtype=F32) / ROPE_PAIRS_PER_AXIS)
    ang = jnp.concatenate([row[:, None] * inv, col[:, None] * inv], axis=-1)
    cos, sin = jnp.cos(ang), jnp.sin(ang)
    reps = LANES // HEAD_DIM
    return (jnp.tile(jnp.concatenate([cos, cos], axis=-1), (1, reps)),
            jnp.tile(jnp.concatenate([-sin, sin], axis=-1), (1, reps)))


def _head_mean_matrix():
    blk = np.kron(np.eye(LANES // HEAD_DIM), np.full((HEAD_DIM, HEAD_DIM), 1.0 / HEAD_DIM))
    return jnp.asarray(np.concatenate([blk, blk], axis=0), BF16)


def _moe_layer(parts, hs_buf, router_wp, router_b, tri, expert_w, layer, tm):
    t = sum(p.shape[0] for p in parts)
    route, counts = _router(parts, router_wp, router_b, tri, tm)
    cls = route[0].astype(jnp.int32)
    rank = route[1].astype(jnp.int32)
    cnt = counts[:N_CLASSES, 0].astype(jnp.int32)
    padded = (cnt + MOE_BLOCK - 1) // MOE_BLOCK * MOE_BLOCK
    pad_end = jnp.cumsum(padded)
    pos = (pad_end - padded)[cls] + rank
    n_blocks = hs_buf.shape[0] // MOE_BLOCK
    assert n_blocks >= -(-t // MOE_BLOCK) + N_CLASSES
    n_used = pad_end[-1] // MOE_BLOCK
    blk = jnp.minimum(jnp.arange(n_blocks, dtype=jnp.int32), n_used - 1)
    bcls = jnp.sum((pad_end[None, :] <= (blk * MOE_BLOCK)[:, None]).astype(jnp.int32), axis=1)
    bcls = jnp.minimum(bcls, N_CLASSES - 1)
    e_lo = EXPERTS_PER_GROUP * (bcls // N_PAIRS) + jnp.asarray(PAIR_LO, jnp.int32)[bcls % N_PAIRS]
    e_hi = EXPERTS_PER_GROUP * (bcls // N_PAIRS) + jnp.asarray(PAIR_HI, jnp.int32)[bcls % N_PAIRS]
    hs = _dispatch(pos, parts, hs_buf, math.gcd(tm, ROW_TILE))
    ys = _moe(e_lo, e_hi, n_used.reshape(1).astype(jnp.int32), hs, *expert_w, layer)
    return pos, route, ys, hs


def kernel(x, c, ctx, c_ctx, w_mod, b_mod, norm_mix_g, norm_ffn_g, ev_w_in, ev_q_norm_g, ev_k_norm_g,
           ev_dw_w, ev_dw_b, ev_ln_g, ev_ln_b, ev_w_out, od_w_in, od_conv_w, od_w_out, router_w,
           router_bias, moe_w_gate, moe_w_up, moe_w_down):
    b, s, d = x.shape
    n_ctx = ctx.shape[1]
    depth = w_mod.shape[0]
    last_attn = (depth - 1) - ((depth - 1) % 2)
    tm = min(ROW_TILE, s)
    tmz = min(ROW_TILE, n_ctx)
    tq = min(Q_TILE, s)
    tqz = min(Q_TILE, n_ctx)
    assert s % tm == 0 and n_ctx % tmz == 0 and s % tq == 0 and n_ctx % tqz == 0 and s % GRID_W == 0

    r = -(-(b + 1) // 8) * 8
    c_all = jnp.zeros((r, d), F32).at[:b].set(c).at[b].set(c_ctx)
    mods = _modulation(c_all, w_mod, b_mod)

    def mod_x(l, j):
        return mods[l, :b, j * d:(j + 1) * d].reshape(b, 1, d)

    def mod_z(l, j):
        return mods[l, b:b + 1, j * d:(j + 1) * d].reshape(1, 1, d)

    cos_t, sin_t = _rope_tables(s)
    gmat = _head_mean_matrix()
    in_perm = _even_in_perm(ev_w_in.shape[2])
    out_perm = _even_out_perm(ev_w_out.shape[1])
    head_perm = np.asarray(_deinterleave(0), np.int32)
    router_wp = jnp.zeros((d, LANES), F32).at[:, :N_EXPERTS].set(router_w).astype(BF16)
    router_b = router_bias.astype(F32).reshape(N_EXPERTS, 1)
    tri_n = min(ROUTER_TILE, b * s)
    tri = jnp.asarray(np.triu(np.ones((tri_n, tri_n), np.float32), 1), BF16)

    xf = x.reshape(b * s, d)
    zf = ctx.reshape(b * n_ctx, d)
    expert_w = tuple(w.astype(BF16) for w in (moe_w_gate, moe_w_up, moe_w_down))
    pending = None
    ctm = min(Q_TILE, s, n_ctx)
    n_routed = b * s + (b * n_ctx if last_attn > 0 else 0)
    hs_buf = jnp.zeros(((-(-n_routed // MOE_BLOCK) + N_CLASSES) * MOE_BLOCK, d // 2), U32)
    for l in range(depth):
        i = l // 2
        ctx_in = l <= last_attn
        ctx_out = l < last_attn
        gmix = norm_mix_g[l].reshape(1, d)
        gffn = norm_ffn_g[l].reshape(1, d)
        if l % 2 == 0:
            w_in = ev_w_in[i][:, in_perm].astype(BF16)
            w_out = ev_w_out[i][out_perm, :].astype(BF16)
            gq = ev_q_norm_g[i][head_perm] * (ATTN_SCALE * LOG2E)
            gk = ev_k_norm_g[i][head_perm]
            gqk = jnp.concatenate([jnp.tile(gq, N_HEADS), jnp.tile(gk, N_KV_HEADS)]).reshape(1, QK_WIDTH)
            conv_params = [jnp.repeat(ev_dw_w[i], SUBLANES, axis=0), ev_dw_b[i].reshape(1, -1),
                           ev_ln_g[i].reshape(1, -1),
                           ev_ln_b[i].reshape(1, -1)]
            res = _inproj_even(xf, pending, gmix, mod_x(l, 0), mod_x(l, 1), s, w_in, gqk, cos_t, sin_t, gmat,
                               tm, True)
            if pending is not None:
                xf, res = res[0], res[1:]
            q, k, v, a = res
            k_all = k.reshape(b, s, KV_WIDTH)
            v_all = v.reshape(N_KV_HEADS, b, s, LANES)
            if ctx_in:
                qz, kz, vz, az = _inproj_even(zf, None, gmix, mod_z(l, 0), mod_z(l, 1), b * n_ctx, w_in, gqk,
                                              cos_t, sin_t, gmat, tmz, False)
                kz = kz.reshape(b, n_ctx, KV_WIDTH)
                vz = vz.reshape(N_KV_HEADS, b, n_ctx, LANES)
                k_all = jnp.concatenate([k_all, kz], axis=1)
                v_all = jnp.concatenate([v_all, vz], axis=2)
            attn = _attention(q, k_all, v_all, s, tq)
            xf, hp = _outproj(_outproj_even_kernel, "outproj_even", xf, [attn], a, conv_params, w_out,
                              mod_x(l, 2), gffn, mod_x(l, 3), mod_x(l, 4), s, s, tm, True)
            if ctx_out:
                attn_z = _attention(qz, kz, vz, n_ctx, tqz)
                zf, hpz = _outproj(_outproj_even_kernel, "outproj_even_ctx", zf, [attn_z], az, conv_params,
                                   w_out, mod_z(l, 2), gffn, mod_z(l, 3), mod_z(l, 4), b * n_ctx, n_ctx, tmz,
                                   True)
        else:
            w_in = od_w_in[i].astype(BF16)
            w_out = od_w_out[i].astype(BF16)
            conv_params = [jnp.repeat(od_conv_w[i], SUBLANES, axis=0)]
            res = _inproj_odd(xf, pending, gmix, mod_x(l, 0), mod_x(l, 1), s, w_in, tm)
            if pending is not None:
                xf, res = res[0], res[1:]
            gb, u = res
            xf, hp = _outproj(_outproj_odd_kernel, "outproj_odd", xf, [gb], u, conv_params, w_out,
                              mod_x(l, 2), gffn, mod_x(l, 3), mod_x(l, 4), s, s, tm, False)
            if ctx_out:
                gbz, uz = _inproj_odd(zf, None, gmix, mod_z(l, 0), mod_z(l, 1), b * n_ctx, w_in, tmz)
                zf, hpz = _outproj(_outproj_odd_kernel, "outproj_odd_ctx", zf, [gbz], uz, conv_params, w_out,
                                   mod_z(l, 2), gffn, mod_z(l, 3), mod_z(l, 4), b * n_ctx, n_ctx, tmz, False)

        parts = [hp, hpz] if ctx_out else [hp]
        rtm = math.gcd(tri_n, math.gcd(b * s, b * n_ctx)) if ctx_out else tri_n
        pos, rc, ys, hs_buf = _moe_layer(parts, hs_buf, router_wp, router_b, tri[:rtm, :rtm], expert_w, l, rtm)
        n_lat = b * s
        pending = (pos[:n_lat], rc, mod_x(l, 5), ys)
        if ctx_out:
            zf = _combine(pos[n_lat:], zf, rc, n_lat // ctm, mod_z(l, 5), ys, b * n_ctx, ctm)
    pos, rc, gate, ys = pending
    return _combine(pos, xf, rc, 0, gate, ys, s, ctm).reshape(b, s, d)
```

```python
import functools
import math

import numpy as np
import jax
import jax.numpy as jnp
from jax import lax
from jax.experimental import pallas as pl
from jax.experimental.pallas import tpu as pltpu

F32 = jnp.float32
BF16 = jnp.bfloat16
U32 = jnp.uint32

GRID_W = 64
EPS = 1e-6
N_HEADS = 8
N_KV_HEADS = 2
GQA_GROUP = N_HEADS // N_KV_HEADS
HEAD_DIM = 64
ATTN_WIDTH = N_HEADS * HEAD_DIM
KV_WIDTH = N_KV_HEADS * HEAD_DIM
QK_WIDTH = ATTN_WIDTH + KV_WIDTH
ATTN_SCALE = HEAD_DIM ** -0.5
ROPE_THETA = 10000.0
ROPE_PAIRS_PER_AXIS = HEAD_DIM // 4
CONV_A_K = 31
SC_K = 3
N_EXPERTS = 16
N_GROUPS = 4
EXPERTS_PER_GROUP = N_EXPERTS // N_GROUPS
D_EXPERT = 512
PAIR_LO = (0, 0, 0, 1, 1, 2)
PAIR_HI = (1, 2, 3, 2, 3, 3)
N_PAIRS = len(PAIR_LO)
N_CLASSES = N_GROUPS * N_PAIRS
CLASS_ROWS = 32

LANES = 128
SUBLANES = 8
MXU_COLS = 256
HALO = 16
VMEM_LIMIT = 48 * 1024 * 1024
ROW_TILE = 512
Q_TILE = 256
ROUTER_TILE = 1024
DMA_QUEUES = 2
MOE_BLOCK = 512
CONV_ROWS = 64
LOG2E = math.log2(math.e)


def _cparams(*sem):
    return pltpu.CompilerParams(dimension_semantics=sem, vmem_limit_bytes=VMEM_LIMIT)


def _sigmoid(x):
    return 1.0 / (1.0 + jnp.exp(-x))


def _rms_mod(x, g, shift, scale):
    y = x * lax.rsqrt(jnp.mean(x * x, axis=-1, keepdims=True) + EPS) * g
    return y * (1.0 + scale) + shift


def _pack_halves(h):
    w = h.shape[1] // 2
    lo = lax.bitcast_convert_type(h[:, :w].astype(BF16).astype(F32), U32)
    hi = lax.bitcast_convert_type(h[:, w:].astype(BF16).astype(F32), U32)
    return (lo >> 16) | (hi & jnp.uint32(0xFFFF0000))


def _unpack_halves(w):
    lo = lax.bitcast_convert_type(w << 16, F32)
    hi = lax.bitcast_convert_type(w & jnp.uint32(0xFFFF0000), F32)
    return lo, hi


def _issue_row_gathers(pos_ref, ys_ref, ybuf, sems, tile, slot, rows):
    tm = ybuf.shape[1]
    for r in rows:
        pltpu.async_copy(ys_ref.at[pl.ds(pos_ref[tile * tm + r], 1)], ybuf.at[slot, pl.ds(r, 1)],
                         sems.at[slot], priority=r % DMA_QUEUES)


def _wait_row_gathers(ys_ref, ybuf, sems, slot):
    pltpu.make_async_copy(ys_ref.at[pl.ds(0, ybuf.shape[1])], ybuf.at[slot], sems.at[slot]).wait()


def _combined_residual(pos_ref, x_ref, rc_ref, gate_ref, ys_ref, ybuf, sems):
    tm = ybuf.shape[1]
    i = pl.program_id(0)
    n = pl.num_programs(0)

    @pl.when(i == 0)
    def _():
        _issue_row_gathers(pos_ref, ys_ref, ybuf, sems, 0, 0, range(tm))

    _wait_row_gathers(ys_ref, ybuf, sems, i % 2)
    ya, yb = _unpack_halves(ybuf[i % 2])
    rc = rc_ref[...].T
    x_new = x_ref[...] + gate_ref[...] * (rc[:, 2:3] * ya + rc[:, 3:4] * yb)

    def issue(k, n_slices):
        _issue_row_gathers(pos_ref, ys_ref, ybuf, sems, jnp.minimum(i + 1, n - 1), (i + 1) % 2,
                           range(k * tm // n_slices, (k + 1) * tm // n_slices))
        if k == n_slices - 1:
            @pl.when(i == n - 1)
            def _():
                _wait_row_gathers(ys_ref, ybuf, sems, (i + 1) % 2)

    return x_new, issue


def _combine_in_specs(tm, d, route_rows, first_tile, rows_per_mod):
    return [pl.BlockSpec((tm, d), lambda i, pos: (i, 0)),
            pl.BlockSpec((route_rows, tm), lambda i, pos: (0, first_tile + i)),
            pl.BlockSpec((None, 1, d), lambda i, pos: ((i * tm) // rows_per_mod, 0, 0)),
            pl.BlockSpec(memory_space=pl.ANY)]


def _combine_scratch(tm, d):
    return [pltpu.VMEM((2, tm, d), U32), pltpu.SemaphoreType.DMA((2,))]


def _mod_kernel(c_ref, w_ref, b_ref, o_ref):
    c = c_ref[...]
    a = (c * _sigmoid(c)).astype(BF16)
    o_ref[...] = jnp.dot(a, w_ref[...].astype(BF16), preferred_element_type=F32) + b_ref[...]


def _modulation(c_all, w_mod, b_mod):
    depth, d, n = w_mod.shape
    r = c_all.shape[0]
    nt = n // d
    return pl.pallas_call(
        _mod_kernel,
        out_shape=jax.ShapeDtypeStruct((depth, r, n), F32),
        grid=(depth, nt),
        in_specs=[pl.BlockSpec((r, d), lambda l, j: (0, 0)),
                  pl.BlockSpec((None, d, d), lambda l, j: (l, 0, j)),
                  pl.BlockSpec((None, 1, d), lambda l, j: (l, 0, j))],
        out_specs=pl.BlockSpec((None, r, d), lambda l, j: (l, 0, j)),
        compiler_params=_cparams("parallel", "parallel"),
        name="modulation",
    )(c_all, w_mod, b_mod.reshape(depth, 1, n))


def _inproj_kernel(*refs, body, n_in, combine):
    if combine:
        pos_ref, x_ref, rc_ref, gate_ref, ys_ref = refs[:5]
        ybuf, sems = refs[-2:]
        rest = refs[5:-2]
        x, issue = _combined_residual(pos_ref, x_ref, rc_ref, gate_ref, ys_ref, ybuf, sems)
        rest[3 + n_in][...] = x
        ins, outs = rest[:3 + n_in], rest[4 + n_in:]
    else:
        x = refs[0][...]
        issue = lambda k, n_slices: None
        ins, outs = refs[1:4 + n_in], refs[4 + n_in:]
    h = _rms_mod(x, ins[0][...], ins[1][...], ins[2][...]).astype(BF16)
    body(h, issue, *ins[3:], *outs)


def _inproj(body, name, x, pending, g, shift, scale, rows_per_mod, extra, extra_specs, out_shape, out_specs,
            tm):
    t, d = x.shape
    mod_map = lambda i, *_: ((i * tm) // rows_per_mod, 0, 0)
    in_specs = [pl.BlockSpec((1, d), lambda i, *_: (0, 0)), pl.BlockSpec((None, 1, d), mod_map),
                pl.BlockSpec((None, 1, d), mod_map)] + list(extra_specs)
    kern = functools.partial(_inproj_kernel, body=body, n_in=len(extra), combine=pending is not None)
    if pending is None:
        return pl.pallas_call(
            kern, out_shape=tuple(out_shape), grid=(t // tm,),
            in_specs=[pl.BlockSpec((tm, d), lambda i: (i, 0))] + in_specs, out_specs=tuple(out_specs),
            compiler_params=_cparams("parallel"), name=name,
        )(x, g, shift, scale, *extra)
    pos, rc, gate, ys = pending
    return pl.pallas_call(
        kern,
        out_shape=(jax.ShapeDtypeStruct((t, d), F32),) + tuple(out_shape),
        grid_spec=pltpu.PrefetchScalarGridSpec(
            num_scalar_prefetch=1, grid=(t // tm,),
            in_specs=_combine_in_specs(tm, d, rc.shape[0], 0, rows_per_mod) + in_specs,
            out_specs=(pl.BlockSpec((tm, d), lambda i, pos: (i, 0)),) + tuple(out_specs),
            scratch_shapes=_combine_scratch(tm, d)),
        compiler_params=_cparams("arbitrary"), name=name + "_combine",
    )(pos, x, rc, gate, ys, g, shift, scale, *extra)


def _inproj_even_body(h, issue, w_ref, gqk_ref, cos_ref, sin_ref, gm_ref, q_ref, k_ref, v_ref, a_ref, *, rope):
    tm = h.shape[0]
    lane = lax.broadcasted_iota(jnp.int32, (tm, LANES), 1)
    first_head = lane < HEAD_DIM
    first_half = (lane % HEAD_DIM) < (HEAD_DIM // 2)
    c0 = QK_WIDTH + KV_WIDTH
    cw = (w_ref.shape[1] - c0) // 2
    issue(0, 1)
    p = jnp.dot(h, w_ref[...], preferred_element_type=F32)
    for m in range(QK_WIDTH // LANES):
        y = p[:, LANES * m:LANES * (m + 1)]
        sq = y * y
        sq_hi = sq.astype(BF16)
        sq_lo = (sq - sq_hi.astype(F32)).astype(BF16)
        ms = jnp.dot(jnp.concatenate([sq_hi, sq_lo], axis=1), gm_ref[...], preferred_element_type=F32)
        y = y * lax.rsqrt(ms + EPS) * gqk_ref[:, LANES * m:LANES * (m + 1)]
        if rope:
            partner = jnp.where(first_half, pltpu.roll(y, LANES - HEAD_DIM // 2, 1),
                                pltpu.roll(y, HEAD_DIM // 2, 1))
            y = y * cos_ref[...] + partner * sin_ref[...]
        if m < ATTN_WIDTH // LANES:
            q_ref[m] = jnp.where(first_head, y, 0.0).astype(BF16)
            q_ref[m + ATTN_WIDTH // LANES] = jnp.where(first_head, 0.0, y).astype(BF16)
        else:
            k_ref[...] = y.astype(BF16)
    v = p[:, QK_WIDTH:c0]
    v_ref[0] = jnp.where(first_head, v, 1.0).astype(BF16)
    v_ref[1] = jnp.where(first_head, 1.0, v).astype(BF16)
    a_ref[...] = (p[:, c0:c0 + cw] * _sigmoid(p[:, c0 + cw:])).astype(BF16)


def _inproj_even(x, pending, g, shift, scale, rows_per_mod, w, gqk, cos_t, sin_t, gmat, tm, rope):
    t, d = x.shape
    n = w.shape[1]
    cw = (n - QK_WIDTH - KV_WIDTH) // 2
    n_pos = cos_t.shape[0] // tm
    return _inproj(
        functools.partial(_inproj_even_body, rope=rope), "inproj_even", x, pending, g, shift, scale,
        rows_per_mod, [w, gqk, cos_t, sin_t, gmat],
        [pl.BlockSpec((d, n), lambda i, *_: (0, 0)),
         pl.BlockSpec((1, QK_WIDTH), lambda i, *_: (0, 0)),
         pl.BlockSpec((tm, LANES), lambda i, *_: (i % n_pos, 0)),
         pl.BlockSpec((tm, LANES), lambda i, *_: (i % n_pos, 0)),
         pl.BlockSpec((2 * LANES, LANES), lambda i, *_: (0, 0))],
        [jax.ShapeDtypeStruct((N_HEADS, t, LANES), BF16), jax.ShapeDtypeStruct((t, KV_WIDTH), BF16),
         jax.ShapeDtypeStruct((N_KV_HEADS, t, LANES), BF16), jax.ShapeDtypeStruct((t, cw), BF16)],
        [pl.BlockSpec((N_HEADS, tm, LANES), lambda i, *_: (0, i, 0)),
         pl.BlockSpec((tm, KV_WIDTH), lambda i, *_: (i, 0)),
         pl.BlockSpec((N_KV_HEADS, tm, LANES), lambda i, *_: (0, i, 0)),
         pl.BlockSpec((tm, cw), lambda i, *_: (i, 0))],
        tm)


def _inproj_odd_body(h, issue, w_ref, gb_ref, u_ref):
    w = w_ref.shape[1] // 3
    n_phase = w // MXU_COLS
    proj = lambda lo: jnp.dot(h, w_ref[:, lo:lo + MXU_COLS], preferred_element_type=F32)
    for c in range(n_phase):
        issue(c, n_phase)
        cols = slice(MXU_COLS * c, MXU_COLS * (c + 1))
        gb_ref[:, cols] = proj(MXU_COLS * c).astype(BF16)
        u_ref[:, cols] = (proj(w + MXU_COLS * c) * proj(2 * w + MXU_COLS * c)).astype(BF16)


def _inproj_odd(x, pending, g, shift, scale, rows_per_mod, w, tm):
    t, d = x.shape
    n = w.shape[1]
    return _inproj(
        _inproj_odd_body, "inproj_odd", x, pending, g, shift, scale, rows_per_mod, [w],
        [pl.BlockSpec((d, n), lambda i, *_: (0, 0))],
        [jax.ShapeDtypeStruct((t, n // 3), BF16), jax.ShapeDtypeStruct((t, n // 3), BF16)],
        [pl.BlockSpec((tm, n // 3), lambda i, *_: (i, 0)), pl.BlockSpec((tm, n // 3), lambda i, *_: (i, 0))],
        tm)


def _attn_kernel(q_ref, k_ref, v_ref, o_ref, acc_ref):
    tq = q_ref.shape[1]
    first = lax.broadcasted_iota(jnp.int32, (tq, LANES), 1) < HEAD_DIM

    def pair(m, carry):
        outs = []
        for g in range(N_KV_HEADS):
            q = q_ref[m + GQA_GROUP * g]
            s = lax.dot_general(q, k_ref[...], (((1,), (1,)), ((), ())), preferred_element_type=F32)
            p = jnp.exp2(s - jnp.max(s, axis=-1, keepdims=True)).astype(BF16)
            o = jnp.dot(p, v_ref[g], preferred_element_type=F32)
            l = o[:, HEAD_DIM:HEAD_DIM + 1] if g == 0 else o[:, 0:1]
            outs.append(o / l)
        acc_ref[m] = jnp.where(first, outs[0], outs[1])
        return carry

    for m in range(GQA_GROUP):
        pair(m, 0)
    o_ref[...] = jnp.concatenate([acc_ref[m] for m in range(GQA_GROUP)], axis=1).astype(BF16)


def _attention(q, k_all, v_all, s_q, tq):
    t = q.shape[1]
    b, skv, _ = k_all.shape
    nq = s_q // tq
    return pl.pallas_call(
        _attn_kernel,
        out_shape=jax.ShapeDtypeStruct((t, ATTN_WIDTH), BF16),
        grid=(b, nq),
        in_specs=[pl.BlockSpec((N_HEADS, tq, LANES), lambda bi, i: (0, bi * nq + i, 0)),
                  pl.BlockSpec((None, skv, LANES), lambda bi, i: (bi, 0, 0)),
                  pl.BlockSpec((N_KV_HEADS, None, skv, LANES), lambda bi, i: (0, bi, 0, 0))],
        out_specs=pl.BlockSpec((tq, ATTN_WIDTH), lambda bi, i: (bi * nq + i, 0)),
        scratch_shapes=[pltpu.VMEM((GQA_GROUP, tq, LANES), F32)],
        compiler_params=_cparams("parallel", "parallel"),
        name="attention",
    )(q, k_all, v_all)


def _fill_window(win_ref, cur_ref, prev_ref, next_ref, is_first, is_last):
    tm = cur_ref.shape[0]
    win_ref[0:HALO, :] = jnp.where(is_first, 0.0, prev_ref[...].astype(F32))
    win_ref[HALO:HALO + tm, :] = cur_ref[...].astype(F32)
    win_ref[HALO + tm:2 * HALO + tm, :] = jnp.where(is_last, 0.0, next_ref[...].astype(F32))


def _fill_shifted(win_ref, sh_ref):
    n = sh_ref.shape[1]
    for s in range(1, SUBLANES):
        sh_ref[s - 1] = win_ref[s:s + n, :]


def _dwconv_chunk(win_ref, sh_ref, w_ref, r0, rows, taps):
    off = HALO - taps // 2
    acc = None
    for kk in range(taps):
        q, s = divmod(off + kk, SUBLANES)
        if sh_ref is None:
            src = win_ref[r0 + off + kk:r0 + off + kk + rows, :]
        elif s == 0:
            src = win_ref[r0 + SUBLANES * q:r0 + SUBLANES * q + rows, :]
        else:
            src = sh_ref[s - 1, r0 + SUBLANES * q:r0 + SUBLANES * q + rows, :]
        term = src.reshape(rows // SUBLANES, SUBLANES, -1) * w_ref[SUBLANES * kk:SUBLANES * (kk + 1), :]
        acc = term if acc is None else acc + term
    return acc.reshape(rows, -1)


def _residual_tail(x_ref, o, gate_ref, g2_ref, sh2_ref, sc2_ref, xo_ref, hp_ref):
    x_new = x_ref[...] + gate_ref[...] * o
    xo_ref[...] = x_new
    hp_ref[...] = _pack_halves(_rms_mod(x_new, g2_ref[...], sh2_ref[...], sc2_ref[...]))


def _outproj_even_kernel(x_ref, at_ref, a_ref, ap_ref, an_ref, dw_ref, db_ref, lg_ref, lb_ref, w_ref,
                         gate_ref, g2_ref, sh2_ref, sc2_ref, xo_ref, hp_ref, win_ref, cv_ref, shw_ref,
                         *, tiles_per_seq):
    tm = x_ref.shape[0]
    i = pl.program_id(0) % tiles_per_seq
    _fill_window(win_ref, a_ref, ap_ref, an_ref, i == 0, i == tiles_per_seq - 1)
    _fill_shifted(win_ref, shw_ref)
    for r0 in range(0, tm, CONV_ROWS):
        y = _dwconv_chunk(win_ref, shw_ref, dw_ref, r0, CONV_ROWS, CONV_A_K) + db_ref[...]
        mu = jnp.mean(y, axis=-1, keepdims=True)
        yc = y - mu
        var = jnp.mean(yc * yc, axis=-1, keepdims=True)
        yn = yc * lax.rsqrt(var + EPS) * lg_ref[...] + lb_ref[...]
        cv_ref[r0:r0 + CONV_ROWS, :] = (yn * _sigmoid(yn)).astype(BF16)
    aw = at_ref.shape[1]
    o = (jnp.dot(at_ref[...], w_ref[0:aw, :], preferred_element_type=F32)
         + jnp.dot(cv_ref[...], w_ref[aw:, :], preferred_element_type=F32))
    _residual_tail(x_ref, o, gate_ref, g2_ref, sh2_ref, sc2_ref, xo_ref, hp_ref)


def _outproj_odd_kernel(x_ref, gb_ref, u_ref, up_ref, un_ref, cw_ref, w_ref,
                        gate_ref, g2_ref, sh2_ref, sc2_ref, xo_ref, hp_ref, win_ref, cv_ref,
                        *, tiles_per_seq):
    tm = x_ref.shape[0]
    i = pl.program_id(0) % tiles_per_seq
    _fill_window(win_ref, u_ref, up_ref, un_ref, i == 0, i == tiles_per_seq - 1)
    for r0 in range(0, tm, CONV_ROWS):
        y = _dwconv_chunk(win_ref, None, cw_ref, r0, CONV_ROWS, SC_K)
        cv_ref[r0:r0 + CONV_ROWS, :] = (gb_ref[r0:r0 + CONV_ROWS, :].astype(F32) * y).astype(BF16)
    o = jnp.dot(cv_ref[...], w_ref[...], preferred_element_type=F32)
    _residual_tail(x_ref, o, gate_ref, g2_ref, sh2_ref, sc2_ref, xo_ref, hp_ref)


def _halo_specs(tm, width, n_rows):
    hb = tm // HALO
    last = n_rows // HALO - 1
    return [pl.BlockSpec((tm, width), lambda i: (i, 0)),
            pl.BlockSpec((HALO, width), lambda i: (jnp.maximum(i * hb - 1, 0), 0)),
            pl.BlockSpec((HALO, width), lambda i: (jnp.minimum((i + 1) * hb, last), 0))]


def _outproj(kernel_fn, name, x, lead, conv_in, params, w, gate, g2, shift2, scale2, rows_per_mod,
             seq_len, tm, shifted_window):
    t, d = x.shape
    cw = conv_in.shape[1]
    scratch = [pltpu.VMEM((tm + 2 * HALO, cw), F32), pltpu.VMEM((tm, cw), BF16)]
    if shifted_window:
        scratch.append(pltpu.VMEM((SUBLANES - 1, tm + 2 * HALO - SUBLANES, cw), F32))
    mod_map = lambda i: ((i * tm) // rows_per_mod, 0, 0)
    row_spec = lambda a: pl.BlockSpec((tm, a.shape[1]), lambda i: (i, 0))
    full_spec = lambda a: pl.BlockSpec(a.shape, lambda i: (0,) * a.ndim)
    in_specs = ([row_spec(x)] + [row_spec(a) for a in lead] + _halo_specs(tm, cw, t)
                + [full_spec(a) for a in params] + [full_spec(w)]
                + [pl.BlockSpec((None, 1, d), mod_map), full_spec(g2),
                   pl.BlockSpec((None, 1, d), mod_map), pl.BlockSpec((None, 1, d), mod_map)])
    return pl.pallas_call(
        functools.partial(kernel_fn, tiles_per_seq=seq_len // tm),
        out_shape=(jax.ShapeDtypeStruct((t, d), F32), jax.ShapeDtypeStruct((t, d // 2), U32)),
        grid=(t // tm,),
        in_specs=in_specs,
        out_specs=(pl.BlockSpec((tm, d), lambda i: (i, 0)), pl.BlockSpec((tm, d // 2), lambda i: (i, 0))),
        scratch_shapes=scratch,
        compiler_params=_cparams("parallel"),
        name=name,
    )(x, *lead, conv_in, conv_in, conv_in, *params, w, gate, g2, shift2, scale2)


def _part_tile_maps(part_tiles):
    maps, start = [], 0
    for n in part_tiles:
        maps.append(lambda i, *_, start=start, n=n: (jnp.clip(i - start, 0, n - 1), 0))
        start += n
    return maps


def _router_kernel(*refs, part_tiles):
    hp_refs = refs[:len(part_tiles)]
    rw_ref, bias_ref, tri_ref, route_ref, cnt_ref, base_ref = refs[len(part_tiles):]
    tm = hp_refs[0].shape[0]

    @pl.when(pl.program_id(0) == 0)
    def _():
        base_ref[...] = jnp.zeros_like(base_ref)

    hp, start = hp_refs[0][...], 0
    for k in range(1, len(part_tiles)):
        start += part_tiles[k - 1]
        hp = jnp.where(pl.program_id(0) >= start, hp_refs[k][...], hp)
    lo, hi = _unpack_halves(hp)
    h = jnp.concatenate([lo, hi], axis=1).astype(BF16)
    logits = jnp.dot(h, rw_ref[...], preferred_element_type=F32)
    lt = logits.T[0:N_EXPERTS, :]
    aff = _sigmoid(lt)
    sel = aff + bias_ref[...]
    rows = lambda a, e: a[e:e + 1, :]

    best = None
    for g in range(N_GROUPS):
        a, b, c, d = (rows(sel, EXPERTS_PER_GROUP * g + j) for j in range(EXPERTS_PER_GROUP))
        score = jnp.maximum(jnp.maximum(a + b, c + d), jnp.maximum(a, b) + jnp.maximum(c, d))
        if best is None:
            best, grp = score, jnp.zeros((1, tm), jnp.int32)
        else:
            better = score > best
            grp = jnp.where(better, g, grp)
            best = jnp.where(better, score, best)

    def in_group(a, j):
        out = rows(a, j)
        for g in range(1, N_GROUPS):
            out = jnp.where(grp == g, rows(a, EXPERTS_PER_GROUP * g + j), out)
        return out

    s = [in_group(sel, j) for j in range(EXPERTS_PER_GROUP)]
    f = [in_group(aff, j) for j in range(EXPERTS_PER_GROUP)]

    def first_argmax(vals):
        m = functools.reduce(jnp.maximum, vals)
        idx = jnp.full((1, tm), len(vals) - 1, jnp.int32)
        for j in range(len(vals) - 2, -1, -1):
            idx = jnp.where(vals[j] == m, j, idx)
        return idx

    i1 = first_argmax(s)
    i2 = first_argmax([jnp.where(i1 == j, -jnp.inf, s[j]) for j in range(EXPERTS_PER_GROUP)])
    e_lo = jnp.minimum(i1, i2)
    e_hi = jnp.maximum(i1, i2)
    pick = lambda vals, idx: functools.reduce(
        lambda acc, j: jnp.where(idx == j, vals[j], acc), range(1, len(vals)), vals[0])
    f_lo, f_hi = pick(f, e_lo), pick(f, e_hi)
    denom = f_lo + f_hi
    pair = jnp.where(e_lo == 0, e_hi - 1, jnp.where(e_lo == 1, e_hi + 1, N_PAIRS - 1))
    cls = grp * N_PAIRS + pair

    onehot = (lax.broadcasted_iota(jnp.int32, (CLASS_ROWS, tm), 0) == cls)
    onehot_f = onehot.astype(F32)
    before = jnp.dot(onehot_f.astype(BF16), tri_ref[...], preferred_element_type=F32) + base_ref[:, 0:1]
    rank = jnp.sum(onehot_f * before, axis=0, keepdims=True)
    base_ref[...] = base_ref[...] + jnp.sum(onehot_f, axis=1, keepdims=True)

    route_ref[...] = jnp.concatenate(
        [cls.astype(F32), rank, f_lo / denom, f_hi / denom, jnp.zeros((4, tm), F32)], axis=0)
    cnt_ref[...] = base_ref[...]


def _router(parts, rw, bias, tri, tm):
    hw = parts[0].shape[1]
    part_tiles = tuple(p.shape[0] // tm for p in parts)
    t = tm * sum(part_tiles)
    return pl.pallas_call(
        functools.partial(_router_kernel, part_tiles=part_tiles),
        out_shape=(jax.ShapeDtypeStruct((8, t), F32), jax.ShapeDtypeStruct((CLASS_ROWS, LANES), F32)),
        grid=(t // tm,),
        in_specs=[pl.BlockSpec((tm, hw), m) for m in _part_tile_maps(part_tiles)] + [
                  pl.BlockSpec(rw.shape, lambda i: (0, 0)),
                  pl.BlockSpec(bias.shape, lambda i: (0, 0)),
                  pl.BlockSpec((tm, tm), lambda i: (0, 0))],
        out_specs=(pl.BlockSpec((8, tm), lambda i: (0, i)),
                   pl.BlockSpec((CLASS_ROWS, LANES), lambda i: (0, 0))),
        scratch_shapes=[pltpu.VMEM((CLASS_ROWS, LANES), F32)],
        compiler_params=_cparams("arbitrary"),
        name="router",
    )(*parts, rw, bias, tri)


def _dispatch_kernel(pos_ref, *refs, part_tiles):
    hp_refs = refs[:len(part_tiles)]
    init_ref, hs_ref, sem = refs[len(part_tiles):]
    del init_ref
    tm = hp_refs[0].shape[0]
    i = pl.program_id(0)

    start = 0
    for hp_ref, n in zip(hp_refs, part_tiles):
        @pl.when((i >= start) & (i < start + n))
        def _(hp_ref=hp_ref):
            for r in range(tm):
                pltpu.async_copy(hp_ref.at[pl.ds(r, 1)], hs_ref.at[pl.ds(pos_ref[i * tm + r], 1)], sem,
                                 priority=r % DMA_QUEUES)
        start += n
    pltpu.make_async_copy(hp_refs[0], hs_ref.at[pl.ds(0, tm)], sem).wait()


def _dispatch(pos, parts, hs_init, tm):
    hw = parts[0].shape[1]
    part_tiles = tuple(p.shape[0] // tm for p in parts)
    n_rows = hs_init.shape[0]
    return pl.pallas_call(
        functools.partial(_dispatch_kernel, part_tiles=part_tiles),
        out_shape=jax.ShapeDtypeStruct((n_rows, hw), U32),
        grid_spec=pltpu.PrefetchScalarGridSpec(
            num_scalar_prefetch=1,
            grid=(sum(part_tiles),),
            in_specs=[pl.BlockSpec((tm, hw), m) for m in _part_tile_maps(part_tiles)]
            + [pl.BlockSpec(memory_space=pl.ANY)],
            out_specs=pl.BlockSpec(memory_space=pl.ANY),
            scratch_shapes=[pltpu.SemaphoreType.DMA(())]),
        input_output_aliases={1 + len(parts): 0},
        compiler_params=_cparams("arbitrary"),
        name="dispatch",
    )(pos, *parts, hs_init)


def _moe_kernel(elo_ref, ehi_ref, nused_ref, hs_ref, wg_lo, wu_lo, wd_lo, wg_hi, wu_hi, wd_hi, ys_ref):
    del elo_ref, ehi_ref

    @pl.when(pl.program_id(0) >= nused_ref[0])
    def _():
        ys_ref[...] = jnp.zeros_like(ys_ref)

    @pl.when(pl.program_id(0) < nused_ref[0])
    def _():
        lo, hi = _unpack_halves(hs_ref[...])
        h = jnp.concatenate([lo, hi], axis=1).astype(BF16)

        def expert(wg_ref, wu_ref, wd_ref):
            g = jnp.dot(h, wg_ref[...], preferred_element_type=F32)
            u = jnp.dot(h, wu_ref[...], preferred_element_type=F32)
            act = (g * _sigmoid(g) * u).astype(BF16)
            return jnp.dot(act, wd_ref[...], preferred_element_type=F32)

        ya = lax.bitcast_convert_type(expert(wg_lo, wu_lo, wd_lo).astype(BF16).astype(F32), U32)
        yb = lax.bitcast_convert_type(expert(wg_hi, wu_hi, wd_hi).astype(BF16).astype(F32), U32)
        ys_ref[...] = (ya >> 16) | (yb & jnp.uint32(0xFFFF0000))


def _moe(e_lo, e_hi, n_used, hs, wg, wu, wd, layer):
    n_rows, hw = hs.shape
    d = wd.shape[3]
    n_blocks = n_rows // MOE_BLOCK
    row_map = lambda j, lo, hi, nu: (jnp.minimum(j, nu[0] - 1), 0)
    w_spec = lambda w, which: pl.BlockSpec(
        (None, None) + w.shape[2:], lambda j, lo, hi, nu: (layer, (lo, hi)[which][j], 0, 0))
    return pl.pallas_call(
        _moe_kernel,
        out_shape=jax.ShapeDtypeStruct((n_rows, d), U32),
        grid_spec=pltpu.PrefetchScalarGridSpec(
            num_scalar_prefetch=3,
            grid=(n_blocks,),
            in_specs=[pl.BlockSpec((MOE_BLOCK, hw), row_map)]
            + [w_spec(w, which) for which in (0, 1) for w in (wg, wu, wd)],
            out_specs=pl.BlockSpec((MOE_BLOCK, d), lambda j, lo, hi, nu: (j, 0))),
        compiler_params=_cparams("arbitrary"),
        name="experts",
    )(e_lo, e_hi, n_used, hs, wg, wu, wd, wg, wu, wd)


def _combine_kernel(pos_ref, x_ref, rc_ref, gate_ref, ys_ref, xo_ref, ybuf, sems):
    x_new, issue = _combined_residual(pos_ref, x_ref, rc_ref, gate_ref, ys_ref, ybuf, sems)
    xo_ref[...] = x_new
    issue(0, 1)


def _combine(pos, x, rc, first_tile, gate, ys, rows_per_mod, tm):
    t, d = x.shape
    return pl.pallas_call(
        _combine_kernel,
        out_shape=jax.ShapeDtypeStruct((t, d), F32),
        grid_spec=pltpu.PrefetchScalarGridSpec(
            num_scalar_prefetch=1,
            grid=(t // tm,),
            in_specs=_combine_in_specs(tm, d, rc.shape[0], first_tile, rows_per_mod),
            out_specs=pl.BlockSpec((tm, d), lambda i, pos: (i, 0)),
            scratch_shapes=_combine_scratch(tm, d)),
        compiler_params=_cparams("arbitrary"),
        name="combine",
    )(pos, x, rc, gate, ys)


def _deinterleave(base):
    return [base + 2 * p for p in range(HEAD_DIM // 2)] + [base + 2 * p + 1 for p in range(HEAD_DIM // 2)]


def _even_in_perm(n_cols):
    cols = []
    half = N_HEADS // N_KV_HEADS
    for m in range(half):
        cols += _deinterleave(HEAD_DIM * m) + _deinterleave(HEAD_DIM * (m + half))
    for kh in range(N_KV_HEADS):
        cols += _deinterleave(ATTN_WIDTH + HEAD_DIM * kh)
    cols += list(range(QK_WIDTH, n_cols))
    return np.asarray(cols, np.int32)


def _even_out_perm(n_rows):
    rows = []
    half = N_HEADS // N_KV_HEADS
    for m in range(half):
        rows += list(range(HEAD_DIM * m, HEAD_DIM * (m + 1)))
        rows += list(range(HEAD_DIM * (m + half), HEAD_DIM * (m + half + 1)))
    rows += list(range(ATTN_WIDTH, n_rows))
    return np.asarray(rows, np.int32)


def _rope_tables(n_tokens):
    rows = n_tokens // GRID_W
    row = jnp.repeat(jnp.arange(rows, dtype=F32), GRID_W)
    col = jnp.tile(jnp.arange(GRID_W, dtype=F32), rows)
    inv = ROPE_THETA ** (-jnp.arange(ROPE_PAIRS_PER_AXIS, dtype=F32) / ROPE_PAIRS_PER_AXIS)
    ang = jnp.concatenate([row[:, None] * inv, col[:, None] * inv], axis=-1)
    cos, sin = jnp.cos(ang), jnp.sin(ang)
    reps = LANES // HEAD_DIM
    return (jnp.tile(jnp.concatenate([cos, cos], axis=-1), (1, reps)),
            jnp.tile(jnp.concatenate([-sin, sin], axis=-1), (1, reps)))


def _head_mean_matrix():
    blk = np.kron(np.eye(LANES // HEAD_DIM), np.full((HEAD_DIM, HEAD_DIM), 1.0 / HEAD_DIM))
    return jnp.asarray(np.concatenate([blk, blk], axis=0), BF16)


def _moe_layer(parts, hs_buf, router_wp, router_b, tri, expert_w, layer, tm):
    t = sum(p.shape[0] for p in parts)
    route, counts = _router(parts, router_wp, router_b, tri, tm)
    cls = route[0].astype(jnp.int32)
    rank = route[1].astype(jnp.int32)
    cnt = counts[:N_CLASSES, 0].astype(jnp.int32)
    padded = (cnt + MOE_BLOCK - 1) // MOE_BLOCK * MOE_BLOCK
    pad_end = jnp.cumsum(padded)
    pos = (pad_end - padded)[cls] + rank
    n_blocks = hs_buf.shape[0] // MOE_BLOCK
    assert n_blocks >= -(-t // MOE_BLOCK) + N_CLASSES
    n_used = pad_end[-1] // MOE_BLOCK
    blk = jnp.minimum(jnp.arange(n_blocks, dtype=jnp.int32), n_used - 1)
    bcls = jnp.sum((pad_end[None, :] <= (blk * MOE_BLOCK)[:, None]).astype(jnp.int32), axis=1)
    bcls = jnp.minimum(bcls, N_CLASSES - 1)
    e_lo = EXPERTS_PER_GROUP * (bcls // N_PAIRS) + jnp.asarray(PAIR_LO, jnp.int32)[bcls % N_PAIRS]
    e_hi = EXPERTS_PER_GROUP * (bcls // N_PAIRS) + jnp.asarray(PAIR_HI, jnp.int32)[bcls % N_PAIRS]
    hs = _dispatch(pos, parts, hs_buf, math.gcd(tm, ROW_TILE))
    ys = _moe(e_lo, e_hi, n_used.reshape(1).astype(jnp.int32), hs, *expert_w, layer)
    return pos, route, ys, hs


def kernel(x, c, ctx, c_ctx, w_mod, b_mod, norm_mix_g, norm_ffn_g, ev_w_in, ev_q_norm_g, ev_k_norm_g,
           ev_dw_w, ev_dw_b, ev_ln_g, ev_ln_b, ev_w_out, od_w_in, od_conv_w, od_w_out, router_w,
           router_bias, moe_w_gate, moe_w_up, moe_w_down):
    b, s, d = x.shape
    n_ctx = ctx.shape[1]
    depth = w_mod.shape[0]
    last_attn = (depth - 1) - ((depth - 1) % 2)
    tm = min(ROW_TILE, s)
    tmz = min(ROW_TILE, n_ctx)
    tq = min(Q_TILE, s)
    tqz = min(Q_TILE, n_ctx)
    assert s % tm == 0 and n_ctx % tmz == 0 and s % tq == 0 and n_ctx % tqz == 0 and s % GRID_W == 0

    r = -(-(b + 1) // 8) * 8
    c_all = jnp.zeros((r, d), F32).at[:b].set(c).at[b].set(c_ctx)
    mods = _modulation(c_all, w_mod, b_mod)

    def mod_x(l, j):
        return mods[l, :b, j * d:(j + 1) * d].reshape(b, 1, d)

    def mod_z(l, j):
        return mods[l, b:b + 1, j * d:(j + 1) * d].reshape(1, 1, d)

    cos_t, sin_t = _rope_tables(s)
    gmat = _head_mean_matrix()
    in_perm = _even_in_perm(ev_w_in.shape[2])
    out_perm = _even_out_perm(ev_w_out.shape[1])
    head_perm = np.asarray(_deinterleave(0), np.int32)
    router_wp = jnp.zeros((d, LANES), F32).at[:, :N_EXPERTS].set(router_w).astype(BF16)
    router_b = router_bias.astype(F32).reshape(N_EXPERTS, 1)
    tri_n = min(ROUTER_TILE, b * s)
    tri = jnp.asarray(np.triu(np.ones((tri_n, tri_n), np.float32), 1), BF16)

    xf = x.reshape(b * s, d)
    zf = ctx.reshape(b * n_ctx, d)
    expert_w = tuple(w.astype(BF16) for w in (moe_w_gate, moe_w_up, moe_w_down))
    pending = None
    ctm = min(Q_TILE, s, n_ctx)
    n_routed = b * s + (b * n_ctx if last_attn > 0 else 0)
    hs_buf = jnp.zeros(((-(-n_routed // MOE_BLOCK) + N_CLASSES) * MOE_BLOCK, d // 2), U32)
    for l in range(depth):
        i = l // 2
        ctx_in = l <= last_attn
        ctx_out = l < last_attn
        gmix = norm_mix_g[l].reshape(1, d)
        gffn = norm_ffn_g[l].reshape(1, d)
        if l % 2 == 0:
            w_in = ev_w_in[i][:, in_perm].astype(BF16)
            w_out = ev_w_out[i][out_perm, :].astype(BF16)
            gq = ev_q_norm_g[i][head_perm] * (ATTN_SCALE * LOG2E)
            gk = ev_k_norm_g[i][head_perm]
            gqk = jnp.concatenate([jnp.tile(gq, N_HEADS), jnp.tile(gk, N_KV_HEADS)]).reshape(1, QK_WIDTH)
            conv_params = [jnp.repeat(ev_dw_w[i], SUBLANES, axis=0), ev_dw_b[i].reshape(1, -1),
                           ev_ln_g[i].reshape(1, -1),
                           ev_ln_b[i].reshape(1, -1)]
            res = _inproj_even(xf, pending, gmix, mod_x(l, 0), mod_x(l, 1), s, w_in, gqk, cos_t, sin_t, gmat,
                               tm, True)
            if pending is not None:
                xf, res = res[0], res[1:]
            q, k, v, a = res
            k_all = k.reshape(b, s, KV_WIDTH)
            v_all = v.reshape(N_KV_HEADS, b, s, LANES)
            if ctx_in:
                qz, kz, vz, az = _inproj_even(zf, None, gmix, mod_z(l, 0), mod_z(l, 1), b * n_ctx, w_in, gqk,
                                              cos_t, sin_t, gmat, tmz, False)
                kz = kz.reshape(b, n_ctx, KV_WIDTH)
                vz = vz.reshape(N_KV_HEADS, b, n_ctx, LANES)
                k_all = jnp.concatenate([k_all, kz], axis=1)
                v_all = jnp.concatenate([v_all, vz], axis=2)
            attn = _attention(q, k_all, v_all, s, tq)
            xf, hp = _outproj(_outproj_even_kernel, "outproj_even", xf, [attn], a, conv_params, w_out,
                              mod_x(l, 2), gffn, mod_x(l, 3), mod_x(l, 4), s, s, tm, True)
            if ctx_out:
                attn_z = _attention(qz, kz, vz, n_ctx, tqz)
                zf, hpz = _outproj(_outproj_even_kernel, "outproj_even_ctx", zf, [attn_z], az, conv_params,
                                   w_out, mod_z(l, 2), gffn, mod_z(l, 3), mod_z(l, 4), b * n_ctx, n_ctx, tmz,
                                   True)
        else:
            w_in = od_w_in[i].astype(BF16)
            w_out = od_w_out[i].astype(BF16)
            conv_params = [jnp.repeat(od_conv_w[i], SUBLANES, axis=0)]
            res = _inproj_odd(xf, pending, gmix, mod_x(l, 0), mod_x(l, 1), s, w_in, tm)
            if pending is not None:
                xf, res = res[0], res[1:]
            gb, u = res
            xf, hp = _outproj(_outproj_odd_kernel, "outproj_odd", xf, [gb], u, conv_params, w_out,
                              mod_x(l, 2), gffn, mod_x(l, 3), mod_x(l, 4), s, s, tm, False)
            if ctx_out:
                gbz, uz = _inproj_odd(zf, None, gmix, mod_z(l, 0), mod_z(l, 1), b * n_ctx, w_in, tmz)
                zf, hpz = _outproj(_outproj_odd_kernel, "outproj_odd_ctx", zf, [gbz], uz, conv_params, w_out,
                                   mod_z(l, 2), gffn, mod_z(l, 3), mod_z(l, 4), b * n_ctx, n_ctx, tmz, False)

        parts = [hp, hpz] if ctx_out else [hp]
        rtm = math.gcd(tri_n, math.gcd(b * s, b * n_ctx)) if ctx_out else tri_n
        pos, rc, ys, hs_buf = _moe_layer(parts, hs_buf, router_wp, router_b, tri[:rtm, :rtm], expert_w, l, rtm)
        n_lat = b * s
        pending = (pos[:n_lat], rc, mod_x(l, 5), ys)
        if ctx_out:
            zf = _combine(pos[n_lat:], zf, rc, n_lat // ctm, mod_z(l, 5), ys, b * n_ctx, ctm)
    pos, rc, gate, ys = pending
    return _combine(pos, xf, rc, 0, gate, ys, s, ctm).reshape(b, s, d)
```

```python
import functools
import math

import numpy as np
import jax
import jax.numpy as jnp
from jax import lax
from jax.experimental import pallas as pl
from jax.experimental.pallas import tpu as pltpu

F32 = jnp.float32
BF16 = jnp.bfloat16
U32 = jnp.uint32

GRID_W = 64
EPS = 1e-6
N_HEADS = 8
N_KV_HEADS = 2
GQA_GROUP = N_HEADS // N_KV_HEADS
HEAD_DIM = 64
ATTN_WIDTH = N_HEADS * HEAD_DIM
KV_WIDTH = N_KV_HEADS * HEAD_DIM
QK_WIDTH = ATTN_WIDTH + KV_WIDTH
ATTN_SCALE = HEAD_DIM ** -0.5
ROPE_THETA = 10000.0
ROPE_PAIRS_PER_AXIS = HEAD_DIM // 4
CONV_A_K = 31
SC_K = 3
N_EXPERTS = 16
N_GROUPS = 4
EXPERTS_PER_GROUP = N_EXPERTS // N_GROUPS
D_EXPERT = 512
PAIR_LO = (0, 0, 0, 1, 1, 2)
PAIR_HI = (1, 2, 3, 2, 3, 3)
N_PAIRS = len(PAIR_LO)
N_CLASSES = N_GROUPS * N_PAIRS
CLASS_ROWS = 32

LANES = 128
SUBLANES = 8
MXU_COLS = 256
HALO = 16
VMEM_LIMIT = 48 * 1024 * 1024
ROW_TILE = 512
Q_TILE = 256
ROUTER_TILE = 1024
DMA_QUEUES = 2
MOE_BLOCK = 512
CONV_ROWS = 64
OUTPROJ_ROWS = 256
LOG2E = math.log2(math.e)


def _cparams(*sem):
    return pltpu.CompilerParams(dimension_semantics=sem, vmem_limit_bytes=VMEM_LIMIT)


def _sigmoid(x):
    return 1.0 / (1.0 + jnp.exp(-x))


def _rms_mod(x, g, shift, scale):
    y = x * lax.rsqrt(jnp.mean(x * x, axis=-1, keepdims=True) + EPS) * g
    return y * (1.0 + scale) + shift


def _pack_halves(h):
    w = h.shape[1] // 2
    lo = lax.bitcast_convert_type(h[:, :w].astype(BF16).astype(F32), U32)
    hi = lax.bitcast_convert_type(h[:, w:].astype(BF16).astype(F32), U32)
    return (lo >> 16) | (hi & jnp.uint32(0xFFFF0000))


def _unpack_halves(w):
    lo = lax.bitcast_convert_type(w << 16, F32)
    hi = lax.bitcast_convert_type(w & jnp.uint32(0xFFFF0000), F32)
    return lo, hi


def _issue_row_gathers(pos_ref, ys_ref, ybuf, sems, tile, slot, rows):
    tm = ybuf.shape[1]
    for r in rows:
        pltpu.async_copy(ys_ref.at[pl.ds(pos_ref[tile * tm + r], 1)], ybuf.at[slot, pl.ds(r, 1)],
                         sems.at[slot], priority=r % DMA_QUEUES)


def _wait_row_gathers(ys_ref, ybuf, sems, slot):
    pltpu.make_async_copy(ys_ref.at[pl.ds(0, ybuf.shape[1])], ybuf.at[slot], sems.at[slot]).wait()


def _combined_residual(pos_ref, x_ref, rc_ref, gate_ref, ys_ref, ybuf, sems):
    tm = ybuf.shape[1]
    i = pl.program_id(0)
    n = pl.num_programs(0)

    @pl.when(i == 0)
    def _():
        _issue_row_gathers(pos_ref, ys_ref, ybuf, sems, 0, 0, range(tm))

    _wait_row_gathers(ys_ref, ybuf, sems, i % 2)
    ya, yb = _unpack_halves(ybuf[i % 2])
    rc = rc_ref[...].T
    x_new = x_ref[...] + gate_ref[...] * (rc[:, 2:3] * ya + rc[:, 3:4] * yb)

    def issue(k, n_slices):
        _issue_row_gathers(pos_ref, ys_ref, ybuf, sems, jnp.minimum(i + 1, n - 1), (i + 1) % 2,
                           range(k * tm // n_slices, (k + 1) * tm // n_slices))
        if k == n_slices - 1:
            @pl.when(i == n - 1)
            def _():
                _wait_row_gathers(ys_ref, ybuf, sems, (i + 1) % 2)

    return x_new, issue


def _combine_in_specs(tm, d, route_rows, first_tile, rows_per_mod):
    return [pl.BlockSpec((tm, d), lambda i, pos: (i, 0)),
            pl.BlockSpec((route_rows, tm), lambda i, pos: (0, first_tile + i)),
            pl.BlockSpec((None, 1, d), lambda i, pos: ((i * tm) // rows_per_mod, 0, 0)),
            pl.BlockSpec(memory_space=pl.ANY)]


def _combine_scratch(tm, d):
    return [pltpu.VMEM((2, tm, d), U32), pltpu.SemaphoreType.DMA((2,))]


def _mod_kernel(c_ref, w_ref, b_ref, o_ref):
    c = c_ref[...]
    a = (c * _sigmoid(c)).astype(BF16)
    o_ref[...] = jnp.dot(a, w_ref[...].astype(BF16), preferred_element_type=F32) + b_ref[...]


def _modulation(c_all, w_mod, b_mod):
    depth, d, n = w_mod.shape
    r = c_all.shape[0]
    nt = n // d
    return pl.pallas_call(
        _mod_kernel,
        out_shape=jax.ShapeDtypeStruct((depth, r, n), F32),
        grid=(depth, nt),
        in_specs=[pl.BlockSpec((r, d), lambda l, j: (0, 0)),
                  pl.BlockSpec((None, d, d), lambda l, j: (l, 0, j)),
                  pl.BlockSpec((None, 1, d), lambda l, j: (l, 0, j))],
        out_specs=pl.BlockSpec((None, r, d), lambda l, j: (l, 0, j)),
        compiler_params=_cparams("parallel", "parallel"),
        name="modulation",
    )(c_all, w_mod, b_mod.reshape(depth, 1, n))


def _inproj_kernel(*refs, body, n_in, combine):
    if combine:
        pos_ref, x_ref, rc_ref, gate_ref, ys_ref = refs[:5]
        ybuf, sems = refs[-2:]
        rest = refs[5:-2]
        x, issue = _combined_residual(pos_ref, x_ref, rc_ref, gate_ref, ys_ref, ybuf, sems)
        rest[3 + n_in][...] = x
        ins, outs = rest[:3 + n_in], rest[4 + n_in:]
    else:
        x = refs[0][...]
        issue = lambda k, n_slices: None
        ins, outs = refs[1:4 + n_in], refs[4 + n_in:]
    h = _rms_mod(x, ins[0][...], ins[1][...], ins[2][...]).astype(BF16)
    body(h, issue, *ins[3:], *outs)


def _inproj(body, name, x, pending, g, shift, scale, rows_per_mod, extra, extra_specs, out_shape, out_specs,
            tm):
    t, d = x.shape
    mod_map = lambda i, *_: ((i * tm) // rows_per_mod, 0, 0)
    in_specs = [pl.BlockSpec((1, d), lambda i, *_: (0, 0)), pl.BlockSpec((None, 1, d), mod_map),
                pl.BlockSpec((None, 1, d), mod_map)] + list(extra_specs)
    kern = functools.partial(_inproj_kernel, body=body, n_in=len(extra), combine=pending is not None)
    if pending is None:
        return pl.pallas_call(
            kern, out_shape=tuple(out_shape), grid=(t // tm,),
            in_specs=[pl.BlockSpec((tm, d), lambda i: (i, 0))] + in_specs, out_specs=tuple(out_specs),
            compiler_params=_cparams("parallel"), name=name,
        )(x, g, shift, scale, *extra)
    pos, rc, gate, ys = pending
    return pl.pallas_call(
        kern,
        out_shape=(jax.ShapeDtypeStruct((t, d), F32),) + tuple(out_shape),
        grid_spec=pltpu.PrefetchScalarGridSpec(
            num_scalar_prefetch=1, grid=(t // tm,),
            in_specs=_combine_in_specs(tm, d, rc.shape[0], 0, rows_per_mod) + in_specs,
            out_specs=(pl.BlockSpec((tm, d), lambda i, pos: (i, 0)),) + tuple(out_specs),
            scratch_shapes=_combine_scratch(tm, d)),
        compiler_params=_cparams("arbitrary"), name=name + "_combine",
    )(pos, x, rc, gate, ys, g, shift, scale, *extra)


def _inproj_even_body(h, issue, w_ref, gqk_ref, cos_ref, sin_ref, gm_ref, q_ref, k_ref, v_ref, a_ref, *, rope):
    tm = h.shape[0]
    lane = lax.broadcasted_iota(jnp.int32, (tm, LANES), 1)
    first_head = lane < HEAD_DIM
    first_half = (lane % HEAD_DIM) < (HEAD_DIM // 2)
    c0 = QK_WIDTH + KV_WIDTH
    cw = (w_ref.shape[1] - c0) // 2
    issue(0, 1)
    p = jnp.dot(h, w_ref[...], preferred_element_type=F32)
    for m in range(QK_WIDTH // LANES):
        y = p[:, LANES * m:LANES * (m + 1)]
        sq = y * y
        sq_hi = sq.astype(BF16)
        sq_lo = (sq - sq_hi.astype(F32)).astype(BF16)
        ms = jnp.dot(jnp.concatenate([sq_hi, sq_lo], axis=1), gm_ref[...], preferred_element_type=F32)
        y = y * lax.rsqrt(ms + EPS) * gqk_ref[:, LANES * m:LANES * (m + 1)]
        if rope:
            partner = jnp.where(first_half, pltpu.roll(y, LANES - HEAD_DIM // 2, 1),
                                pltpu.roll(y, HEAD_DIM // 2, 1))
            y = y * cos_ref[...] + partner * sin_ref[...]
        if m < ATTN_WIDTH // LANES:
            q_ref[m] = jnp.where(first_head, y, 0.0).astype(BF16)
            q_ref[m + ATTN_WIDTH // LANES] = jnp.where(first_head, 0.0, y).astype(BF16)
        else:
            k_ref[...] = y.astype(BF16)
    v = p[:, QK_WIDTH:c0]
    v_ref[0] = jnp.where(first_head, v, 1.0).astype(BF16)
    v_ref[1] = jnp.where(first_head, 1.0, v).astype(BF16)
    a_ref[...] = (p[:, c0:c0 + cw] * _sigmoid(p[:, c0 + cw:])).astype(BF16)


def _inproj_even(x, pending, g, shift, scale, rows_per_mod, w, gqk, cos_t, sin_t, gmat, tm, rope):
    t, d = x.shape
    n = w.shape[1]
    cw = (n - QK_WIDTH - KV_WIDTH) // 2
    n_pos = cos_t.shape[0] // tm
    return _inproj(
        functools.partial(_inproj_even_body, rope=rope), "inproj_even", x, pending, g, shift, scale,
        rows_per_mod, [w, gqk, cos_t, sin_t, gmat],
        [pl.BlockSpec((d, n), lambda i, *_: (0, 0)),
         pl.BlockSpec((1, QK_WIDTH), lambda i, *_: (0, 0)),
         pl.BlockSpec((tm, LANES), lambda i, *_: (i % n_pos, 0)),
         pl.BlockSpec((tm, LANES), lambda i, *_: (i % n_pos, 0)),
         pl.BlockSpec((2 * LANES, LANES), lambda i, *_: (0, 0))],
        [jax.ShapeDtypeStruct((N_HEADS, t, LANES), BF16), jax.ShapeDtypeStruct((t, KV_WIDTH), BF16),
         jax.ShapeDtypeStruct((N_KV_HEADS, t, LANES), BF16), jax.ShapeDtypeStruct((t, cw), BF16)],
        [pl.BlockSpec((N_HEADS, tm, LANES), lambda i, *_: (0, i, 0)),
         pl.BlockSpec((tm, KV_WIDTH), lambda i, *_: (i, 0)),
         pl.BlockSpec((N_KV_HEADS, tm, LANES), lambda i, *_: (0, i, 0)),
         pl.BlockSpec((tm, cw), lambda i, *_: (i, 0))],
        tm)


def _inproj_odd_body(h, issue, w_ref, gb_ref, u_ref):
    w = w_ref.shape[1] // 3
    n_phase = w // MXU_COLS
    proj = lambda lo: jnp.dot(h, w_ref[:, lo:lo + MXU_COLS], preferred_element_type=F32)
    for c in range(n_phase):
        issue(c, n_phase)
        cols = slice(MXU_COLS * c, MXU_COLS * (c + 1))
        gb_ref[:, cols] = proj(MXU_COLS * c).astype(BF16)
        u_ref[:, cols] = (proj(w + MXU_COLS * c) * proj(2 * w + MXU_COLS * c)).astype(BF16)


def _inproj_odd(x, pending, g, shift, scale, rows_per_mod, w, tm):
    t, d = x.shape
    n = w.shape[1]
    return _inproj(
        _inproj_odd_body, "inproj_odd", x, pending, g, shift, scale, rows_per_mod, [w],
        [pl.BlockSpec((d, n), lambda i, *_: (0, 0))],
        [jax.ShapeDtypeStruct((t, n // 3), BF16), jax.ShapeDtypeStruct((t, n // 3), BF16)],
        [pl.BlockSpec((tm, n // 3), lambda i, *_: (i, 0)), pl.BlockSpec((tm, n // 3), lambda i, *_: (i, 0))],
        tm)


def _attn_kernel(q_ref, k_ref, v_ref, o_ref, acc_ref):
    tq = q_ref.shape[1]
    first = lax.broadcasted_iota(jnp.int32, (tq, LANES), 1) < HEAD_DIM

    def pair(m, carry):
        outs = []
        for g in range(N_KV_HEADS):
            q = q_ref[m + GQA_GROUP * g]
            s = lax.dot_general(q, k_ref[...], (((1,), (1,)), ((), ())), preferred_element_type=F32)
            p = jnp.exp2(s - jnp.max(s, axis=-1, keepdims=True)).astype(BF16)
            o = jnp.dot(p, v_ref[g], preferred_element_type=F32)
            l = o[:, HEAD_DIM:HEAD_DIM + 1] if g == 0 else o[:, 0:1]
            outs.append(o / l)
        acc_ref[m] = jnp.where(first, outs[0], outs[1])
        return carry

    for m in range(GQA_GROUP):
        pair(m, 0)
    o_ref[...] = jnp.concatenate([acc_ref[m] for m in range(GQA_GROUP)], axis=1).astype(BF16)


def _attention(q, k_all, v_all, s_q, tq):
    t = q.shape[1]
    b, skv, _ = k_all.shape
    nq = s_q // tq
    return pl.pallas_call(
        _attn_kernel,
        out_shape=jax.ShapeDtypeStruct((t, ATTN_WIDTH), BF16),
        grid=(b, nq),
        in_specs=[pl.BlockSpec((N_HEADS, tq, LANES), lambda bi, i: (0, bi * nq + i, 0)),
                  pl.BlockSpec((None, skv, LANES), lambda bi, i: (bi, 0, 0)),
                  pl.BlockSpec((N_KV_HEADS, None, skv, LANES), lambda bi, i: (0, bi, 0, 0))],
        out_specs=pl.BlockSpec((tq, ATTN_WIDTH), lambda bi, i: (bi * nq + i, 0)),
        scratch_shapes=[pltpu.VMEM((GQA_GROUP, tq, LANES), F32)],
        compiler_params=_cparams("parallel", "parallel"),
        name="attention",
    )(q, k_all, v_all)


def _fill_window(win_ref, cur_ref, prev_ref, next_ref, is_first, is_last):
    tm = cur_ref.shape[0]
    win_ref[0:HALO, :] = jnp.where(is_first, 0.0, prev_ref[...].astype(F32))
    win_ref[HALO:HALO + tm, :] = cur_ref[...].astype(F32)
    win_ref[HALO + tm:2 * HALO + tm, :] = jnp.where(is_last, 0.0, next_ref[...].astype(F32))


def _fill_shifted(win_ref, sh_ref):
    n = sh_ref.shape[1]
    for s in range(1, SUBLANES):
        sh_ref[s - 1] = win_ref[s:s + n, :]


def _dwconv_chunk(win_ref, sh_ref, w_ref, r0, rows, taps):
    off = HALO - taps // 2
    acc = None
    for kk in range(taps):
        q, s = divmod(off + kk, SUBLANES)
        if sh_ref is None:
            src = win_ref[r0 + off + kk:r0 + off + kk + rows, :]
        elif s == 0:
            src = win_ref[r0 + SUBLANES * q:r0 + SUBLANES * q + rows, :]
        else:
            src = sh_ref[s - 1, r0 + SUBLANES * q:r0 + SUBLANES * q + rows, :]
        term = src.reshape(rows // SUBLANES, SUBLANES, -1) * w_ref[SUBLANES * kk:SUBLANES * (kk + 1), :]
        acc = term if acc is None else acc + term
    return acc.reshape(rows, -1)


def _residual_tail(x_ref, o, gate_ref, g2_ref, sh2_ref, sc2_ref, xo_ref, hp_ref, rows=slice(None)):
    x_new = x_ref[rows, :] + gate_ref[...] * o
    xo_ref[rows, :] = x_new
    hp_ref[rows, :] = _pack_halves(_rms_mod(x_new, g2_ref[...], sh2_ref[...], sc2_ref[...]))


def _outproj_even_kernel(x_ref, at_ref, a_ref, ap_ref, an_ref, dw_ref, db_ref, lg_ref, lb_ref, w_ref,
                         gate_ref, g2_ref, sh2_ref, sc2_ref, xo_ref, hp_ref, win_ref, cv_ref, shw_ref,
                         *, tiles_per_seq):
    tm = x_ref.shape[0]
    i = pl.program_id(0) % tiles_per_seq
    _fill_window(win_ref, a_ref, ap_ref, an_ref, i == 0, i == tiles_per_seq - 1)
    _fill_shifted(win_ref, shw_ref)
    aw = at_ref.shape[1]
    blk = min(OUTPROJ_ROWS, tm)
    for b0 in range(0, tm, blk):
        rows = slice(b0, b0 + blk)
        for r0 in range(b0, b0 + blk, CONV_ROWS):
            y = _dwconv_chunk(win_ref, shw_ref, dw_ref, r0, CONV_ROWS, CONV_A_K) + db_ref[...]
            mu = jnp.mean(y, axis=-1, keepdims=True)
            yc = y - mu
            var = jnp.mean(yc * yc, axis=-1, keepdims=True)
            yn = yc * lax.rsqrt(var + EPS) * lg_ref[...] + lb_ref[...]
            cv_ref[r0:r0 + CONV_ROWS, :] = (yn * _sigmoid(yn)).astype(BF16)
        o = (jnp.dot(at_ref[rows, :], w_ref[0:aw, :], preferred_element_type=F32)
             + jnp.dot(cv_ref[rows, :], w_ref[aw:, :], preferred_element_type=F32))
        _residual_tail(x_ref, o, gate_ref, g2_ref, sh2_ref, sc2_ref, xo_ref, hp_ref, rows)


def _outproj_odd_kernel(x_ref, gb_ref, u_ref, up_ref, un_ref, cw_ref, w_ref,
                        gate_ref, g2_ref, sh2_ref, sc2_ref, xo_ref, hp_ref, win_ref, cv_ref,
                        *, tiles_per_seq):
    tm = x_ref.shape[0]
    i = pl.program_id(0) % tiles_per_seq
    _fill_window(win_ref, u_ref, up_ref, un_ref, i == 0, i == tiles_per_seq - 1)
    for r0 in range(0, tm, CONV_ROWS):
        y = _dwconv_chunk(win_ref, None, cw_ref, r0, CONV_ROWS, SC_K)
        cv_ref[r0:r0 + CONV_ROWS, :] = (gb_ref[r0:r0 + CONV_ROWS, :].astype(F32) * y).astype(BF16)
    o = jnp.dot(cv_ref[...], w_ref[...], preferred_element_type=F32)
    _residual_tail(x_ref, o, gate_ref, g2_ref, sh2_ref, sc2_ref, xo_ref, hp_ref)


def _halo_specs(tm, width, n_rows):
    hb = tm // HALO
    last = n_rows // HALO - 1
    return [pl.BlockSpec((tm, width), lambda i: (i, 0)),
            pl.BlockSpec((HALO, width), lambda i: (jnp.maximum(i * hb - 1, 0), 0)),
            pl.BlockSpec((HALO, width), lambda i: (jnp.minimum((i + 1) * hb, last), 0))]


def _outproj(kernel_fn, name, x, lead, conv_in, params, w, gate, g2, shift2, scale2, rows_per_mod,
             seq_len, tm, shifted_window):
    t, d = x.shape
    cw = conv_in.shape[1]
    scratch = [pltpu.VMEM((tm + 2 * HALO, cw), F32), pltpu.VMEM((tm, cw), BF16)]
    if shifted_window:
        scratch.append(pltpu.VMEM((SUBLANES - 1, tm + 2 * HALO - SUBLANES, cw), F32))
    mod_map = lambda i: ((i * tm) // rows_per_mod, 0, 0)
    row_spec = lambda a: pl.BlockSpec((tm, a.shape[1]), lambda i: (i, 0))
    full_spec = lambda a: pl.BlockSpec(a.shape, lambda i: (0,) * a.ndim)
    in_specs = ([row_spec(x)] + [row_spec(a) for a in lead] + _halo_specs(tm, cw, t)
                + [full_spec(a) for a in params] + [full_spec(w)]
                + [pl.BlockSpec((None, 1, d), mod_map), full_spec(g2),
                   pl.BlockSpec((None, 1, d), mod_map), pl.BlockSpec((None, 1, d), mod_map)])
    return pl.pallas_call(
        functools.partial(kernel_fn, tiles_per_seq=seq_len // tm),
        out_shape=(jax.ShapeDtypeStruct((t, d), F32), jax.ShapeDtypeStruct((t, d // 2), U32)),
        grid=(t // tm,),
        in_specs=in_specs,
        out_specs=(pl.BlockSpec((tm, d), lambda i: (i, 0)), pl.BlockSpec((tm, d // 2), lambda i: (i, 0))),
        scratch_shapes=scratch,
        compiler_params=_cparams("parallel"),
        name=name,
    )(x, *lead, conv_in, conv_in, conv_in, *params, w, gate, g2, shift2, scale2)


def _part_tile_maps(part_tiles):
    maps, start = [], 0
    for n in part_tiles:
        maps.append(lambda i, *_, start=start, n=n: (jnp.clip(i - start, 0, n - 1), 0))
        start += n
    return maps


def _router_kernel(*refs, part_tiles):
    hp_refs = refs[:len(part_tiles)]
    rw_ref, bias_ref, tri_ref, route_ref, cnt_ref, base_ref = refs[len(part_tiles):]
    tm = hp_refs[0].shape[0]

    @pl.when(pl.program_id(0) == 0)
    def _():
        base_ref[...] = jnp.zeros_like(base_ref)

    hp, start = hp_refs[0][...], 0
    for k in range(1, len(part_tiles)):
        start += part_tiles[k - 1]
        hp = jnp.where(pl.program_id(0) >= start, hp_refs[k][...], hp)
    lo, hi = _unpack_halves(hp)
    h = jnp.concatenate([lo, hi], axis=1).astype(BF16)
    logits = jnp.dot(h, rw_ref[...], preferred_element_type=F32)
    lt = logits.T[0:N_EXPERTS, :]
    aff = _sigmoid(lt)
    sel = aff + bias_ref[...]
    rows = lambda a, e: a[e:e + 1, :]

    best = None
    for g in range(N_GROUPS):
        a, b, c, d = (rows(sel, EXPERTS_PER_GROUP * g + j) for j in range(EXPERTS_PER_GROUP))
        score = jnp.maximum(jnp.maximum(a + b, c + d), jnp.maximum(a, b) + jnp.maximum(c, d))
        if best is None:
            best, grp = score, jnp.zeros((1, tm), jnp.int32)
        else:
            better = score > best
            grp = jnp.where(better, g, grp)
            best = jnp.where(better, score, best)

    def in_group(a, j):
        out = rows(a, j)
        for g in range(1, N_GROUPS):
            out = jnp.where(grp == g, rows(a, EXPERTS_PER_GROUP * g + j), out)
        return out

    s = [in_group(sel, j) for j in range(EXPERTS_PER_GROUP)]
    f = [in_group(aff, j) for j in range(EXPERTS_PER_GROUP)]

    def first_argmax(vals):
        m = functools.reduce(jnp.maximum, vals)
        idx = jnp.full((1, tm), len(vals) - 1, jnp.int32)
        for j in range(len(vals) - 2, -1, -1):
            idx = jnp.where(vals[j] == m, j, idx)
        return idx

    i1 = first_argmax(s)
    i2 = first_argmax([jnp.where(i1 == j, -jnp.inf, s[j]) for j in range(EXPERTS_PER_GROUP)])
    e_lo = jnp.minimum(i1, i2)
    e_hi = jnp.maximum(i1, i2)
    pick = lambda vals, idx: functools.reduce(
        lambda acc, j: jnp.where(idx == j, vals[j], acc), range(1, len(vals)), vals[0])
    f_lo, f_hi = pick(f, e_lo), pick(f, e_hi)
    denom = f_lo + f_hi
    pair = jnp.where(e_lo == 0, e_hi - 1, jnp.where(e_lo == 1, e_hi + 1, N_PAIRS - 1))
    cls = grp * N_PAIRS + pair

    onehot = (lax.broadcasted_iota(jnp.int32, (CLASS_ROWS, tm), 0) == cls)
    onehot_f = onehot.astype(F32)
    before = jnp.dot(onehot_f.astype(BF16), tri_ref[...], preferred_element_type=F32) + base_ref[:, 0:1]
    rank = jnp.sum(onehot_f * before, axis=0, keepdims=True)
    base_ref[...] = base_ref[...] + jnp.sum(onehot_f, axis=1, keepdims=True)

    route_ref[...] = jnp.concatenate(
        [cls.astype(F32), rank, f_lo / denom, f_hi / denom, jnp.zeros((4, tm), F32)], axis=0)
    cnt_ref[...] = base_ref[...]


def _router(parts, rw, bias, tri, tm):
    hw = parts[0].shape[1]
    part_tiles = tuple(p.shape[0] // tm for p in parts)
    t = tm * sum(part_tiles)
    return pl.pallas_call(
        functools.partial(_router_kernel, part_tiles=part_tiles),
        out_shape=(jax.ShapeDtypeStruct((8, t), F32), jax.ShapeDtypeStruct((CLASS_ROWS, LANES), F32)),
        grid=(t // tm,),
        in_specs=[pl.BlockSpec((tm, hw), m) for m in _part_tile_maps(part_tiles)] + [
                  pl.BlockSpec(rw.shape, lambda i: (0, 0)),
                  pl.BlockSpec(bias.shape, lambda i: (0, 0)),
                  pl.BlockSpec((tm, tm), lambda i: (0, 0))],
        out_specs=(pl.BlockSpec((8, tm), lambda i: (0, i)),
                   pl.BlockSpec((CLASS_ROWS, LANES), lambda i: (0, 0))),
        scratch_shapes=[pltpu.VMEM((CLASS_ROWS, LANES), F32)],
        compiler_params=_cparams("arbitrary"),
        name="router",
    )(*parts, rw, bias, tri)


def _dispatch_kernel(pos_ref, *refs, part_tiles):
    hp_refs = refs[:len(part_tiles)]
    init_ref, hs_ref, sem = refs[len(part_tiles):]
    del init_ref
    tm = hp_refs[0].shape[0]
    i = pl.program_id(0)

    start = 0
    for hp_ref, n in zip(hp_refs, part_tiles):
        @pl.when((i >= start) & (i < start + n))
        def _(hp_ref=hp_ref):
            for r in range(tm):
                pltpu.async_copy(hp_ref.at[pl.ds(r, 1)], hs_ref.at[pl.ds(pos_ref[i * tm + r], 1)], sem,
                                 priority=r % DMA_QUEUES)
        start += n
    pltpu.make_async_copy(hp_refs[0], hs_ref.at[pl.ds(0, tm)], sem).wait()


def _dispatch(pos, parts, hs_init, tm):
    hw = parts[0].shape[1]
    part_tiles = tuple(p.shape[0] // tm for p in parts)
    n_rows = hs_init.shape[0]
    return pl.pallas_call(
        functools.partial(_dispatch_kernel, part_tiles=part_tiles),
        out_shape=jax.ShapeDtypeStruct((n_rows, hw), U32),
        grid_spec=pltpu.PrefetchScalarGridSpec(
            num_scalar_prefetch=1,
            grid=(sum(part_tiles),),
            in_specs=[pl.BlockSpec((tm, hw), m) for m in _part_tile_maps(part_tiles)]
            + [pl.BlockSpec(memory_space=pl.ANY)],
            out_specs=pl.BlockSpec(memory_space=pl.ANY),
            scratch_shapes=[pltpu.SemaphoreType.DMA(())]),
        input_output_aliases={1 + len(parts): 0},
        compiler_params=_cparams("arbitrary"),
        name="dispatch",
    )(pos, *parts, hs_init)


def _moe_kernel(elo_ref, ehi_ref, nused_ref, hs_ref, wg_lo, wu_lo, wd_lo, wg_hi, wu_hi, wd_hi, ys_ref):
    del elo_ref, ehi_ref

    @pl.when(pl.program_id(0) >= nused_ref[0])
    def _():
        ys_ref[...] = jnp.zeros_like(ys_ref)

    @pl.when(pl.program_id(0) < nused_ref[0])
    def _():
        lo, hi = _unpack_halves(hs_ref[...])
        h = jnp.concatenate([lo, hi], axis=1).astype(BF16)

        def expert(wg_ref, wu_ref, wd_ref):
            g = jnp.dot(h, wg_ref[...], preferred_element_type=F32)
            u = jnp.dot(h, wu_ref[...], preferred_element_type=F32)
            act = (g * _sigmoid(g) * u).astype(BF16)
            return jnp.dot(act, wd_ref[...], preferred_element_type=F32)

        ya = lax.bitcast_convert_type(expert(wg_lo, wu_lo, wd_lo).astype(BF16).astype(F32), U32)
        yb = lax.bitcast_convert_type(expert(wg_hi, wu_hi, wd_hi).astype(BF16).astype(F32), U32)
        ys_ref[...] = (ya >> 16) | (yb & jnp.uint32(0xFFFF0000))


def _moe(e_lo, e_hi, n_used, hs, wg, wu, wd, layer):
    n_rows, hw = hs.shape
    d = wd.shape[3]
    n_blocks = n_rows // MOE_BLOCK
    row_map = lambda j, lo, hi, nu: (jnp.minimum(j, nu[0] - 1), 0)
    w_spec = lambda w, which: pl.BlockSpec(
        (None, None) + w.shape[2:], lambda j, lo, hi, nu: (layer, (lo, hi)[which][j], 0, 0))
    return pl.pallas_call(
        _moe_kernel,
        out_shape=jax.ShapeDtypeStruct((n_rows, d), U32),
        grid_spec=pltpu.PrefetchScalarGridSpec(
            num_scalar_prefetch=3,
            grid=(n_blocks,),
            in_specs=[pl.BlockSpec((MOE_BLOCK, hw), row_map)]
            + [w_spec(w, which) for which in (0, 1) for w in (wg, wu, wd)],
            out_specs=pl.BlockSpec((MOE_BLOCK, d), lambda j, lo, hi, nu: (j, 0))),
        compiler_params=_cparams("arbitrary"),
        name="experts",
    )(e_lo, e_hi, n_used, hs, wg, wu, wd, wg, wu, wd)


def _combine_kernel(pos_ref, x_ref, rc_ref, gate_ref, ys_ref, xo_ref, ybuf, sems):
    x_new, issue = _combined_residual(pos_ref, x_ref, rc_ref, gate_ref, ys_ref, ybuf, sems)
    xo_ref[...] = x_new
    issue(0, 1)


def _combine(pos, x, rc, first_tile, gate, ys, rows_per_mod, tm):
    t, d = x.shape
    return pl.pallas_call(
        _combine_kernel,
        out_shape=jax.ShapeDtypeStruct((t, d), F32),
        grid_spec=pltpu.PrefetchScalarGridSpec(
            num_scalar_prefetch=1,
            grid=(t // tm,),
            in_specs=_combine_in_specs(tm, d, rc.shape[0], first_tile, rows_per_mod),
            out_specs=pl.BlockSpec((tm, d), lambda i, pos: (i, 0)),
            scratch_shapes=_combine_scratch(tm, d)),
        compiler_params=_cparams("arbitrary"),
        name="combine",
    )(pos, x, rc, gate, ys)


def _deinterleave(base):
    return [base + 2 * p for p in range(HEAD_DIM // 2)] + [base + 2 * p + 1 for p in range(HEAD_DIM // 2)]


def _even_in_perm(n_cols):
    cols = []
    half = N_HEADS // N_KV_HEADS
    for m in range(half):
        cols += _deinterleave(HEAD_DIM * m) + _deinterleave(HEAD_DIM * (m + half))
    for kh in range(N_KV_HEADS):
        cols += _deinterleave(ATTN_WIDTH + HEAD_DIM * kh)
    cols += list(range(QK_WIDTH, n_cols))
    return np.asarray(cols, np.int32)


def _even_out_perm(n_rows):
    rows = []
    half = N_HEADS // N_KV_HEADS
    for m in range(half):
        rows += list(range(HEAD_DIM * m, HEAD_DIM * (m + 1)))
        rows += list(range(HEAD_DIM * (m + half), HEAD_DIM * (m + half + 1)))
    rows += list(range(ATTN_WIDTH, n_rows))
    return np.asarray(rows, np.int32)


def _rope_tables(n_tokens):
    rows = n_tokens // GRID_W
    row = jnp.repeat(jnp.arange(rows, dtype=F32), GRID_W)
    col = jnp.tile(jnp.arange(GRID_W, dtype=F32), rows)
    inv = ROPE_THETA ** (-jnp.arange(ROPE_PAIRS_PER_AXIS, dtype=F32) / ROPE_PAIRS_PER_AXIS)
    ang = jnp.concatenate([row[:, None] * inv, col[:, None] * inv], axis=-1)
    cos, sin = jnp.cos(ang), jnp.sin(ang)
    reps = LANES // HEAD_DIM
    return (jnp.tile(jnp.concatenate([cos, cos], axis=-1), (1, reps)),
            jnp.tile(jnp.concatenate([-sin, sin], axis=-1), (1, reps)))


def _head_mean_matrix():
    blk = np.kron(np.eye(LANES // HEAD_DIM), np.full((HEAD_DIM, HEAD_DIM), 1.0 / HEAD_DIM))
    return jnp.asarray(np.concatenate([blk, blk], axis=0), BF16)


def _moe_layer(parts, hs_buf, router_wp, router_b, tri, expert_w, layer, tm):
    t = sum(p.shape[0] for p in parts)
    route, counts = _router(parts, router_wp, router_b, tri, tm)
    cls = route[0].astype(jnp.int32)
    rank = route[1].astype(jnp.int32)
    cnt = counts[:N_CLASSES, 0].astype(jnp.int32)
    padded = (cnt + MOE_BLOCK - 1) // MOE_BLOCK * MOE_BLOCK
    pad_end = jnp.cumsum(padded)
    pos = (pad_end - padded)[cls] + rank
    n_blocks = hs_buf.shape[0] // MOE_BLOCK
    assert n_blocks >= -(-t // MOE_BLOCK) + N_CLASSES
    n_used = pad_end[-1] // MOE_BLOCK
    blk = jnp.minimum(jnp.arange(n_blocks, dtype=jnp.int32), n_used - 1)
    bcls = jnp.sum((pad_end[None, :] <= (blk * MOE_BLOCK)[:, None]).astype(jnp.int32), axis=1)
    bcls = jnp.minimum(bcls, N_CLASSES - 1)
    e_lo = EXPERTS_PER_GROUP * (bcls // N_PAIRS) + jnp.asarray(PAIR_LO, jnp.int32)[bcls % N_PAIRS]
    e_hi = EXPERTS_PER_GROUP * (bcls // N_PAIRS) + jnp.asarray(PAIR_HI, jnp.int32)[bcls % N_PAIRS]
    hs = _dispatch(pos, parts, hs_buf, math.gcd(tm, ROW_TILE))
    ys = _moe(e_lo, e_hi, n_used.reshape(1).astype(jnp.int32), hs, *expert_w, layer)
    return pos, route, ys, hs


def kernel(x, c, ctx, c_ctx, w_mod, b_mod, norm_mix_g, norm_ffn_g, ev_w_in, ev_q_norm_g, ev_k_norm_g,
           ev_dw_w, ev_dw_b, ev_ln_g, ev_ln_b, ev_w_out, od_w_in, od_conv_w, od_w_out, router_w,
           router_bias, moe_w_gate, moe_w_up, moe_w_down):
    b, s, d = x.shape
    n_ctx = ctx.shape[1]
    depth = w_mod.shape[0]
    last_attn = (depth - 1) - ((depth - 1) % 2)
    tm = min(ROW_TILE, s)
    tmz = min(ROW_TILE, n_ctx)
    tq = min(Q_TILE, s)
    tqz = min(Q_TILE, n_ctx)
    assert s % tm == 0 and n_ctx % tmz == 0 and s % tq == 0 and n_ctx % tqz == 0 and s % GRID_W == 0

    r = -(-(b + 1) // 8) * 8
    c_all = jnp.zeros((r, d), F32).at[:b].set(c).at[b].set(c_ctx)
    mods = _modulation(c_all, w_mod, b_mod)

    def mod_x(l, j):
        return mods[l, :b, j * d:(j + 1) * d].reshape(b, 1, d)

    def mod_z(l, j):
        return mods[l, b:b + 1, j * d:(j + 1) * d].reshape(1, 1, d)

    cos_t, sin_t = _rope_tables(s)
    gmat = _head_mean_matrix()
    in_perm = _even_in_perm(ev_w_in.shape[2])
    out_perm = _even_out_perm(ev_w_out.shape[1])
    head_perm = np.asarray(_deinterleave(0), np.int32)
    router_wp = jnp.zeros((d, LANES), F32).at[:, :N_EXPERTS].set(router_w).astype(BF16)
    router_b = router_bias.astype(F32).reshape(N_EXPERTS, 1)
    tri_n = min(ROUTER_TILE, b * s)
    tri = jnp.asarray(np.triu(np.ones((tri_n, tri_n), np.float32), 1), BF16)

    xf = x.reshape(b * s, d)
    zf = ctx.reshape(b * n_ctx, d)
    expert_w = tuple(w.astype(BF16) for w in (moe_w_gate, moe_w_up, moe_w_down))
    pending = None
    ctm = min(Q_TILE, s, n_ctx)
    n_routed = b * s + (b * n_ctx if last_attn > 0 else 0)
    hs_buf = jnp.zeros(((-(-n_routed // MOE_BLOCK) + N_CLASSES) * MOE_BLOCK, d // 2), U32)
    for l in range(depth):
        i = l // 2
        ctx_in = l <= last_attn
        ctx_out = l < last_attn
        gmix = norm_mix_g[l].reshape(1, d)
        gffn = norm_ffn_g[l].reshape(1, d)
        if l % 2 == 0:
            w_in = ev_w_in[i][:, in_perm].astype(BF16)
            w_out = ev_w_out[i][out_perm, :].astype(BF16)
            gq = ev_q_norm_g[i][head_perm] * (ATTN_SCALE * LOG2E)
            gk = ev_k_norm_g[i][head_perm]
            gqk = jnp.concatenate([jnp.tile(gq, N_HEADS), jnp.tile(gk, N_KV_HEADS)]).reshape(1, QK_WIDTH)
            conv_params = [jnp.repeat(ev_dw_w[i], SUBLANES, axis=0), ev_dw_b[i].reshape(1, -1),
                           ev_ln_g[i].reshape(1, -1),
                           ev_ln_b[i].reshape(1, -1)]
            res = _inproj_even(xf, pending, gmix, mod_x(l, 0), mod_x(l, 1), s, w_in, gqk, cos_t, sin_t, gmat,
                               tm, True)
            if pending is not None:
                xf, res = res[0], res[1:]
            q, k, v, a = res
            k_all = k.reshape(b, s, KV_WIDTH)
            v_all = v.reshape(N_KV_HEADS, b, s, LANES)
            if ctx_in:
                qz, kz, vz, az = _inproj_even(zf, None, gmix, mod_z(l, 0), mod_z(l, 1), b * n_ctx, w_in, gqk,
                                              cos_t, sin_t, gmat, tmz, False)
                kz = kz.reshape(b, n_ctx, KV_WIDTH)
                vz = vz.reshape(N_KV_HEADS, b, n_ctx, LANES)
                k_all = jnp.concatenate([k_all, kz], axis=1)
                v_all = jnp.concatenate([v_all, vz], axis=2)
            attn = _attention(q, k_all, v_all, s, tq)
            xf, hp = _outproj(_outproj_even_kernel, "outproj_even", xf, [attn], a, conv_params, w_out,
                              mod_x(l, 2), gffn, mod_x(l, 3), mod_x(l, 4), s, s, tm, True)
            if ctx_out:
                attn_z = _attention(qz, kz, vz, n_ctx, tqz)
                zf, hpz = _outproj(_outproj_even_kernel, "outproj_even_ctx", zf, [attn_z], az, conv_params,
                                   w_out, mod_z(l, 2), gffn, mod_z(l, 3), mod_z(l, 4), b * n_ctx, n_ctx, tmz,
                                   True)
        else:
            w_in = od_w_in[i].astype(BF16)
            w_out = od_w_out[i].astype(BF16)
            conv_params = [jnp.repeat(od_conv_w[i], SUBLANES, axis=0)]
            res = _inproj_odd(xf, pending, gmix, mod_x(l, 0), mod_x(l, 1), s, w_in, tm)
            if pending is not None:
                xf, res = res[0], res[1:]
            gb, u = res
            xf, hp = _outproj(_outproj_odd_kernel, "outproj_odd", xf, [gb], u, conv_params, w_out,
                              mod_x(l, 2), gffn, mod_x(l, 3), mod_x(l, 4), s, s, tm, False)
            if ctx_out:
                gbz, uz = _inproj_odd(zf, None, gmix, mod_z(l, 0), mod_z(l, 1), b * n_ctx, w_in, tmz)
                zf, hpz = _outproj(_outproj_odd_kernel, "outproj_odd_ctx", zf, [gbz], uz, conv_params, w_out,
                                   mod_z(l, 2), gffn, mod_z(l, 3), mod_z(l, 4), b * n_ctx, n_ctx, tmz, False)

        parts = [hp, hpz] if ctx_out else [hp]
        rtm = math.gcd(tri_n, math.gcd(b * s, b * n_ctx)) if ctx_out else tri_n
        pos, rc, ys, hs_buf = _moe_layer(parts, hs_buf, router_wp, router_b, tri[:rtm, :rtm], expert_w, l, rtm)
        n_lat = b * s
        pending = (pos[:n_lat], rc, mod_x(l, 5), ys)
        if ctx_out:
            zf = _combine(pos[n_lat:], zf, rc, n_lat // ctm, mod_z(l, 5), ys, b * n_ctx, ctm)
    pos, rc, gate, ys = pending
    return _combine(pos, xf, rc, 0, gate, ys, s, ctm).reshape(b, s, d)
```

```python
import functools
import math

import numpy as np
import jax
import jax.numpy as jnp
from jax import lax
from jax.experimental import pallas as pl
from jax.experimental.pallas import tpu as pltpu

F32 = jnp.float32
BF16 = jnp.bfloat16
U32 = jnp.uint32

GRID_W = 64
EPS = 1e-6
N_HEADS = 8
N_KV_HEADS = 2
GQA_GROUP = N_HEADS // N_KV_HEADS
HEAD_DIM = 64
ATTN_WIDTH = N_HEADS * HEAD_DIM
KV_WIDTH = N_KV_HEADS * HEAD_DIM
QK_WIDTH = ATTN_WIDTH + KV_WIDTH
ATTN_SCALE = HEAD_DIM ** -0.5
ROPE_THETA = 10000.0
ROPE_PAIRS_PER_AXIS = HEAD_DIM // 4
CONV_A_K = 31
SC_K = 3
N_EXPERTS = 16
N_GROUPS = 4
EXPERTS_PER_GROUP = N_EXPERTS // N_GROUPS
D_EXPERT = 512
PAIR_LO = (0, 0, 0, 1, 1, 2)
PAIR_HI = (1, 2, 3, 2, 3, 3)
N_PAIRS = len(PAIR_LO)
N_CLASSES = N_GROUPS * N_PAIRS
CLASS_ROWS = 32

LANES = 128
SUBLANES = 8
MXU_COLS = 256
HALO = 16
VMEM_LIMIT = 48 * 1024 * 1024
ROW_TILE = 512
Q_TILE = 256
ROUTER_TILE = 1024
DMA_QUEUES = 2
MOE_BLOCK = 512
CONV_ROWS = 64
LOG2E = math.log2(math.e)


def _cparams(*sem):
    return pltpu.CompilerParams(dimension_semantics=sem, vmem_limit_bytes=VMEM_LIMIT)


def _sigmoid(x):
    return 1.0 / (1.0 + jnp.exp(-x))


def _rms_mod(x, g, shift, scale):
    y = x * lax.rsqrt(jnp.mean(x * x, axis=-1, keepdims=True) + EPS) * g
    return y * (1.0 + scale) + shift


def _pack_halves(h):
    w = h.shape[1] // 2
    lo = lax.bitcast_convert_type(h[:, :w].astype(BF16).astype(F32), U32)
    hi = lax.bitcast_convert_type(h[:, w:].astype(BF16).astype(F32), U32)
    return (lo >> 16) | (hi & jnp.uint32(0xFFFF0000))


def _unpack_halves(w):
    lo = lax.bitcast_convert_type(w << 16, F32)
    hi = lax.bitcast_convert_type(w & jnp.uint32(0xFFFF0000), F32)
    return lo, hi


def _issue_row_gathers(pos_ref, ys_ref, ybuf, sems, tile, slot, rows):
    tm = ybuf.shape[1]
    for r in rows:
        pltpu.async_copy(ys_ref.at[pl.ds(pos_ref[tile * tm + r], 1)], ybuf.at[slot, pl.ds(r, 1)],
                         sems.at[slot], priority=r % DMA_QUEUES)


def _wait_row_gathers(ys_ref, ybuf, sems, slot):
    pltpu.make_async_copy(ys_ref.at[pl.ds(0, ybuf.shape[1])], ybuf.at[slot], sems.at[slot]).wait()


def _combined_residual(pos_ref, x_ref, rc_ref, gate_ref, ys_ref, ybuf, sems):
    tm = ybuf.shape[1]
    i = pl.program_id(0)
    n = pl.num_programs(0)

    @pl.when(i == 0)
    def _():
        _issue_row_gathers(pos_ref, ys_ref, ybuf, sems, 0, 0, range(tm))

    _wait_row_gathers(ys_ref, ybuf, sems, i % 2)
    ya, yb = _unpack_halves(ybuf[i % 2])
    rc = rc_ref[...].T
    x_new = x_ref[...] + gate_ref[...] * (rc[:, 2:3] * ya + rc[:, 3:4] * yb)

    def issue(k, n_slices):
        _issue_row_gathers(pos_ref, ys_ref, ybuf, sems, jnp.minimum(i + 1, n - 1), (i + 1) % 2,
                           range(k * tm // n_slices, (k + 1) * tm // n_slices))
        if k == n_slices - 1:
            @pl.when(i == n - 1)
            def _():
                _wait_row_gathers(ys_ref, ybuf, sems, (i + 1) % 2)

    return x_new, issue


def _combine_in_specs(tm, d, route_rows, first_tile, rows_per_mod):
    return [pl.BlockSpec((tm, d), lambda i, pos: (i, 0)),
            pl.BlockSpec((route_rows, tm), lambda i, pos: (0, first_tile + i)),
            pl.BlockSpec((None, 1, d), lambda i, pos: ((i * tm) // rows_per_mod, 0, 0)),
            pl.BlockSpec(memory_space=pl.ANY)]


def _combine_scratch(tm, d):
    return [pltpu.VMEM((2, tm, d), U32), pltpu.SemaphoreType.DMA((2,))]


def _mod_kernel(c_ref, w_ref, b_ref, o_ref):
    c = c_ref[...]
    a = (c * _sigmoid(c)).astype(BF16)
    o_ref[...] = jnp.dot(a, w_ref[...].astype(BF16), preferred_element_type=F32) + b_ref[...]


def _modulation(c_all, w_mod, b_mod):
    depth, d, n = w_mod.shape
    r = c_all.shape[0]
    nt = n // d
    return pl.pallas_call(
        _mod_kernel,
        out_shape=jax.ShapeDtypeStruct((depth, r, n), F32),
        grid=(depth, nt),
        in_specs=[pl.BlockSpec((r, d), lambda l, j: (0, 0)),
                  pl.BlockSpec((None, d, d), lambda l, j: (l, 0, j)),
                  pl.BlockSpec((None, 1, d), lambda l, j: (l, 0, j))],
        out_specs=pl.BlockSpec((None, r, d), lambda l, j: (l, 0, j)),
        compiler_params=_cparams("parallel", "parallel"),
        name="modulation",
    )(c_all, w_mod, b_mod.reshape(depth, 1, n))


def _inproj_kernel(*refs, body, n_in, combine):
    if combine:
        pos_ref, x_ref, rc_ref, gate_ref, ys_ref = refs[:5]
        ybuf, sems = refs[-2:]
        rest = refs[5:-2]
        x, issue = _combined_residual(pos_ref, x_ref, rc_ref, gate_ref, ys_ref, ybuf, sems)
        rest[3 + n_in][...] = x
        ins, outs = rest[:3 + n_in], rest[4 + n_in:]
    else:
        x = refs[0][...]
        issue = lambda k, n_slices: None
        ins, outs = refs[1:4 + n_in], refs[4 + n_in:]
    h = _rms_mod(x, ins[0][...], ins[1][...], ins[2][...]).astype(BF16)
    body(h, issue, *ins[3:], *outs)


def _inproj(body, name, x, pending, g, shift, scale, rows_per_mod, extra, extra_specs, out_shape, out_specs,
            tm):
    t, d = x.shape
    mod_map = lambda i, *_: ((i * tm) // rows_per_mod, 0, 0)
    in_specs = [pl.BlockSpec((1, d), lambda i, *_: (0, 0)), pl.BlockSpec((None, 1, d), mod_map),
                pl.BlockSpec((None, 1, d), mod_map)] + list(extra_specs)
    kern = functools.partial(_inproj_kernel, body=body, n_in=len(extra), combine=pending is not None)
    if pending is None:
        return pl.pallas_call(
            kern, out_shape=tuple(out_shape), grid=(t // tm,),
            in_specs=[pl.BlockSpec((tm, d), lambda i: (i, 0))] + in_specs, out_specs=tuple(out_specs),
            compiler_params=_cparams("parallel"), name=name,
        )(x, g, shift, scale, *extra)
    pos, rc, gate, ys = pending
    return pl.pallas_call(
        kern,
        out_shape=(jax.ShapeDtypeStruct((t, d), F32),) + tuple(out_shape),
        grid_spec=pltpu.PrefetchScalarGridSpec(
            num_scalar_prefetch=1, grid=(t // tm,),
            in_specs=_combine_in_specs(tm, d, rc.shape[0], 0, rows_per_mod) + in_specs,
            out_specs=(pl.BlockSpec((tm, d), lambda i, pos: (i, 0)),) + tuple(out_specs),
            scratch_shapes=_combine_scratch(tm, d)),
        compiler_params=_cparams("arbitrary"), name=name + "_combine",
    )(pos, x, rc, gate, ys, g, shift, scale, *extra)


def _inproj_even_body(h, issue, w_ref, gqk_ref, cos_ref, sin_ref, gm_ref, q_ref, k_ref, v_ref, a_ref, *, rope):
    tm = h.shape[0]
    lane = lax.broadcasted_iota(jnp.int32, (tm, LANES), 1)
    first_head = lane < HEAD_DIM
    first_half = (lane % HEAD_DIM) < (HEAD_DIM // 2)
    c0 = QK_WIDTH + KV_WIDTH
    cw = (w_ref.shape[1] - c0) // 2
    issue(0, 1)
    p = jnp.dot(h, w_ref[...], preferred_element_type=F32)
    for m in range(QK_WIDTH // LANES):
        y = p[:, LANES * m:LANES * (m + 1)]
        sq = y * y
        sq_hi = sq.astype(BF16)
        sq_lo = (sq - sq_hi.astype(F32)).astype(BF16)
        ms = jnp.dot(jnp.concatenate([sq_hi, sq_lo], axis=1), gm_ref[...], preferred_element_type=F32)
        y = y * lax.rsqrt(ms + EPS) * gqk_ref[:, LANES * m:LANES * (m + 1)]
        if rope:
            partner = jnp.where(first_half, pltpu.roll(y, LANES - HEAD_DIM // 2, 1),
                                pltpu.roll(y, HEAD_DIM // 2, 1))
            y = y * cos_ref[...] + partner * sin_ref[...]
        if m < ATTN_WIDTH // LANES:
            q_ref[m] = jnp.where(first_head, y, 0.0).astype(BF16)
            q_ref[m + ATTN_WIDTH // LANES] = jnp.where(first_head, 0.0, y).astype(BF16)
        else:
            k_ref[...] = y.astype(BF16)
    v = p[:, QK_WIDTH:c0]
    v_ref[0] = jnp.where(first_head, v, 1.0).astype(BF16)
    v_ref[1] = jnp.where(first_head, 1.0, v).astype(BF16)
    a_ref[...] = (p[:, c0:c0 + cw] * _sigmoid(p[:, c0 + cw:])).astype(BF16)


def _inproj_even(x, pending, g, shift, scale, rows_per_mod, w, gqk, cos_t, sin_t, gmat, tm, rope):
    t, d = x.shape
    n = w.shape[1]
    cw = (n - QK_WIDTH - KV_WIDTH) // 2
    n_pos = cos_t.shape[0] // tm
    return _inproj(
        functools.partial(_inproj_even_body, rope=rope), "inproj_even", x, pending, g, shift, scale,
        rows_per_mod, [w, gqk, cos_t, sin_t, gmat],
        [pl.BlockSpec((d, n), lambda i, *_: (0, 0)),
         pl.BlockSpec((1, QK_WIDTH), lambda i, *_: (0, 0)),
         pl.BlockSpec((tm, LANES), lambda i, *_: (i % n_pos, 0)),
         pl.BlockSpec((tm, LANES), lambda i, *_: (i % n_pos, 0)),
         pl.BlockSpec((2 * LANES, LANES), lambda i, *_: (0, 0))],
        [jax.ShapeDtypeStruct((N_HEADS, t, LANES), BF16), jax.ShapeDtypeStruct((t, KV_WIDTH), BF16),
         jax.ShapeDtypeStruct((N_KV_HEADS, t, LANES), BF16), jax.ShapeDtypeStruct((t, cw), BF16)],
        [pl.BlockSpec((N_HEADS, tm, LANES), lambda i, *_: (0, i, 0)),
         pl.BlockSpec((tm, KV_WIDTH), lambda i, *_: (i, 0)),
         pl.BlockSpec((N_KV_HEADS, tm, LANES), lambda i, *_: (0, i, 0)),
         pl.BlockSpec((tm, cw), lambda i, *_: (i, 0))],
        tm)


def _inproj_odd_body(h, issue, w_ref, gb_ref, u_ref):
    w = w_ref.shape[1] // 3
    n_phase = w // MXU_COLS
    proj = lambda lo: jnp.dot(h, w_ref[:, lo:lo + MXU_COLS], preferred_element_type=F32)
    for c in range(n_phase):
        issue(c, n_phase)
        cols = slice(MXU_COLS * c, MXU_COLS * (c + 1))
        gb_ref[:, cols] = proj(MXU_COLS * c).astype(BF16)
        u_ref[:, cols] = (proj(w + MXU_COLS * c) * proj(2 * w + MXU_COLS * c)).astype(BF16)


def _inproj_odd(x, pending, g, shift, scale, rows_per_mod, w, tm):
    t, d = x.shape
    n = w.shape[1]
    return _inproj(
        _inproj_odd_body, "inproj_odd", x, pending, g, shift, scale, rows_per_mod, [w],
        [pl.BlockSpec((d, n), lambda i, *_: (0, 0))],
        [jax.ShapeDtypeStruct((t, n // 3), BF16), jax.ShapeDtypeStruct((t, n // 3), BF16)],
        [pl.BlockSpec((tm, n // 3), lambda i, *_: (i, 0)), pl.BlockSpec((tm, n // 3), lambda i, *_: (i, 0))],
        tm)


def _attn_kernel(q_ref, k_ref, v_ref, o_ref, acc_ref):
    tq = q_ref.shape[1]
    first = lax.broadcasted_iota(jnp.int32, (tq, LANES), 1) < HEAD_DIM

    def pair(m, carry):
        outs = []
        for g in range(N_KV_HEADS):
            q = q_ref[m + GQA_GROUP * g]
            s = lax.dot_general(q, k_ref[...], (((1,), (1,)), ((), ())), preferred_element_type=F32)
            p = jnp.exp2(s - jnp.max(s, axis=-1, keepdims=True)).astype(BF16)
            o = jnp.dot(p, v_ref[g], preferred_element_type=F32)
            l = o[:, HEAD_DIM:HEAD_DIM + 1] if g == 0 else o[:, 0:1]
            outs.append(o / l)
        acc_ref[m] = jnp.where(first, outs[0], outs[1])
        return carry

    for m in range(GQA_GROUP):
        pair(m, 0)
    o_ref[...] = jnp.concatenate([acc_ref[m] for m in range(GQA_GROUP)], axis=1).astype(BF16)


def _attention(q, k_all, v_all, s_q, tq):
    t = q.shape[1]
    b, skv, _ = k_all.shape
    nq = s_q // tq
    return pl.pallas_call(
        _attn_kernel,
        out_shape=jax.ShapeDtypeStruct((t, ATTN_WIDTH), BF16),
        grid=(b, nq),
        in_specs=[pl.BlockSpec((N_HEADS, tq, LANES), lambda bi, i: (0, bi * nq + i, 0)),
                  pl.BlockSpec((None, skv, LANES), lambda bi, i: (bi, 0, 0)),
                  pl.BlockSpec((N_KV_HEADS, None, skv, LANES), lambda bi, i: (0, bi, 0, 0))],
        out_specs=pl.BlockSpec((tq, ATTN_WIDTH), lambda bi, i: (bi * nq + i, 0)),
        scratch_shapes=[pltpu.VMEM((GQA_GROUP, tq, LANES), F32)],
        compiler_params=_cparams("parallel", "parallel"),
        name="attention",
    )(q, k_all, v_all)


def _fill_window(win_ref, cur_ref, prev_ref, next_ref, is_first, is_last):
    tm = cur_ref.shape[0]
    win_ref[0:HALO, :] = jnp.where(is_first, 0.0, prev_ref[...].astype(F32))
    win_ref[HALO:HALO + tm, :] = cur_ref[...].astype(F32)
    win_ref[HALO + tm:2 * HALO + tm, :] = jnp.where(is_last, 0.0, next_ref[...].astype(F32))


def _fill_shifted(win_ref, sh_ref):
    n = sh_ref.shape[1]
    for s in range(1, SUBLANES):
        sh_ref[s - 1] = win_ref[s:s + n, :]


def _dwconv_chunk(win_ref, sh_ref, w_ref, r0, rows, taps):
    off = HALO - taps // 2
    acc = None
    for kk in range(taps):
        q, s = divmod(off + kk, SUBLANES)
        if sh_ref is None:
            src = win_ref[r0 + off + kk:r0 + off + kk + rows, :]
        elif s == 0:
            src = win_ref[r0 + SUBLANES * q:r0 + SUBLANES * q + rows, :]
        else:
            src = sh_ref[s - 1, r0 + SUBLANES * q:r0 + SUBLANES * q + rows, :]
        term = src.reshape(rows // SUBLANES, SUBLANES, -1) * w_ref[SUBLANES * kk:SUBLANES * (kk + 1), :]
        acc = term if acc is None else acc + term
    return acc.reshape(rows, -1)


def _residual_tail(x_ref, o, gate_ref, g2_ref, sh2_ref, sc2_ref, xo_ref, hp_ref):
    x_new = x_ref[...] + gate_ref[...] * o
    xo_ref[...] = x_new
    hp_ref[...] = _pack_halves(_rms_mod(x_new, g2_ref[...], sh2_ref[...], sc2_ref[...]))


def _outproj_even_kernel(x_ref, at_ref, a_ref, ap_ref, an_ref, dw_ref, db_ref, lg_ref, lb_ref, w_ref,
                         gate_ref, g2_ref, sh2_ref, sc2_ref, xo_ref, hp_ref, win_ref, cv_ref, shw_ref,
                         *, tiles_per_seq):
    tm = x_ref.shape[0]
    i = pl.program_id(0) % tiles_per_seq
    _fill_window(win_ref, a_ref, ap_ref, an_ref, i == 0, i == tiles_per_seq - 1)
    _fill_shifted(win_ref, shw_ref)
    for r0 in range(0, tm, CONV_ROWS):
        y = _dwconv_chunk(win_ref, shw_ref, dw_ref, r0, CONV_ROWS, CONV_A_K) + db_ref[...]
        mu = jnp.mean(y, axis=-1, keepdims=True)
        yc = y - mu
        var = jnp.mean(yc * yc, axis=-1, keepdims=True)
        yn = yc * lax.rsqrt(var + EPS) * lg_ref[...] + lb_ref[...]
        cv_ref[r0:r0 + CONV_ROWS, :] = (yn * _sigmoid(yn)).astype(BF16)
    aw = at_ref.shape[1]
    o = (jnp.dot(at_ref[...], w_ref[0:aw, :], preferred_element_type=F32)
         + jnp.dot(cv_ref[...], w_ref[aw:, :], preferred_element_type=F32))
    _residual_tail(x_ref, o, gate_ref, g2_ref, sh2_ref, sc2_ref, xo_ref, hp_ref)


def _outproj_odd_kernel(x_ref, gb_ref, u_ref, up_ref, un_ref, cw_ref, w_ref,
                        gate_ref, g2_ref, sh2_ref, sc2_ref, xo_ref, hp_ref, win_ref, cv_ref,
                        *, tiles_per_seq):
    tm = x_ref.shape[0]
    i = pl.program_id(0) % tiles_per_seq
    _fill_window(win_ref, u_ref, up_ref, un_ref, i == 0, i == tiles_per_seq - 1)
    for r0 in range(0, tm, CONV_ROWS):
        y = _dwconv_chunk(win_ref, None, cw_ref, r0, CONV_ROWS, SC_K)
        cv_ref[r0:r0 + CONV_ROWS, :] = (gb_ref[r0:r0 + CONV_ROWS, :].astype(F32) * y).astype(BF16)
    o = jnp.dot(cv_ref[...], w_ref[...], preferred_element_type=F32)
    _residual_tail(x_ref, o, gate_ref, g2_ref, sh2_ref, sc2_ref, xo_ref, hp_ref)


def _halo_specs(tm, width, n_rows):
    hb = tm // HALO
    last = n_rows // HALO - 1
    return [pl.BlockSpec((tm, width), lambda i: (i, 0)),
            pl.BlockSpec((HALO, width), lambda i: (jnp.maximum(i * hb - 1, 0), 0)),
            pl.BlockSpec((HALO, width), lambda i: (jnp.minimum((i + 1) * hb, last), 0))]


def _outproj(kernel_fn, name, x, lead, conv_in, params, w, gate, g2, shift2, scale2, rows_per_mod,
             seq_len, tm, shifted_window):
    t, d = x.shape
    cw = conv_in.shape[1]
    scratch = [pltpu.VMEM((tm + 2 * HALO, cw), F32), pltpu.VMEM((tm, cw), BF16)]
    if shifted_window:
        scratch.append(pltpu.VMEM((SUBLANES - 1, tm + 2 * HALO - SUBLANES, cw), F32))
    mod_map = lambda i: ((i * tm) // rows_per_mod, 0, 0)
    row_spec = lambda a: pl.BlockSpec((tm, a.shape[1]), lambda i: (i, 0))
    full_spec = lambda a: pl.BlockSpec(a.shape, lambda i: (0,) * a.ndim)
    in_specs = ([row_spec(x)] + [row_spec(a) for a in lead] + _halo_specs(tm, cw, t)
                + [full_spec(a) for a in params] + [full_spec(w)]
                + [pl.BlockSpec((None, 1, d), mod_map), full_spec(g2),
                   pl.BlockSpec((None, 1, d), mod_map), pl.BlockSpec((None, 1, d), mod_map)])
    return pl.pallas_call(
        functools.partial(kernel_fn, tiles_per_seq=seq_len // tm),
        out_shape=(jax.ShapeDtypeStruct((t, d), F32), jax.ShapeDtypeStruct((t, d // 2), U32)),
        grid=(t // tm,),
        in_specs=in_specs,
        out_specs=(pl.BlockSpec((tm, d), lambda i: (i, 0)), pl.BlockSpec((tm, d // 2), lambda i: (i, 0))),
        scratch_shapes=scratch,
        compiler_params=_cparams("parallel"),
        name=name,
    )(x, *lead, conv_in, conv_in, conv_in, *params, w, gate, g2, shift2, scale2)


def _part_tile_maps(part_tiles):
    maps, start = [], 0
    for n in part_tiles:
        maps.append(lambda i, *_, start=start, n=n: (jnp.clip(i - start, 0, n - 1), 0))
        start += n
    return maps


def _router_kernel(*refs, part_tiles):
    hp_refs = refs[:len(part_tiles)]
    rw_ref, bias_ref, tri_ref, route_ref, cnt_ref, base_ref = refs[len(part_tiles):]
    tm = hp_refs[0].shape[0]

    @pl.when(pl.program_id(0) == 0)
    def _():
        base_ref[...] = jnp.zeros_like(base_ref)

    hp, start = hp_refs[0][...], 0
    for k in range(1, len(part_tiles)):
        start += part_tiles[k - 1]
        hp = jnp.where(pl.program_id(0) >= start, hp_refs[k][...], hp)
    lo, hi = _unpack_halves(hp)
    h = jnp.concatenate([lo, hi], axis=1).astype(BF16)
    logits = jnp.dot(h, rw_ref[...], preferred_element_type=F32)
    lt = logits.T[0:N_EXPERTS, :]
    aff = _sigmoid(lt)
    sel = aff + bias_ref[...]
    rows = lambda a, e: a[e:e + 1, :]

    best = None
    for g in range(N_GROUPS):
        a, b, c, d = (rows(sel, EXPERTS_PER_GROUP * g + j) for j in range(EXPERTS_PER_GROUP))
        score = jnp.maximum(jnp.maximum(a + b, c + d), jnp.maximum(a, b) + jnp.maximum(c, d))
        if best is None:
            best, grp = score, jnp.zeros((1, tm), jnp.int32)
        else:
            better = score > best
            grp = jnp.where(better, g, grp)
            best = jnp.where(better, score, best)

    def in_group(a, j):
        out = rows(a, j)
        for g in range(1, N_GROUPS):
            out = jnp.where(grp == g, rows(a, EXPERTS_PER_GROUP * g + j), out)
        return out

    s = [in_group(sel, j) for j in range(EXPERTS_PER_GROUP)]
    f = [in_group(aff, j) for j in range(EXPERTS_PER_GROUP)]

    def first_argmax(vals):
        m = functools.reduce(jnp.maximum, vals)
        idx = jnp.full((1, tm), len(vals) - 1, jnp.int32)
        for j in range(len(vals) - 2, -1, -1):
            idx = jnp.where(vals[j] == m, j, idx)
        return idx

    i1 = first_argmax(s)
    i2 = first_argmax([jnp.where(i1 == j, -jnp.inf, s[j]) for j in range(EXPERTS_PER_GROUP)])
    e_lo = jnp.minimum(i1, i2)
    e_hi = jnp.maximum(i1, i2)
    pick = lambda vals, idx: functools.reduce(
        lambda acc, j: jnp.where(idx == j, vals[j], acc), range(1, len(vals)), vals[0])
    f_lo, f_hi = pick(f, e_lo), pick(f, e_hi)
    denom = f_lo + f_hi
    pair = jnp.where(e_lo == 0, e_hi - 1, jnp.where(e_lo == 1, e_hi + 1, N_PAIRS - 1))
    cls = grp * N_PAIRS + pair

    onehot = (lax.broadcasted_iota(jnp.int32, (CLASS_ROWS, tm), 0) == cls)
    onehot_f = onehot.astype(F32)
    before = jnp.dot(onehot_f.astype(BF16), tri_ref[...], preferred_element_type=F32) + base_ref[:, 0:1]
    rank = jnp.sum(onehot_f * before, axis=0, keepdims=True)
    base_ref[...] = base_ref[...] + jnp.sum(onehot_f, axis=1, keepdims=True)

    route_ref[...] = jnp.concatenate(
        [cls.astype(F32), rank, f_lo / denom, f_hi / denom, jnp.zeros((4, tm), F32)], axis=0)
    cnt_ref[...] = base_ref[...]


def _router(parts, rw, bias, tri, tm):
    hw = parts[0].shape[1]
    part_tiles = tuple(p.shape[0] // tm for p in parts)
    t = tm * sum(part_tiles)
    return pl.pallas_call(
        functools.partial(_router_kernel, part_tiles=part_tiles),
        out_shape=(jax.ShapeDtypeStruct((8, t), F32), jax.ShapeDtypeStruct((CLASS_ROWS, LANES), F32)),
        grid=(t // tm,),
        in_specs=[pl.BlockSpec((tm, hw), m) for m in _part_tile_maps(part_tiles)] + [
                  pl.BlockSpec(rw.shape, lambda i: (0, 0)),
                  pl.BlockSpec(bias.shape, lambda i: (0, 0)),
                  pl.BlockSpec((tm, tm), lambda i: (0, 0))],
        out_specs=(pl.BlockSpec((8, tm), lambda i: (0, i)),
                   pl.BlockSpec((CLASS_ROWS, LANES), lambda i: (0, 0))),
        scratch_shapes=[pltpu.VMEM((CLASS_ROWS, LANES), F32)],
        compiler_params=_cparams("arbitrary"),
        name="router",
    )(*parts, rw, bias, tri)


def _dispatch_kernel(pos_ref, *refs, part_tiles):
    hp_refs = refs[:len(part_tiles)]
    init_ref, hs_ref, stage, sems = refs[len(part_tiles):]
    del init_ref
    tm = hp_refs[0].shape[0]
    n_steps = sum(part_tiles)
    i = pl.program_id(0)
    slot = i % 2

    def wait(s):
        pltpu.make_async_copy(stage.at[s], hs_ref.at[pl.ds(0, tm)], sems.at[s]).wait()

    @pl.when(i >= 2)
    def _():
        wait(slot)

    start = 0
    for hp_ref, n in zip(hp_refs, part_tiles):
        @pl.when((i >= start) & (i < start + n))
        def _(hp_ref=hp_ref):
            stage[slot] = hp_ref[...]
        start += n
    for r in range(tm):
        pltpu.async_copy(stage.at[slot, pl.ds(r, 1)], hs_ref.at[pl.ds(pos_ref[i * tm + r], 1)], sems.at[slot],
                         priority=r % DMA_QUEUES)

    @pl.when(i == n_steps - 1)
    def _():
        wait(slot)
        if n_steps >= 2:
            wait(1 - slot)


def _dispatch(pos, parts, hs_init, tm):
    hw = parts[0].shape[1]
    part_tiles = tuple(p.shape[0] // tm for p in parts)
    n_rows = hs_init.shape[0]
    return pl.pallas_call(
        functools.partial(_dispatch_kernel, part_tiles=part_tiles),
        out_shape=jax.ShapeDtypeStruct((n_rows, hw), U32),
        grid_spec=pltpu.PrefetchScalarGridSpec(
            num_scalar_prefetch=1,
            grid=(sum(part_tiles),),
            in_specs=[pl.BlockSpec((tm, hw), m) for m in _part_tile_maps(part_tiles)]
            + [pl.BlockSpec(memory_space=pl.ANY)],
            out_specs=pl.BlockSpec(memory_space=pl.ANY),
            scratch_shapes=[pltpu.VMEM((2, tm, hw), U32), pltpu.SemaphoreType.DMA((2,))]),
        input_output_aliases={1 + len(parts): 0},
        compiler_params=_cparams("arbitrary"),
        name="dispatch",
    )(pos, *parts, hs_init)


def _moe_kernel(elo_ref, ehi_ref, nused_ref, hs_ref, wg_lo, wu_lo, wd_lo, wg_hi, wu_hi, wd_hi, ys_ref):
    del elo_ref, ehi_ref

    @pl.when(pl.program_id(0) >= nused_ref[0])
    def _():
        ys_ref[...] = jnp.zeros_like(ys_ref)

    @pl.when(pl.program_id(0) < nused_ref[0])
    def _():
        lo, hi = _unpack_halves(hs_ref[...])
        h = jnp.concatenate([lo, hi], axis=1).astype(BF16)

        def expert(wg_ref, wu_ref, wd_ref):
            g = jnp.dot(h, wg_ref[...], preferred_element_type=F32)
            u = jnp.dot(h, wu_ref[...], preferred_element_type=F32)
            act = (g * _sigmoid(g) * u).astype(BF16)
            return jnp.dot(act, wd_ref[...], preferred_element_type=F32)

        ya = lax.bitcast_convert_type(expert(wg_lo, wu_lo, wd_lo).astype(BF16).astype(F32), U32)
        yb = lax.bitcast_convert_type(expert(wg_hi, wu_hi, wd_hi).astype(BF16).astype(F32), U32)
        ys_ref[...] = (ya >> 16) | (yb & jnp.uint32(0xFFFF0000))


def _moe(e_lo, e_hi, n_used, hs, wg, wu, wd, layer):
    n_rows, hw = hs.shape
    d = wd.shape[3]
    n_blocks = n_rows // MOE_BLOCK
    row_map = lambda j, lo, hi, nu: (jnp.minimum(j, nu[0] - 1), 0)
    w_spec = lambda w, which: pl.BlockSpec(
        (None, None) + w.shape[2:], lambda j, lo, hi, nu: (layer, (lo, hi)[which][j], 0, 0))
    return pl.pallas_call(
        _moe_kernel,
        out_shape=jax.ShapeDtypeStruct((n_rows, d), U32),
        grid_spec=pltpu.PrefetchScalarGridSpec(
            num_scalar_prefetch=3,
            grid=(n_blocks,),
            in_specs=[pl.BlockSpec((MOE_BLOCK, hw), row_map)]
            + [w_spec(w, which) for which in (0, 1) for w in (wg, wu, wd)],
            out_specs=pl.BlockSpec((MOE_BLOCK, d), lambda j, lo, hi, nu: (j, 0))),
        compiler_params=_cparams("arbitrary"),
        name="experts",
    )(e_lo, e_hi, n_used, hs, wg, wu, wd, wg, wu, wd)


def _combine_kernel(pos_ref, x_ref, rc_ref, gate_ref, ys_ref, xo_ref, ybuf, sems):
    x_new, issue = _combined_residual(pos_ref, x_ref, rc_ref, gate_ref, ys_ref, ybuf, sems)
    xo_ref[...] = x_new
    issue(0, 1)


def _combine(pos, x, rc, first_tile, gate, ys, rows_per_mod, tm):
    t, d = x.shape
    return pl.pallas_call(
        _combine_kernel,
        out_shape=jax.ShapeDtypeStruct((t, d), F32),
        grid_spec=pltpu.PrefetchScalarGridSpec(
            num_scalar_prefetch=1,
            grid=(t // tm,),
            in_specs=_combine_in_specs(tm, d, rc.shape[0], first_tile, rows_per_mod),
            out_specs=pl.BlockSpec((tm, d), lambda i, pos: (i, 0)),
            scratch_shapes=_combine_scratch(tm, d)),
        compiler_params=_cparams("arbitrary"),
        name="combine",
    )(pos, x, rc, gate, ys)


def _deinterleave(base):
    return [base + 2 * p for p in range(HEAD_DIM // 2)] + [base + 2 * p + 1 for p in range(HEAD_DIM // 2)]


def _even_in_perm(n_cols):
    cols = []
    half = N_HEADS // N_KV_HEADS
    for m in range(half):
        cols += _deinterleave(HEAD_DIM * m) + _deinterleave(HEAD_DIM * (m + half))
    for kh in range(N_KV_HEADS):
        cols += _deinterleave(ATTN_WIDTH + HEAD_DIM * kh)
    cols += list(range(QK_WIDTH, n_cols))
    return np.asarray(cols, np.int32)


def _even_out_perm(n_rows):
    rows = []
    half = N_HEADS // N_KV_HEADS
    for m in range(half):
        rows += list(range(HEAD_DIM * m, HEAD_DIM * (m + 1)))
        rows += list(range(HEAD_DIM * (m + half), HEAD_DIM * (m + half + 1)))
    rows += list(range(ATTN_WIDTH, n_rows))
    return np.asarray(rows, np.int32)


def _rope_tables(n_tokens):
    rows = n_tokens // GRID_W
    row = jnp.repeat(jnp.arange(rows, dtype=F32), GRID_W)
    col = jnp.tile(jnp.arange(GRID_W, dtype=F32), rows)
    inv = ROPE_THETA ** (-jnp.arange(ROPE_PAIRS_PER_AXIS, dtype=F32) / ROPE_PAIRS_PER_AXIS)
    ang = jnp.concatenate([row[:, None] * inv, col[:, None] * inv], axis=-1)
    cos, sin = jnp.cos(ang), jnp.sin(ang)
    reps = LANES // HEAD_DIM
    return (jnp.tile(jnp.concatenate([cos, cos], axis=-1), (1, reps)),
            jnp.tile(jnp.concatenate([-sin, sin], axis=-1), (1, reps)))


def _head_mean_matrix():
    blk = np.kron(np.eye(LANES // HEAD_DIM), np.full((HEAD_DIM, HEAD_DIM), 1.0 / HEAD_DIM))
    return jnp.asarray(np.concatenate([blk, blk], axis=0), BF16)


def _moe_layer(parts, hs_buf, router_wp, router_b, tri, expert_w, layer, tm):
    t = sum(p.shape[0] for p in parts)
    route, counts = _router(parts, router_wp, router_b, tri, tm)
    cls = route[0].astype(jnp.int32)
    rank = route[1].astype(jnp.int32)
    cnt = counts[:N_CLASSES, 0].astype(jnp.int32)
    padded = (cnt + MOE_BLOCK - 1) // MOE_BLOCK * MOE_BLOCK
    pad_end = jnp.cumsum(padded)
    pos = (pad_end - padded)[cls] + rank
    n_blocks = hs_buf.shape[0] // MOE_BLOCK
    assert n_blocks >= -(-t // MOE_BLOCK) + N_CLASSES
    n_used = pad_end[-1] // MOE_BLOCK
    blk = jnp.minimum(jnp.arange(n_blocks, dtype=jnp.int32), n_used - 1)
    bcls = jnp.sum((pad_end[None, :] <= (blk * MOE_BLOCK)[:, None]).astype(jnp.int32), axis=1)
    bcls = jnp.minimum(bcls, N_CLASSES - 1)
    e_lo = EXPERTS_PER_GROUP * (bcls // N_PAIRS) + jnp.asarray(PAIR_LO, jnp.int32)[bcls % N_PAIRS]
    e_hi = EXPERTS_PER_GROUP * (bcls // N_PAIRS) + jnp.asarray(PAIR_HI, jnp.int32)[bcls % N_PAIRS]
    hs = _dispatch(pos, parts, hs_buf, math.gcd(tm, ROW_TILE))
    ys = _moe(e_lo, e_hi, n_used.reshape(1).astype(jnp.int32), hs, *expert_w, layer)
    return pos, route, ys, hs


def kernel(x, c, ctx, c_ctx, w_mod, b_mod, norm_mix_g, norm_ffn_g, ev_w_in, ev_q_norm_g, ev_k_norm_g,
           ev_dw_w, ev_dw_b, ev_ln_g, ev_ln_b, ev_w_out, od_w_in, od_conv_w, od_w_out, router_w,
           router_bias, moe_w_gate, moe_w_up, moe_w_down):
    b, s, d = x.shape
    n_ctx = ctx.shape[1]
    depth = w_mod.shape[0]
    last_attn = (depth - 1) - ((depth - 1) % 2)
    tm = min(ROW_TILE, s)
    tmz = min(ROW_TILE, n_ctx)
    tq = min(Q_TILE, s)
    tqz = min(Q_TILE, n_ctx)
    assert s % tm == 0 and n_ctx % tmz == 0 and s % tq == 0 and n_ctx % tqz == 0 and s % GRID_W == 0

    r = -(-(b + 1) // 8) * 8
    c_all = jnp.zeros((r, d), F32).at[:b].set(c).at[b].set(c_ctx)
    mods = _modulation(c_all, w_mod, b_mod)

    def mod_x(l, j):
        return mods[l, :b, j * d:(j + 1) * d].reshape(b, 1, d)

    def mod_z(l, j):
        return mods[l, b:b + 1, j * d:(j + 1) * d].reshape(1, 1, d)

    cos_t, sin_t = _rope_tables(s)
    gmat = _head_mean_matrix()
    in_perm = _even_in_perm(ev_w_in.shape[2])
    out_perm = _even_out_perm(ev_w_out.shape[1])
    head_perm = np.asarray(_deinterleave(0), np.int32)
    router_wp = jnp.zeros((d, LANES), F32).at[:, :N_EXPERTS].set(router_w).astype(BF16)
    router_b = router_bias.astype(F32).reshape(N_EXPERTS, 1)
    tri_n = min(ROUTER_TILE, b * s)
    tri = jnp.asarray(np.triu(np.ones((tri_n, tri_n), np.float32), 1), BF16)

    xf = x.reshape(b * s, d)
    zf = ctx.reshape(b * n_ctx, d)
    expert_w = tuple(w.astype(BF16) for w in (moe_w_gate, moe_w_up, moe_w_down))
    pending = None
    ctm = min(Q_TILE, s, n_ctx)
    n_routed = b * s + (b * n_ctx if last_attn > 0 else 0)
    hs_buf = jnp.zeros(((-(-n_routed // MOE_BLOCK) + N_CLASSES) * MOE_BLOCK, d // 2), U32)
    for l in range(depth):
        i = l // 2
        ctx_in = l <= last_attn
        ctx_out = l < last_attn
        gmix = norm_mix_g[l].reshape(1, d)
        gffn = norm_ffn_g[l].reshape(1, d)
        if l % 2 == 0:
            w_in = ev_w_in[i][:, in_perm].astype(BF16)
            w_out = ev_w_out[i][out_perm, :].astype(BF16)
            gq = ev_q_norm_g[i][head_perm] * (ATTN_SCALE * LOG2E)
            gk = ev_k_norm_g[i][head_perm]
            gqk = jnp.concatenate([jnp.tile(gq, N_HEADS), jnp.tile(gk, N_KV_HEADS)]).reshape(1, QK_WIDTH)
            conv_params = [jnp.repeat(ev_dw_w[i], SUBLANES, axis=0), ev_dw_b[i].reshape(1, -1),
                           ev_ln_g[i].reshape(1, -1),
                           ev_ln_b[i].reshape(1, -1)]
            res = _inproj_even(xf, pending, gmix, mod_x(l, 0), mod_x(l, 1), s, w_in, gqk, cos_t, sin_t, gmat,
                               tm, True)
            if pending is not None:
                xf, res = res[0], res[1:]
            q, k, v, a = res
            k_all = k.reshape(b, s, KV_WIDTH)
            v_all = v.reshape(N_KV_HEADS, b, s, LANES)
            if ctx_in:
                qz, kz, vz, az = _inproj_even(zf, None, gmix, mod_z(l, 0), mod_z(l, 1), b * n_ctx, w_in, gqk,
                                              cos_t, sin_t, gmat, tmz, False)
                kz = kz.reshape(b, n_ctx, KV_WIDTH)
                vz = vz.reshape(N_KV_HEADS, b, n_ctx, LANES)
                k_all = jnp.concatenate([k_all, kz], axis=1)
                v_all = jnp.concatenate([v_all, vz], axis=2)
            attn = _attention(q, k_all, v_all, s, tq)
            xf, hp = _outproj(_outproj_even_kernel, "outproj_even", xf, [attn], a, conv_params, w_out,
                              mod_x(l, 2), gffn, mod_x(l, 3), mod_x(l, 4), s, s, tm, True)
            if ctx_out:
                attn_z = _attention(qz, kz, vz, n_ctx, tqz)
                zf, hpz = _outproj(_outproj_even_kernel, "outproj_even_ctx", zf, [attn_z], az, conv_params,
                                   w_out, mod_z(l, 2), gffn, mod_z(l, 3), mod_z(l, 4), b * n_ctx, n_ctx, tmz,
                                   True)
        else:
            w_in = od_w_in[i].astype(BF16)
            w_out = od_w_out[i].astype(BF16)
            conv_params = [jnp.repeat(od_conv_w[i], SUBLANES, axis=0)]
            res = _inproj_odd(xf, pending, gmix, mod_x(l, 0), mod_x(l, 1), s, w_in, tm)
            if pending is not None:
                xf, res = res[0], res[1:]
            gb, u = res
            xf, hp = _outproj(_outproj_odd_kernel, "outproj_odd", xf, [gb], u, conv_params, w_out,
                              mod_x(l, 2), gffn, mod_x(l, 3), mod_x(l, 4), s, s, tm, False)
            if ctx_out:
                gbz, uz = _inproj_odd(zf, None, gmix, mod_z(l, 0), mod_z(l, 1), b * n_ctx, w_in, tmz)
                zf, hpz = _outproj(_outproj_odd_kernel, "outproj_odd_ctx", zf, [gbz], uz, conv_params, w_out,
                                   mod_z(l, 2), gffn, mod_z(l, 3), mod_z(l, 4), b * n_ctx, n_ctx, tmz, False)

        parts = [hp, hpz] if ctx_out else [hp]
        rtm = math.gcd(tri_n, math.gcd(b * s, b * n_ctx)) if ctx_out else tri_n
        pos, rc, ys, hs_buf = _moe_layer(parts, hs_buf, router_wp, router_b, tri[:rtm, :rtm], expert_w, l, rtm)
        n_lat = b * s
        pending = (pos[:n_lat], rc, mod_x(l, 5), ys)
        if ctx_out:
            zf = _combine(pos[n_lat:], zf, rc, n_lat // ctm, mod_z(l, 5), ys, b * n_ctx, ctm)
    pos, rc, gate, ys = pending
    return _combine(pos, xf, rc, 0, gate, ys, s, ctm).reshape(b, s, d)
```
